```python
import math
import jax, jax.numpy as jnp
from jax import lax
import numpy as np

D_MODEL = 1024
BATCH = 16
SEQ = 256
DEPTH = 2
DEC_BATCH = 8
DEC_SEQ = 1024
PAST_LEN = 512

GRID_W = 64
HG_WIDTH = D_MODEL // 2
HG_DK = 128
HG_HEADS = HG_WIDTH // HG_DK
HG_DV = HG_WIDTH // HG_HEADS
HY_WIDTH = D_MODEL - HG_WIDTH
N_IN = 5 * HG_WIDTH + 3 * HY_WIDTH
CHUNK = 64
CONV_W = 3
D_FF = 2816
HY_BANDS = 16
HY_EMB = 1 + 2 * HY_BANDS
HY_ORDER = 64
HY_TARGET = 1e-2
HY_FAST = 0.3
HY_SLOW = 1.5
EPS = 1e-6

kernel_name = "hybrid_hgrn2_hyena_diffusion_step"

F32 = jnp.float32


def _rmsnorm(x, w):
    xf = x.astype(F32)
    y = xf * lax.rsqrt(jnp.mean(xf * xf, axis=-1, keepdims=True) + EPS)
    return (y * w.astype(F32)).astype(x.dtype)


def _dwconv3(x, w, rows):
    B, L, C = x.shape
    xr = x.reshape(B, rows, L // rows, C)
    xp = jnp.pad(xr, ((0, 0), (0, 0), (1, 1), (0, 0)))
    y = w[0] * xp[:, :, :-2] + w[1] * xp[:, :, 1:-1] + w[2] * xp[:, :, 2:]
    return y.reshape(B, L, C)


def _lower_bounds(p):
    s = jax.nn.softmax(p.astype(F32), axis=0)
    cs = jnp.cumsum(s, axis=0)
    return cs - cs[0:1]


def _log_forget(x, lb):
    return jnp.logaddexp(jnp.log(lb), jnp.log1p(-lb) + jax.nn.log_sigmoid(x.astype(F32)))


def _chunk_scan(q, k, v, log_f, s0):
    B, L, H, DK = q.shape
    DV = v.shape[-1]
    nc = L // CHUNK

    def to_chunks(a):
        return a.reshape(B, nc, CHUNK, H, a.shape[-1]).transpose(1, 0, 3, 2, 4)

    qc, kc, vc, gc = to_chunks(q), to_chunks(k), to_chunks(v), to_chunks(log_f)
    mask = jnp.tril(jnp.ones((CHUNK, CHUNK), dtype=bool))[None, None, :, :, None]

    def step(S, inp):
        qi, ki, vi, gi = inp
        b = jnp.cumsum(gi, axis=2)
        diff = b[:, :, :, None, :] - b[:, :, None, :, :]
        decay = jnp.exp(jnp.where(mask, diff, -jnp.inf))
        scores = jnp.einsum('bhjd,bhid,bhjid->bhji', qi, ki, decay)
        o = jnp.einsum('bhji,bhiv->bhjv', scores, vi) + jnp.einsum('bhjd,bhdv->bhjv', qi * jnp.exp(b), S)
        b_last = b[:, :, -1:, :]
        S_new = jnp.exp(b_last[:, :, 0, :, None]) * S + jnp.einsum('bhid,bhiv->bhdv', ki * jnp.exp(b_last - b), vi)
        return S_new, o

    s_fin, oc = lax.scan(step, s0.astype(F32), (qc, kc, vc, gc))
    o = oc.transpose(1, 0, 3, 2, 4).reshape(B, L, H, DV)
    return o, s_fin


def _hyena_filter(L, w1, b1, fr1, w2, b2, fr2, w3):
    t = jnp.linspace(0.0, 1.0, L, dtype=F32)[:, None]
    pos = jnp.arange(L, dtype=F32)[:, None]
    bands = jnp.linspace(1e-4, HY_BANDS - 1, HY_BANDS, dtype=F32)[None, :]
    ang = 2.0 * math.pi * pos * bands / L
    feats = jnp.concatenate([t, jnp.cos(ang), jnp.sin(ang)], axis=-1)
    h = jnp.sin(fr1.astype(F32) * (feats @ w1.astype(F32) + b1.astype(F32)))
    h = jnp.sin(fr2.astype(F32) * (h @ w2.astype(F32) + b2.astype(F32)))
    h = h @ w3.astype(F32)
    min_decay = math.log(HY_TARGET) / HY_SLOW
    max_decay = math.log(HY_TARGET) / HY_FAST
    deltas = jnp.abs(jnp.linspace(min_decay, max_decay, HY_WIDTH, dtype=F32))
    deltas = jnp.concatenate([deltas, deltas])
    h = h * jnp.exp(-t * deltas[None, :])
    h_f, h_b = h[:, :HY_WIDTH], h[:, HY_WIDTH:]
    kern = jnp.concatenate([h_f, jnp.zeros((1, HY_WIDTH), F32), h_b[1:][::-1]], axis=0)
    return kern / (jnp.sum(jnp.abs(kern), axis=0, keepdims=True) + EPS)


def _long_conv(u, kern):
    L = u.shape[1]
    uf = jnp.fft.rfft(u.astype(F32), n=2 * L, axis=1)
    kf = jnp.fft.rfft(kern, n=2 * L, axis=0)
    return jnp.fft.irfft(uf * kf[None], n=2 * L, axis=1)[:, :L]


def _mixer(h, rows, s0, l, p):
    B, L, _ = h.shape
    proj = jnp.einsum('bld,de->ble', h, p['w_in'][l])
    cuts = [HG_WIDTH * i for i in range(1, 6)]
    q, ff, fb, iv, g, hy = jnp.split(proj, cuts, axis=-1)

    def heads(a):
        return a.reshape(B, L, HG_HEADS, -1)

    qh = heads(jax.nn.silu(q.astype(F32)) * (HG_DK ** -0.5))
    vh = heads(iv.astype(F32))
    logf_f = heads(_log_forget(ff, _lower_bounds(p['hg_lb_fwd'])[l]))
    logf_b = heads(_log_forget(fb, _lower_bounds(p['hg_lb_bwd'])[l]))
    k_f = -jnp.expm1(logf_f)
    k_b = -jnp.expm1(logf_b)
    o_f, s_f = _chunk_scan(qh, k_f, vh, logf_f, s0[:, 0])
    rev = lambda a: a[:, ::-1]
    o_b, s_b = _chunk_scan(rev(qh), rev(k_b), rev(vh), rev(logf_b), s0[:, 1])
    o = o_f + rev(o_b)
    o = o * lax.rsqrt(jnp.mean(o * o, axis=-1, keepdims=True) + EPS)
    o = o * p['hg_norm_w'][l].astype(F32).reshape(HG_HEADS, HG_DV)
    o_hg = o.reshape(B, L, HG_WIDTH) * jax.nn.silu(g.astype(F32))

    hyc = _dwconv3(hy, p['hy_conv_w'][l], rows).astype(F32)
    v, x1, x2 = jnp.split(hyc, [HY_WIDTH, 2 * HY_WIDTH], axis=-1)
    kern = _hyena_filter(L, p['hy_w1'][l], p['hy_b1'][l], p['hy_freq1'][l],
                         p['hy_w2'][l], p['hy_b2'][l], p['hy_freq2'][l], p['hy_w3'][l])
    u = x1 * v
    z = _long_conv(u, kern) + p['hy_d'][l].astype(F32) * u
    y_hy = x2 * z
    y_hy = y_hy * lax.rsqrt(jnp.mean(y_hy * y_hy, axis=-1, keepdims=True) + EPS) * p['hy_norm_w'][l].astype(F32)

    merged = jnp.concatenate([o_hg, y_hy], axis=-1).astype(h.dtype)
    out = jnp.einsum('ble,ed->bld', merged, p['w_out'][l])
    return out, jnp.stack([s_f, s_b], axis=1)


def _convffn(h, rows, l, p):
    up = jnp.einsum('bld,df->blf', h, p['ffn_w_up'][l])
    up = _dwconv3(up, p['ffn_conv_w'][l], rows)
    gate, val = jnp.split(up, 2, axis=-1)
    return jnp.einsum('blf,fd->bld', jax.nn.silu(gate) * val, p['ffn_w_down'][l])


def _layer(x, cvec, rows, s0, l, p):
    mod = jnp.einsum('bd,de->be', jax.nn.silu(cvec), p['ada_w'][l]) + p['ada_b'][l]
    sh1, sc1, g1, sh2, sc2, g2 = [m[:, None, :] for m in jnp.split(mod, 6, axis=-1)]
    h = _rmsnorm(x, p['norm1_w'][l]) * (1 + sc1) + sh1
    mix, s_fin = _mixer(h, rows, s0, l, p)
    x = x + g1 * mix
    h = _rmsnorm(x, p['norm2_w'][l]) * (1 + sc2) + sh2
    x = x + g2 * _convffn(h, rows, l, p)
    return x.astype(cvec.dtype), s_fin


def setup_inputs(seed: int = 0) -> dict:
    key = jax.random.key(seed)
    ks = jax.random.split(key, 32)
    n = lambda k, s, sc: jax.random.normal(k, s, F32) * sc
    D = D_MODEL
    return {
        "x_prompt": n(ks[0], (BATCH, SEQ, D), 1.0),
        "x_sample": n(ks[1], (DEC_BATCH, DEC_SEQ, D), 1.0),
        "state_hgrn": n(ks[2], (DEC_BATCH, DEPTH, 2, HG_HEADS, HG_DK, HG_DV), 0.5),
        "c": n(ks[3], (DEC_BATCH, D), 1.0),
        "c_ctx": n(ks[4], (D,), 1.0),
        "w_in": n(ks[5], (DEPTH, D, N_IN), D ** -0.5),
        "w_out": n(ks[6], (DEPTH, D, D), D ** -0.5),
        "ada_w": n(ks[7], (DEPTH, D, 6 * D), 0.5 * D ** -0.5),
        "ada_b": n(ks[8], (DEPTH, 6 * D), 0.02),
        "norm1_w": 1.0 + n(ks[9], (DEPTH, D), 0.05),
        "norm2_w": 1.0 + n(ks[10], (DEPTH, D), 0.05),
        "hg_lb_fwd": n(ks[11], (DEPTH, HG_WIDTH), 1.0),
        "hg_lb_bwd": n(ks[12], (DEPTH, HG_WIDTH), 1.0),
        "hg_norm_w": 1.0 + n(ks[13], (DEPTH, HG_WIDTH), 0.05),
        "hy_conv_w": n(ks[14], (DEPTH, CONV_W, 3 * HY_WIDTH), 0.6),
        "hy_w1": n(ks[15], (DEPTH, HY_EMB, HY_ORDER), HY_EMB ** -0.5),
        "hy_b1": n(ks[16], (DEPTH, HY_ORDER), 0.1),
        "hy_freq1": 1.0 + n(ks[17], (DEPTH, HY_ORDER), 0.05),
        "hy_w2": n(ks[18], (DEPTH, HY_ORDER, HY_ORDER), HY_ORDER ** -0.5),
        "hy_b2": n(ks[19], (DEPTH, HY_ORDER), 0.1),
        "hy_freq2": 1.0 + n(ks[20], (DEPTH, HY_ORDER), 0.05),
        "hy_w3": n(ks[21], (DEPTH, HY_ORDER, 2 * HY_WIDTH), HY_ORDER ** -0.5),
        "hy_d": n(ks[22], (DEPTH, HY_WIDTH), 1.0),
        "hy_norm_w": 1.0 + n(ks[23], (DEPTH, HY_WIDTH), 0.05),
        "ffn_w_up": n(ks[24], (DEPTH, D, 2 * D_FF), D ** -0.5),
        "ffn_conv_w": n(ks[25], (DEPTH, CONV_W, 2 * D_FF), 0.6),
        "ffn_w_down": n(ks[26], (DEPTH, D_FF, D), D_FF ** -0.5),
        "final_norm_w": 1.0 + n(ks[27], (D,), 0.05),
    }


def reference(x_prompt, x_sample, state_hgrn, c, c_ctx, w_in, w_out, ada_w, ada_b,
              norm1_w, norm2_w, hg_lb_fwd, hg_lb_bwd, hg_norm_w, hy_conv_w,
              hy_w1, hy_b1, hy_freq1, hy_w2, hy_b2, hy_freq2, hy_w3, hy_d, hy_norm_w,
              ffn_w_up, ffn_conv_w, ffn_w_down, final_norm_w):
    p = dict(w_in=w_in, w_out=w_out, ada_w=ada_w, ada_b=ada_b, norm1_w=norm1_w,
             norm2_w=norm2_w, hg_lb_fwd=hg_lb_fwd, hg_lb_bwd=hg_lb_bwd, hg_norm_w=hg_norm_w,
             hy_conv_w=hy_conv_w, hy_w1=hy_w1, hy_b1=hy_b1, hy_freq1=hy_freq1, hy_w2=hy_w2,
             hy_b2=hy_b2, hy_freq2=hy_freq2, hy_w3=hy_w3, hy_d=hy_d, hy_norm_w=hy_norm_w,
             ffn_w_up=ffn_w_up, ffn_conv_w=ffn_conv_w, ffn_w_down=ffn_w_down)

    bp = x_prompt.shape[0]
    zero_state = jnp.zeros((bp, 2, HG_HEADS, HG_DK, HG_DV), F32)
    cvec_ctx = jnp.broadcast_to(c_ctx[None, :], (bp, c_ctx.shape[0])).astype(x_prompt.dtype)
    xp = x_prompt
    ctx_states = []
    for l in range(DEPTH):
        xp, st = _layer(xp, cvec_ctx, 1, zero_state, l, p)
        ctx_states.append(st)
    y_prompt = _rmsnorm(xp, final_norm_w)
    new_state_hgrn = jnp.stack(ctx_states, axis=1)

    rows = x_sample.shape[1] // GRID_W
    xs = x_sample
    for l in range(DEPTH):
        xs, _ = _layer(xs, c.astype(x_sample.dtype), rows, state_hgrn[:, l], l, p)
    y_sample = _rmsnorm(xs, final_norm_w)
    return (y_prompt, y_sample, new_state_hgrn)
```

```python
import functools
import math

import numpy as np
import jax
import jax.numpy as jnp
from jax import lax
from jax.experimental import pallas as pl
from jax.experimental.pallas import tpu as pltpu

F32 = jnp.float32
BF16 = jnp.bfloat16

HG_DK = 128
CHUNK = 64
GRID_W = 64
HY_BANDS = 16
HY_TARGET = 1e-2
HY_FAST = 0.3
HY_SLOW = 1.5
EPS = 1e-6

LANES = 128
VMEM_LIMIT = 56 * 1024 * 1024

N_LEVELS = int(math.log2(CHUNK))
N_WBLK = N_LEVELS + 2
MOD_ROWS = 16


def _bdot(a, b):
    return jnp.dot(a.astype(BF16), b.astype(BF16), preferred_element_type=F32)


def _bdot_nt(a, b):
    return lax.dot_general(a.astype(BF16), b.astype(BF16), (((1,), (1,)), ((), ())),
                           preferred_element_type=F32)


def _bdot_tn(a, b):
    return lax.dot_general(a.astype(BF16), b.astype(BF16), (((0,), (0,)), ((), ())),
                           preferred_element_type=F32)


def _hdot(a, b):
    return jnp.dot(a, b, preferred_element_type=F32, precision=lax.Precision.HIGHEST)


def _silu(x):
    return x * jax.nn.sigmoid(x)


def _rms(x):
    return x * lax.rsqrt(jnp.mean(x * x, axis=-1, keepdims=True) + EPS)


def _const_spec(shape, index_map):
    return pl.BlockSpec(shape, index_map, pipeline_mode=pl.Buffered(1))


def _dwconv3(p, w, row_w):
    n = p.shape[0]
    r = lax.broadcasted_iota(jnp.int32, (n, 1), 0) & (row_w - 1)
    prev = jnp.where(r == 0, 0.0, pltpu.roll(p, 1, 0))
    nxt = jnp.where(r == row_w - 1, 0.0, pltpu.roll(p, n - 1, 0))
    return w[0:1] * prev + w[1:2] * p + w[2:3] * nxt


def _mod_kernel(cv_ref, w_ref, b_ref, o_ref):
    o_ref[0] = _hdot(_silu(cv_ref[...]), w_ref[0]) + b_ref[0]


def _modulation(cv, ada_w, ada_b):
    depth, d, n = ada_w.shape
    tn = 1536
    return pl.pallas_call(
        _mod_kernel,
        grid=(depth, n // tn),
        in_specs=[
            pl.BlockSpec((MOD_ROWS, d), lambda l, j: (0, 0)),
            pl.BlockSpec((1, d, tn), lambda l, j: (l, 0, j)),
            pl.BlockSpec((1, 1, tn), lambda l, j: (l, 0, j)),
        ],
        out_specs=pl.BlockSpec((1, MOD_ROWS, tn), lambda l, j: (l, 0, j)),
        out_shape=jax.ShapeDtypeStruct((depth, MOD_ROWS, n), F32),
        compiler_params=pltpu.CompilerParams(vmem_limit_bytes=VMEM_LIMIT),
        name="modulation",
    )(cv, ada_w, ada_b.reshape(depth, 1, n))


def _dft_tables(L):
    f = np.arange(L, dtype=np.int64)[:, None]
    t = np.arange(L, dtype=np.int64)[None, :]
    ang = 2.0 * np.pi * (((2 * f + 1) * t) % (4 * L)).astype(np.float64) / (4 * L)
    return np.cos(ang), np.sin(ang)


def _filter_feats(L):
    def feats(pos):
        t = pos / (L - 1)
        bands = np.linspace(1e-4, HY_BANDS - 1, HY_BANDS)[None, :]
        ang = 2.0 * np.pi * pos[:, None] * bands / L
        out = np.zeros((pos.shape[0], LANES), np.float64)
        out[:, 0] = t
        out[:, 1:1 + HY_BANDS] = np.cos(ang)
        out[:, 1 + HY_BANDS:1 + 2 * HY_BANDS] = np.sin(ang)
        return out
    pos = np.arange(L, dtype=np.float64)
    return np.concatenate([feats(pos), feats(L - pos)], axis=0).astype(np.float32)


def _filter_taps_kernel(L, hw, ft_ref, w1_ref, b1_ref, f1_ref, w2_ref, b2_ref, f2_ref, w3_ref,
                        dl_ref, kk_ref):
    ft = ft_ref[...]
    h = jnp.sin(f1_ref[0] * (_hdot(ft, w1_ref[0]) + b1_ref[0]))
    h = jnp.sin(f2_ref[0] * (_hdot(h, w2_ref[0]) + b2_ref[0]))
    h = _hdot(h, w3_ref[0])
    dl = dl_ref[...]
    t = ft[:, 0:1]
    k1 = h[:L, :hw] * jnp.exp(-t[:L] * dl)
    row = lax.broadcasted_iota(jnp.int32, (L, 1), 0)
    k2 = jnp.where(row == 0, 0.0, -(h[L:, hw:] * jnp.exp(-t[L:] * dl)))
    norm = jnp.sum(jnp.abs(k1), axis=0, keepdims=True) + jnp.sum(jnp.abs(k2), axis=0, keepdims=True) + EPS
    kk_ref[0, :, :hw] = k1 / norm
    kk_ref[0, :, hw:] = k2 / norm


def _filter_dft_kernel(hw, kk_ref, cm_ref, sm_ref, kr_ref, ki_ref):
    kk = kk_ref[0]
    gc = _hdot(cm_ref[...], kk)
    gs = _hdot(sm_ref[...], kk)
    tf = gc.shape[0]
    row = lax.broadcasted_iota(jnp.int32, (tf, 1), 0)
    alt = jnp.where((row & 1) == 0, 1.0, -1.0)
    kr_ref[0] = gc[:, :hw] - alt * gs[:, hw:]
    ki_ref[0] = -(gs[:, :hw] + alt * gc[:, hw:])


def _filters(L, p):
    depth, _, hw2 = p["hy_w3"].shape
    hw = hw2 // 2
    cm, sm = _dft_tables(L)
    ft = jnp.asarray(_filter_feats(L))
    min_decay = math.log(HY_TARGET) / HY_SLOW
    max_decay = math.log(HY_TARGET) / HY_FAST
    deltas = jnp.asarray(np.abs(np.linspace(min_decay, max_decay, hw)).astype(np.float32)[None, :])

    def pad_to(a, rows, cols):
        return jnp.pad(a, ((0, 0), (0, rows - a.shape[1]), (0, cols - a.shape[2])))

    w1 = pad_to(p["hy_w1"], LANES, LANES)
    w2 = pad_to(p["hy_w2"], LANES, LANES)
    w3 = pad_to(p["hy_w3"], LANES, hw2)
    vec = lambda a: pad_to(a[:, None, :], 1, LANES)
    lay = lambda shape: pl.BlockSpec((1,) + shape, lambda l: (l, 0, 0))
    kk = pl.pallas_call(
        functools.partial(_filter_taps_kernel, L, hw),
        grid=(depth,),
        in_specs=[
            _const_spec((2 * L, LANES), lambda l: (0, 0)),
            lay((LANES, LANES)), lay((1, LANES)), lay((1, LANES)),
            lay((LANES, LANES)), lay((1, LANES)), lay((1, LANES)),
            lay((LANES, hw2)),
            _const_spec((1, hw), lambda l: (0, 0)),
        ],
        out_specs=lay((L, hw2)),
        out_shape=jax.ShapeDtypeStruct((depth, L, hw2), F32),
        compiler_params=pltpu.CompilerParams(vmem_limit_bytes=VMEM_LIMIT),
        name=f"hyena_filter_taps_{L}",
    )(ft, w1, vec(p["hy_b1"]), vec(p["hy_freq1"]), w2, vec(p["hy_b2"]), vec(p["hy_freq2"]), w3, deltas)
    tf = 256
    frq = pl.BlockSpec((tf, L), lambda l, j: (j, 0))
    out = pl.BlockSpec((1, tf, hw), lambda l, j: (l, j, 0))
    return pl.pallas_call(
        functools.partial(_filter_dft_kernel, hw),
        grid=(depth, L // tf),
        in_specs=[pl.BlockSpec((1, L, hw2), lambda l, j: (l, 0, 0)), frq, frq],
        out_specs=[out, out],
        out_shape=[jax.ShapeDtypeStruct((depth, L, hw), F32)] * 2,
        compiler_params=pltpu.CompilerParams(vmem_limit_bytes=VMEM_LIMIT),
        name=f"hyena_filter_dft_{L}",
    )(kk, jnp.asarray(cm, F32), jnp.asarray(sm, F32))


def _lower_bound(lb_ref, l):
    prm = lb_ref[...]
    e = jnp.exp(prm - jnp.max(prm, axis=0, keepdims=True))
    s = e / jnp.sum(e, axis=0, keepdims=True)
    acc = jnp.zeros_like(s[0:1])
    for i in range(1, l + 1):
        acc = acc + s[i:i + 1]
    return acc


def _log_forget(x, lb):
    ls = jnp.minimum(x, 0.0) - jnp.log1p(jnp.exp(-jnp.abs(x)))
    a = jnp.log(lb)
    b = jnp.log1p(-lb) + ls
    return jnp.maximum(a, b) + jnp.log1p(jnp.exp(-jnp.abs(a - b)))


def _pre_kernel(l, row_w, hg, hy, x_ref, mod_ref, n1_ref, win_ref, lbf_ref, lbb_ref, cw_ref,
                q_ref, kf_ref, kb_ref, lf_ref, lb_ref, v_ref, sg_ref, u_ref, x2_ref):
    d = x_ref.shape[1]
    x = x_ref[...]
    m = mod_ref[0]
    sh1, sc1 = m[:, 0:d], m[:, d:2 * d]
    h = (_rms(x) * n1_ref[0] * (1.0 + sc1) + sh1).astype(BF16)

    def proj(i, width):
        return jnp.dot(h, win_ref[0, :, i:i + width], preferred_element_type=F32)

    q_ref[...] = _silu(proj(0, hg)) * (HG_DK ** -0.5)
    for i, (lbp_ref, k_ref, g_ref) in enumerate(((lbf_ref, kf_ref, lf_ref), (lbb_ref, kb_ref, lb_ref))):
        lb = _lower_bound(lbp_ref, l)
        z = proj((1 + i) * hg, hg)
        g_ref[...] = _log_forget(z, lb)
        k_ref[...] = (1.0 - lb) * jax.nn.sigmoid(-z)
    v_ref[...] = proj(3 * hg, hg)
    sg_ref[...] = _silu(proj(4 * hg, hg))
    cw = cw_ref[0]
    conv = lambda j: _dwconv3(proj(5 * hg + j * hy, hy), cw[:, j * hy:(j + 1) * hy], row_w)
    u_ref[...] = conv(1) * conv(0)
    x2_ref[...] = conv(2)


def _pre(l, x, mod, mod_row, row_w, seq, tm, p):
    t, d = x.shape
    hg = p["hg_lb_fwd"].shape[1]
    hy = p["hy_d"].shape[1]
    n_in = p["w_in"].shape[2]
    depth = p["w_in"].shape[0]
    tok = lambda w: pl.BlockSpec((tm, w), lambda i: (i, 0))
    out = jax.ShapeDtypeStruct((t, hg), F32)
    return pl.pallas_call(
        functools.partial(_pre_kernel, l, row_w, hg, hy),
        grid=(t // tm,),
        in_specs=[
            tok(d),
            pl.BlockSpec((1, 1, mod.shape[2]), lambda i: (mod_row(l, i * tm // seq), 0, 0)),
            _const_spec((1, 1, d), lambda i: (l, 0, 0)),
            _const_spec((1, d, n_in), lambda i: (l, 0, 0)),
            _const_spec((depth, hg), lambda i: (0, 0)),
            _const_spec((depth, hg), lambda i: (0, 0)),
            _const_spec((1, 3, 3 * hy), lambda i: (l, 0, 0)),
        ],
        out_specs=[tok(hg)] * 9,
        out_shape=[out] * 9,
        compiler_params=pltpu.CompilerParams(vmem_limit_bytes=VMEM_LIMIT),
        name="pre_mixer",
    )(x, mod, p["norm1_w3"], p["w_in_bf"], p["hg_lb_fwd"], p["hg_lb_bwd"], p["hy_conv_w"])


def _scan_tables():
    c = CHUNK
    blocks, masks = [], []
    j = np.arange(c)[:, None]
    i = np.arange(c)[None, :]
    s = c // 2
    while s >= 1:
        w = np.zeros((c, c), np.float32)
        for t in range(c):
            ref = (t // (2 * s)) * 2 * s + s - 1
            if t % (2 * s) >= s:
                w[t, ref + 1:t + 1] = 1.0
            else:
                w[t, t + 1:ref + 1] = 1.0
        blocks.append(w)
        masks.append(((j // (2 * s) == i // (2 * s)) & (j % (2 * s) >= s) & (i % (2 * s) < s)).astype(np.float32))
        s //= 2
    blocks.append(np.tril(np.ones((c, c), np.float32)))
    blocks.append(np.triu(np.ones((c, c), np.float32), 1))
    masks.append(np.eye(c, dtype=np.float32))
    wf = np.concatenate(blocks, axis=0)
    mf = np.stack(masks)
    flip = lambda a: a[..., ::-1, ::-1]
    wb = np.concatenate([flip(b) for b in blocks], axis=0)
    mb = np.stack([flip(mk) for mk in masks])
    return (np.concatenate([wf] * 3, axis=1), np.concatenate([wb] * 3, axis=1), mf, mb)


def _scan_kernel(nc, heads, has_s0, want_state, *refs):
    q_ref, kf_ref, kb_ref, lf_ref, lb_ref, v_ref = refs[:6]
    refs = refs[6:]
    if has_s0:
        s0_ref, refs = refs[0], refs[1:]
    wf_ref, wb_ref, mf_ref, mb_ref, o_ref = refs[:5]
    refs = refs[5:]
    if want_state:
        st_ref, refs = refs[0], refs[1:]
    s_scr = refs[0]
    c = CHUNK

    for d, (k_ref, g_ref, w_ref, m_ref) in enumerate(((kf_ref, lf_ref, wf_ref, mf_ref),
                                                      (kb_ref, lb_ref, wb_ref, mb_ref))):
        for h in range(heads):
            s_scr[h] = s0_ref[0, 0, d, h].T if has_s0 else jnp.zeros(s_scr.shape[1:], F32)

        def body(ci, carry, d=d, k_ref=k_ref, g_ref=g_ref, w_ref=w_ref, m_ref=m_ref):
            r0 = pl.multiple_of((ci if d == 0 else nc - 1 - ci) * c, c)
            rows = pl.ds(r0, c)
            g = g_ref[rows, :]
            g1 = g.astype(BF16)
            r1 = g - g1.astype(F32)
            g2 = r1.astype(BF16)
            g3 = (r1 - g2.astype(F32)).astype(BF16)
            e = jnp.exp(jnp.dot(w_ref[...], jnp.concatenate([g1, g2, g3], axis=0),
                                preferred_element_type=F32))
            q, k, v = q_ref[rows, :], k_ref[rows, :], v_ref[rows, :]
            for h in range(heads):
                cols = slice(h * HG_DK, (h + 1) * HG_DK)
                qh, kh, vh = q[:, cols], k[:, cols], v[:, cols]
                a = m_ref[N_LEVELS] * _bdot_nt(qh, kh)
                for s in range(N_LEVELS):
                    es = e[s * c:(s + 1) * c, cols]
                    a = a + m_ref[s] * _bdot_nt(qh * es, kh * es)
                ef = e[N_LEVELS * c:(N_LEVELS + 1) * c, cols]
                el = e[(N_LEVELS + 1) * c:(N_LEVELS + 2) * c, cols]
                st = s_scr[h]
                o = _bdot(a, vh) + _bdot_nt(qh * ef, st)
                e_all = ef[c - 1:c] if d == 0 else ef[0:1]
                s_scr[h] = e_all * st + _bdot_tn(vh, kh * el)
                if d == 0:
                    o_ref[rows, cols] = o
                else:
                    o_ref[rows, cols] += o
            return carry

        lax.fori_loop(0, nc, body, 0)
        if want_state:
            for h in range(heads):
                st_ref[0, d, h] = s_scr[h].T


def _scan(l, seq, nseq, arrs, s0, want_state):
    t, hg = arrs[0].shape
    heads = hg // HG_DK
    nc = seq // CHUNK
    wf, wb, mf, mb = _scan_tables()
    wf, wb = jnp.asarray(wf, F32).astype(BF16), jnp.asarray(wb, F32).astype(BF16)
    mf, mb = jnp.asarray(mf), jnp.asarray(mb)
    seq_spec = pl.BlockSpec((seq, hg), lambda b: (b, 0))
    in_specs = [seq_spec] * 6
    args = list(arrs)
    if s0 is not None:
        in_specs.append(pl.BlockSpec((1, 1) + s0.shape[2:], lambda b: (b, l, 0, 0, 0, 0)))
        args.append(s0)
    in_specs += [_const_spec(wf.shape, lambda b: (0, 0)), _const_spec(wb.shape, lambda b: (0, 0)),
                 _const_spec(mf.shape, lambda b: (0, 0, 0)), _const_spec(mb.shape, lambda b: (0, 0, 0))]
    args += [wf, wb, mf, mb]
    out_specs = [seq_spec]
    out_shape = [jax.ShapeDtypeStruct((t, hg), F32)]
    if want_state:
        out_specs.append(pl.BlockSpec((1, 2, heads, HG_DK, HG_DK), lambda b: (b, 0, 0, 0, 0)))
        out_shape.append(jax.ShapeDtypeStruct((nseq, 2, heads, HG_DK, HG_DK), F32))
    res = pl.pallas_call(
        functools.partial(_scan_kernel, nc, heads, s0 is not None, want_state),
        grid=(nseq,),
        in_specs=in_specs,
        out_specs=out_specs,
        out_shape=out_shape,
        scratch_shapes=[pltpu.VMEM((heads, HG_DK, HG_DK), F32)],
        compiler_params=pltpu.CompilerParams(vmem_limit_bytes=VMEM_LIMIT),
        name="hgrn2_scan",
    )(*args)
    return res if want_state else (res[0], None)


def _hyena_kernel(L, u_ref, x2_ref, kr_ref, ki_ref, d_ref, nw_ref, cs_ref, ics_ref, y_ref):
    u = u_ref[...]
    uu = jnp.dot(cs_ref[...], u.astype(BF16), preferred_element_type=F32)
    ur, us = uu[:L], uu[L:]
    kr, ki = kr_ref[0], ki_ref[0]
    zr = ur * kr + us * ki
    zi = ur * ki - us * kr
    zz = jnp.concatenate([zr.astype(BF16), zi.astype(BF16)], axis=0)
    z = jnp.dot(ics_ref[...], zz, preferred_element_type=F32)
    y = x2_ref[...] * (z + d_ref[0] * u)
    y_ref[...] = _rms(y) * nw_ref[0]


def _hyena(l, seq, nseq, u, x2, kr, ki, p):
    t, hw = u.shape
    cm, sm = _dft_tables(seq)
    cs = jnp.asarray(np.concatenate([cm, sm], axis=0), F32).astype(BF16)
    ics = jnp.asarray(np.concatenate([cm.T, -sm.T], axis=1) / seq, F32).astype(BF16)
    seq_spec = pl.BlockSpec((seq, hw), lambda b: (b, 0))
    lay = lambda shape: _const_spec((1,) + shape, lambda b: (l, 0, 0))
    return pl.pallas_call(
        functools.partial(_hyena_kernel, seq),
        grid=(nseq,),
        in_specs=[seq_spec, seq_spec, lay((seq, hw)), lay((seq, hw)), lay((1, hw)), lay((1, hw)),
                  _const_spec(cs.shape, lambda b: (0, 0)), _const_spec(ics.shape, lambda b: (0, 0))],
        out_specs=seq_spec,
        out_shape=jax.ShapeDtypeStruct((t, hw), F32),
        compiler_params=pltpu.CompilerParams(vmem_limit_bytes=VMEM_LIMIT),
        name="hyena_conv",
    )(u, x2, kr, ki, p["hy_d3"], p["hy_norm_w3"], cs, ics)


def _post_kernel(row_w, heads, ff_chunk, last, x_ref, o_ref, sg_ref, y_ref, mod_ref, hn_ref, wout_ref,
                 n2_ref, wup_ref, cw_ref, wdn_ref, fn_ref, out_ref):
    d = x_ref.shape[1]
    hg = o_ref.shape[1]
    dff = wdn_ref.shape[1]
    m = mod_ref[0]
    g1, sh2, sc2, g2 = (m[:, i * d:(i + 1) * d] for i in range(2, 6))
    o = o_ref[...]
    hn = hn_ref[0]
    sg = sg_ref[...]
    mix = _bdot(y_ref[...], wout_ref[0, hg:, :])
    for h in range(heads):
        cols = slice(h * HG_DK, (h + 1) * HG_DK)
        mix = mix + _bdot(_rms(o[:, cols]) * hn[:, cols] * sg[:, cols], wout_ref[0, cols, :])
    x = x_ref[...] + g1 * mix
    h2 = (_rms(x) * n2_ref[0] * (1.0 + sc2) + sh2).astype(BF16)
    cw = cw_ref[0]
    ffn = jnp.zeros_like(x)
    for j in range(dff // ff_chunk):
        a, b = j * ff_chunk, (j + 1) * ff_chunk
        gate = _dwconv3(jnp.dot(h2, wup_ref[0, :, a:b], preferred_element_type=F32), cw[:, a:b], row_w)
        val = _dwconv3(jnp.dot(h2, wup_ref[0, :, dff + a:dff + b], preferred_element_type=F32),
                       cw[:, dff + a:dff + b], row_w)
        ffn = ffn + _bdot(_silu(gate) * val, wdn_ref[0, a:b, :])
    x = x + g2 * ffn
    out_ref[...] = _rms(x) * fn_ref[...] if last else x


def _post(l, x, o, sg, y, mod, mod_row, row_w, seq, tm, last, p):
    t, d = x.shape
    hg = o.shape[1]
    dff = p["ffn_w_down"].shape[1]
    tok = lambda w: pl.BlockSpec((tm, w), lambda i: (i, 0))
    lay = lambda shape: _const_spec((1,) + shape, lambda i: (l, 0, 0))
    return pl.pallas_call(
        functools.partial(_post_kernel, row_w, hg // HG_DK, 256, last),
        grid=(t // tm,),
        in_specs=[
            tok(d), tok(hg), tok(hg), tok(y.shape[1]),
            pl.BlockSpec((1, 1, mod.shape[2]), lambda i: (mod_row(l, i * tm // seq), 0, 0)),
            lay((1, hg)), lay((d, d)), lay((1, d)), lay((d, 2 * dff)), lay((3, 2 * dff)), lay((dff, d)),
            _const_spec((1, d), lambda i: (0, 0)),
        ],
        out_specs=tok(d),
        out_shape=jax.ShapeDtypeStruct((t, d), F32),
        compiler_params=pltpu.CompilerParams(vmem_limit_bytes=VMEM_LIMIT),
        name="post_mixer_ffn",
    )(x, o, sg, y, mod, p["hg_norm_w3"], p["w_out_bf"], p["norm2_w3"], p["ffn_w_up_bf"],
      p["ffn_conv_w"], p["ffn_w_down_bf"], p["final_norm_w2"])


def _group(x3, mod, mod_row, row_w, s0, want_state, filt, tm, p):
    nseq, seq, d = x3.shape
    depth = p["w_in"].shape[0]
    x = x3.reshape(nseq * seq, d)
    kr, ki = filt
    states = []
    for l in range(depth):
        q, kf, kb, lf, lb, v, sg, u, x2 = _pre(l, x, mod, mod_row, row_w, seq, tm, p)
        o, st = _scan(l, seq, nseq, (q, kf, kb, lf, lb, v), s0, want_state)
        y = _hyena(l, seq, nseq, u, x2, kr, ki, p)
        x = _post(l, x, o, sg, y, mod, mod_row, row_w, seq, tm, l == depth - 1, p)
        states.append(st)
    return x.reshape(nseq, seq, d), states


def kernel(x_prompt, x_sample, state_hgrn, c, c_ctx, w_in, w_out, ada_w, ada_b, norm1_w, norm2_w,
           hg_lb_fwd, hg_lb_bwd, hg_norm_w, hy_conv_w, hy_w1, hy_b1, hy_freq1, hy_w2, hy_b2, hy_freq2,
           hy_w3, hy_d, hy_norm_w, ffn_w_up, ffn_conv_w, ffn_w_down, final_norm_w):
    depth, d, _ = w_in.shape
    vec3 = lambda a: a.reshape(a.shape[0], 1, a.shape[1])
    p = dict(w_in=w_in, w_in_bf=w_in.astype(BF16), w_out_bf=w_out.astype(BF16),
             ffn_w_up_bf=ffn_w_up.astype(BF16), ffn_w_down=ffn_w_down, ffn_w_down_bf=ffn_w_down.astype(BF16),
             norm1_w3=vec3(norm1_w), norm2_w3=vec3(norm2_w), hg_norm_w3=vec3(hg_norm_w),
             hy_d=hy_d, hy_d3=vec3(hy_d), hy_norm_w3=vec3(hy_norm_w), final_norm_w2=final_norm_w[None, :],
             hg_lb_fwd=hg_lb_fwd, hg_lb_bwd=hg_lb_bwd, hy_conv_w=hy_conv_w, ffn_conv_w=ffn_conv_w,
             hy_w1=hy_w1, hy_b1=hy_b1, hy_freq1=hy_freq1, hy_w2=hy_w2, hy_b2=hy_b2, hy_freq2=hy_freq2,
             hy_w3=hy_w3)

    n_dec = c.shape[0]
    cv = jnp.concatenate([c, c_ctx[None, :], jnp.zeros((MOD_ROWS - n_dec - 1, d), F32)], axis=0)
    mod = _modulation(cv, ada_w, ada_b).reshape(depth * MOD_ROWS, 1, 6 * d)

    seq_p, seq_s = x_prompt.shape[1], x_sample.shape[1]
    y_p, states = _group(x_prompt, mod, lambda l, b: l * MOD_ROWS + n_dec, seq_p, None, True,
                         _filters(seq_p, p), 256, p)
    y_s, _ = _group(x_sample, mod, lambda l, b: l * MOD_ROWS + b, GRID_W, state_hgrn, False,
                    _filters(seq_s, p), 256, p)
    return (y_p, y_s, jnp.stack(states, axis=1))
```

```python
import functools
import math

import numpy as np
import jax
import jax.numpy as jnp
from jax import lax
from jax.experimental import pallas as pl
from jax.experimental.pallas import tpu as pltpu

F32 = jnp.float32
BF16 = jnp.bfloat16

HG_DK = 128
CHUNK = 64
GRID_W = 64
HY_BANDS = 16
HY_TARGET = 1e-2
HY_FAST = 0.3
HY_SLOW = 1.5
EPS = 1e-6

LANES = 128
VMEM_LIMIT = 56 * 1024 * 1024

N_LEVELS = int(math.log2(CHUNK))
MOD_ROWS = 16
TOKEN_TILE = 512


def _bdot(a, b):
    return jnp.dot(a.astype(BF16), b.astype(BF16), preferred_element_type=F32)


def _bdot_nt(a, b):
    return lax.dot_general(a.astype(BF16), b.astype(BF16), (((1,), (1,)), ((), ())),
                           preferred_element_type=F32)


def _bdot_tn(a, b):
    return lax.dot_general(a.astype(BF16), b.astype(BF16), (((0,), (0,)), ((), ())),
                           preferred_element_type=F32)


def _hdot(a, b):
    return jnp.dot(a, b, preferred_element_type=F32, precision=lax.Precision.HIGHEST)


def _silu(x):
    return x * jax.nn.sigmoid(x)


def _rms(x):
    return x * lax.rsqrt(jnp.mean(x * x, axis=-1, keepdims=True) + EPS)


def _const_spec(shape, index_map):
    return pl.BlockSpec(shape, index_map, pipeline_mode=pl.Buffered(1))


def _dwconv3(p, w, row_w):
    n = p.shape[0]
    r = lax.broadcasted_iota(jnp.int32, (n, 1), 0) & (row_w - 1)
    prev = jnp.where(r == 0, 0.0, pltpu.roll(p, 1, 0))
    nxt = jnp.where(r == row_w - 1, 0.0, pltpu.roll(p, n - 1, 0))
    return w[0:1] * prev + w[1:2] * p + w[2:3] * nxt


def _mod_kernel(cv_ref, w_ref, b_ref, o_ref):
    o_ref[0] = _hdot(_silu(cv_ref[...]), w_ref[0]) + b_ref[0]


def _modulation(cv, ada_w, ada_b):
    depth, d, n = ada_w.shape
    tn = 1536
    return pl.pallas_call(
        _mod_kernel,
        grid=(depth, n // tn),
        in_specs=[
            pl.BlockSpec((MOD_ROWS, d), lambda l, j: (0, 0)),
            pl.BlockSpec((1, d, tn), lambda l, j: (l, 0, j)),
            pl.BlockSpec((1, 1, tn), lambda l, j: (l, 0, j)),
        ],
        out_specs=pl.BlockSpec((1, MOD_ROWS, tn), lambda l, j: (l, 0, j)),
        out_shape=jax.ShapeDtypeStruct((depth, MOD_ROWS, n), F32),
        compiler_params=pltpu.CompilerParams(vmem_limit_bytes=VMEM_LIMIT),
        name="modulation",
    )(cv, ada_w, ada_b.reshape(depth, 1, n))


def _dft_tables(L):
    f = np.arange(L, dtype=np.int64)[:, None]
    t = np.arange(L, dtype=np.int64)[None, :]
    ang = 2.0 * np.pi * (((2 * f + 1) * t) % (4 * L)).astype(np.float64) / (4 * L)
    return np.cos(ang), np.sin(ang)


def _filter_feats(L):
    def feats(pos):
        t = pos / (L - 1)
        bands = np.linspace(1e-4, HY_BANDS - 1, HY_BANDS)[None, :]
        ang = 2.0 * np.pi * pos[:, None] * bands / L
        out = np.zeros((pos.shape[0], LANES), np.float64)
        out[:, 0] = t
        out[:, 1:1 + HY_BANDS] = np.cos(ang)
        out[:, 1 + HY_BANDS:1 + 2 * HY_BANDS] = np.sin(ang)
        return out
    pos = np.arange(L, dtype=np.float64)
    return np.concatenate([feats(pos), feats(L - pos)], axis=0).astype(np.float32)


def _filter_taps_kernel(L, hw, ft_ref, w1_ref, b1_ref, f1_ref, w2_ref, b2_ref, f2_ref, w3_ref,
                        dl_ref, kk_ref):
    ft = ft_ref[...]
    h = jnp.sin(f1_ref[0] * (_hdot(ft, w1_ref[0]) + b1_ref[0]))
    h = jnp.sin(f2_ref[0] * (_hdot(h, w2_ref[0]) + b2_ref[0]))
    h = _hdot(h, w3_ref[0])
    dl = dl_ref[...]
    t = ft[:, 0:1]
    k1 = h[:L, :hw] * jnp.exp(-t[:L] * dl)
    row = lax.broadcasted_iota(jnp.int32, (L, 1), 0)
    k2 = jnp.where(row == 0, 0.0, -(h[L:, hw:] * jnp.exp(-t[L:] * dl)))
    norm = jnp.sum(jnp.abs(k1), axis=0, keepdims=True) + jnp.sum(jnp.abs(k2), axis=0, keepdims=True) + EPS
    kk_ref[0, :, :hw] = k1 / norm
    kk_ref[0, :, hw:] = k2 / norm


def _filter_dft_kernel(hw, kk_ref, cm_ref, sm_ref, kr_ref, ki_ref):
    kk = kk_ref[0]
    gc = _hdot(cm_ref[...], kk)
    gs = _hdot(sm_ref[...], kk)
    tf = gc.shape[0]
    row = lax.broadcasted_iota(jnp.int32, (tf, 1), 0)
    alt = jnp.where((row & 1) == 0, 1.0, -1.0)
    kr_ref[0] = gc[:, :hw] - alt * gs[:, hw:]
    ki_ref[0] = -(gs[:, :hw] + alt * gc[:, hw:])


def _filters(L, p):
    depth, _, hw2 = p["hy_w3"].shape
    hw = hw2 // 2
    cm, sm = _dft_tables(L)
    ft = jnp.asarray(_filter_feats(L))
    min_decay = math.log(HY_TARGET) / HY_SLOW
    max_decay = math.log(HY_TARGET) / HY_FAST
    deltas = jnp.asarray(np.abs(np.linspace(min_decay, max_decay, hw)).astype(np.float32)[None, :])

    def pad_to(a, rows, cols):
        return jnp.pad(a, ((0, 0), (0, rows - a.shape[1]), (0, cols - a.shape[2])))

    w1 = pad_to(p["hy_w1"], LANES, LANES)
    w2 = pad_to(p["hy_w2"], LANES, LANES)
    w3 = pad_to(p["hy_w3"], LANES, hw2)
    vec = lambda a: pad_to(a[:, None, :], 1, LANES)
    lay = lambda shape: pl.BlockSpec((1,) + shape, lambda l: (l, 0, 0))
    kk = pl.pallas_call(
        functools.partial(_filter_taps_kernel, L, hw),
        grid=(depth,),
        in_specs=[
            _const_spec((2 * L, LANES), lambda l: (0, 0)),
            lay((LANES, LANES)), lay((1, LANES)), lay((1, LANES)),
            lay((LANES, LANES)), lay((1, LANES)), lay((1, LANES)),
            lay((LANES, hw2)),
            _const_spec((1, hw), lambda l: (0, 0)),
        ],
        out_specs=lay((L, hw2)),
        out_shape=jax.ShapeDtypeStruct((depth, L, hw2), F32),
        compiler_params=pltpu.CompilerParams(vmem_limit_bytes=VMEM_LIMIT),
        name=f"hyena_filter_taps_{L}",
    )(ft, w1, vec(p["hy_b1"]), vec(p["hy_freq1"]), w2, vec(p["hy_b2"]), vec(p["hy_freq2"]), w3, deltas)
    tf = 256
    frq = pl.BlockSpec((tf, L), lambda l, j: (j, 0))
    out = pl.BlockSpec((1, tf, hw), lambda l, j: (l, j, 0))
    return pl.pallas_call(
        functools.partial(_filter_dft_kernel, hw),
        grid=(depth, L // tf),
        in_specs=[pl.BlockSpec((1, L, hw2), lambda l, j: (l, 0, 0)), frq, frq],
        out_specs=[out, out],
        out_shape=[jax.ShapeDtypeStruct((depth, L, hw), F32)] * 2,
        compiler_params=pltpu.CompilerParams(vmem_limit_bytes=VMEM_LIMIT),
        name=f"hyena_filter_dft_{L}",
    )(kk, jnp.asarray(cm, F32), jnp.asarray(sm, F32))


def _lower_bound(lb_ref, l):
    prm = lb_ref[...]
    e = jnp.exp(prm - jnp.max(prm, axis=0, keepdims=True))
    s = e / jnp.sum(e, axis=0, keepdims=True)
    acc = jnp.zeros_like(s[0:1])
    for i in range(1, l + 1):
        acc = acc + s[i:i + 1]
    return acc


def _log_forget(x, lb):
    ls = jnp.minimum(x, 0.0) - jnp.log1p(jnp.exp(-jnp.abs(x)))
    a = jnp.log(lb)
    b = jnp.log1p(-lb) + ls
    return jnp.maximum(a, b) + jnp.log1p(jnp.exp(-jnp.abs(a - b)))


def _pre_kernel(l, row_w, hg, hy, x_ref, mod_ref, n1_ref, win_ref, lbf_ref, lbb_ref, cw_ref,
                q_ref, kf_ref, kb_ref, lf_ref, lb_ref, v_ref, sg_ref, u_ref, x2_ref):
    d = x_ref.shape[1]
    x = x_ref[...]
    m = mod_ref[0]
    sh1, sc1 = m[:, 0:d], m[:, d:2 * d]
    h = (_rms(x) * n1_ref[0] * (1.0 + sc1) + sh1).astype(BF16)

    def proj(i, width):
        return jnp.dot(h, win_ref[0, :, i:i + width], preferred_element_type=F32)

    q_ref[...] = _silu(proj(0, hg)) * (HG_DK ** -0.5)
    for i, (lbp_ref, k_ref, g_ref) in enumerate(((lbf_ref, kf_ref, lf_ref), (lbb_ref, kb_ref, lb_ref))):
        lb = _lower_bound(lbp_ref, l)
        z = proj((1 + i) * hg, hg)
        g_ref[...] = _log_forget(z, lb)
        k_ref[...] = (1.0 - lb) * jax.nn.sigmoid(-z)
    v_ref[...] = proj(3 * hg, hg)
    sg_ref[...] = _silu(proj(4 * hg, hg))
    cw = cw_ref[0]
    conv = lambda j: _dwconv3(proj(5 * hg + j * hy, hy), cw[:, j * hy:(j + 1) * hy], row_w)
    u_ref[...] = conv(1) * conv(0)
    x2_ref[...] = conv(2)


def _pre(l, x, mod, mod_row, row_w, seq, tm, p):
    t, d = x.shape
    hg = p["hg_lb_fwd"].shape[1]
    hy = p["hy_d"].shape[1]
    n_in = p["w_in"].shape[2]
    depth = p["w_in"].shape[0]
    tok = lambda w: pl.BlockSpec((tm, w), lambda i: (i, 0))
    out = jax.ShapeDtypeStruct((t, hg), F32)
    return pl.pallas_call(
        functools.partial(_pre_kernel, l, row_w, hg, hy),
        grid=(t // tm,),
        in_specs=[
            tok(d),
            pl.BlockSpec((1, 1, mod.shape[2]), lambda i: (mod_row(l, i * tm // seq), 0, 0)),
            _const_spec((1, 1, d), lambda i: (l, 0, 0)),
            _const_spec((1, d, n_in), lambda i: (l, 0, 0)),
            _const_spec((depth, hg), lambda i: (0, 0)),
            _const_spec((depth, hg), lambda i: (0, 0)),
            _const_spec((1, 3, 3 * hy), lambda i: (l, 0, 0)),
        ],
        out_specs=[tok(hg)] * 9,
        out_shape=[out] * 9,
        compiler_params=pltpu.CompilerParams(vmem_limit_bytes=VMEM_LIMIT),
        name="pre_mixer",
    )(x, mod, p["norm1_w3"], p["w_in_bf"], p["hg_lb_fwd"], p["hg_lb_bwd"], p["hy_conv_w"])


LEVELS = tuple(CHUNK >> (i + 1) for i in range(N_LEVELS))
SUBLANES = 8


def _scan_tables():
    c = CHUNK
    j = np.arange(c)[:, None]
    i = np.arange(c)[None, :]
    masks = [((j // (2 * s) == i // (2 * s)) & (j % (2 * s) >= s) & (i % (2 * s) < s)).astype(np.float32)
             for s in LEVELS]
    masks.append(np.eye(c, dtype=np.float32))
    mf = np.stack(masks)
    tri = np.tril(np.ones((c, c), np.float32))
    tri = np.stack([np.concatenate([tri] * 3, axis=1), np.concatenate([tri.T] * 3, axis=1)])
    return tri, np.stack([mf, mf[:, ::-1, ::-1]])


def _row_of_group(b, k):
    n = b.shape[1]
    return jnp.concatenate([jnp.broadcast_to(b[r + k:r + k + 1], (SUBLANES, n))
                            for r in range(0, b.shape[0], SUBLANES)], axis=0)


def _decay_logs(b, g, d):
    c = b.shape[0]
    fwd = d == 0
    out = []
    r = lax.broadcasted_iota(jnp.int32, (c, 1), 0) & (SUBLANES - 1)
    for s in LEVELS:
        if s >= SUBLANES:
            parts = []
            for b0 in range(0, c, 2 * s):
                ref = b0 + s - 1 if fwd else b0 + s
                br, lo, hi = b[ref:ref + 1], b[b0:b0 + s], b[b0 + s:b0 + 2 * s]
                parts += [br - lo, hi - br] if fwd else [lo - br, br - hi]
            out.append(jnp.concatenate(parts, axis=0))
        elif s > 1:
            low = (r & (2 * s - 1)) < s
            br = _row_of_group(b, s - 1 if fwd else s)
            for k in range(2 * s, SUBLANES, 2 * s):
                br = jnp.where(r < k, br, _row_of_group(b, k + (s - 1 if fwd else s)))
            diff = b - br
            out.append(jnp.where(low == fwd, -diff, diff))
        else:
            out.append(jnp.where((r & 1) == (1 if fwd else 0), g, 0.0))
    far = b[c - 1:c] if fwd else b[0:1]
    return out + [b, far - b]


def _scan_kernel(nc, heads, has_s0, want_state, *refs):
    q_ref, kf_ref, kb_ref, lf_ref, lb_ref, v_ref = refs[:6]
    refs = refs[6:]
    if has_s0:
        s0_ref, refs = refs[0], refs[1:]
    tri_ref, m_ref, o_ref = refs[:3]
    refs = refs[3:]
    if want_state:
        st_ref, refs = refs[0], refs[1:]
    s_scr, ob_scr = refs
    c = CHUNK
    dirs = ((kf_ref, lf_ref), (kb_ref, lb_ref))
    pairs = [(d, h) for d in range(2) for h in range(heads)]
    col = lambda x, h: x[:, h * HG_DK:(h + 1) * HG_DK]

    for d, h in pairs:
        s_scr[d, h] = s0_ref[0, 0, d, h].T if has_s0 else jnp.zeros(s_scr.shape[2:], F32)

    def body(ci, carry):
        rows_d = [pl.ds(pl.multiple_of((ci if d == 0 else nc - 1 - ci) * c, c), c) for d in range(2)]
        loaded = [(g_ref[rows_d[d], :], q_ref[rows_d[d], :], k_ref[rows_d[d], :], v_ref[rows_d[d], :])
                  for d, (k_ref, g_ref) in enumerate(dirs)]
        states = {(d, h): s_scr[d, h] for d, h in pairs}

        cums = []
        for d in range(2):
            g = loaded[d][0]
            g1 = g.astype(BF16)
            r1 = g - g1.astype(F32)
            g2 = r1.astype(BF16)
            g3 = (r1 - g2.astype(F32)).astype(BF16)
            cums.append(jnp.dot(tri_ref[d], jnp.concatenate([g1, g2, g3], axis=0),
                                preferred_element_type=F32))

        scores, inter, new_state = {}, {}, {}
        for d in range(2):
            g, q, k, v = loaded[d]
            e = [jnp.exp(x) for x in _decay_logs(cums[d], g, d)]
            for h in range(heads):
                qh, kh, vh = col(q, h), col(k, h), col(v, h)
                scores[d, h] = [_bdot_nt(qh * col(es, h), kh * col(es, h)) for es in e[:N_LEVELS]]
                scores[d, h].append(_bdot_nt(qh, kh))
                ef, el = col(e[N_LEVELS], h), col(e[N_LEVELS + 1], h)
                st = states[d, h]
                inter[d, h] = _bdot_nt(qh * ef, st)
                e_all = ef[c - 1:c] if d == 0 else ef[0:1]
                new_state[d, h] = e_all * st + _bdot_tn(vh, kh * el)

        outs = {}
        for d, h in pairs:
            a = m_ref[d, 0] * scores[d, h][0]
            for s in range(1, N_LEVELS + 1):
                a = a + m_ref[d, s] * scores[d, h][s]
            outs[d, h] = _bdot(a, col(loaded[d][3], h)) + inter[d, h]

        for d, h in pairs:
            (o_ref if d == 0 else ob_scr)[rows_d[d], h * HG_DK:(h + 1) * HG_DK] = outs[d, h]
            s_scr[d, h] = new_state[d, h]
        return carry

    lax.fori_loop(0, nc, body, 0)
    o_ref[...] += ob_scr[...]
    if want_state:
        for d in range(2):
            for h in range(heads):
                st_ref[0, d, h] = s_scr[d, h].T


def _scan(l, seq, nseq, arrs, s0, want_state):
    t, hg = arrs[0].shape
    heads = hg // HG_DK
    nc = seq // CHUNK
    tri, masks = _scan_tables()
    tri, masks = jnp.asarray(tri, F32).astype(BF16), jnp.asarray(masks)
    seq_spec = pl.BlockSpec((seq, hg), lambda b: (b, 0))
    in_specs = [seq_spec] * 6
    args = list(arrs)
    if s0 is not None:
        in_specs.append(pl.BlockSpec((1, 1) + s0.shape[2:], lambda b: (b, l, 0, 0, 0, 0)))
        args.append(s0)
    in_specs += [_const_spec(tri.shape, lambda b: (0, 0, 0)), _const_spec(masks.shape, lambda b: (0, 0, 0, 0))]
    args += [tri, masks]
    out_specs = [seq_spec]
    out_shape = [jax.ShapeDtypeStruct((t, hg), F32)]
    if want_state:
        out_specs.append(pl.BlockSpec((1, 2, heads, HG_DK, HG_DK), lambda b: (b, 0, 0, 0, 0)))
        out_shape.append(jax.ShapeDtypeStruct((nseq, 2, heads, HG_DK, HG_DK), F32))
    res = pl.pallas_call(
        functools.partial(_scan_kernel, nc, heads, s0 is not None, want_state),
        grid=(nseq,),
        in_specs=in_specs,
        out_specs=out_specs,
        out_shape=out_shape,
        scratch_shapes=[pltpu.VMEM((2, heads, HG_DK, HG_DK), F32), pltpu.VMEM((seq, hg), F32)],
        compiler_params=pltpu.CompilerParams(vmem_limit_bytes=VMEM_LIMIT),
        name="hgrn2_scan",
    )(*args)
    return res if want_state else (res[0], None)


def _hyena_kernel(L, u_ref, x2_ref, kr_ref, ki_ref, d_ref, nw_ref, cs_ref, ics_ref, y_ref):
    u = u_ref[...]
    uu = jnp.dot(cs_ref[...], u.astype(BF16), preferred_element_type=F32)
    ur, us = uu[:L], uu[L:]
    kr, ki = kr_ref[0], ki_ref[0]
    zr = ur * kr + us * ki
    zi = ur * ki - us * kr
    zz = jnp.concatenate([zr.astype(BF16), zi.astype(BF16)], axis=0)
    z = jnp.dot(ics_ref[...], zz, preferred_element_type=F32)
    y = x2_ref[...] * (z + d_ref[0] * u)
    y_ref[...] = _rms(y) * nw_ref[0]


def _hyena(l, seq, nseq, u, x2, kr, ki, p):
    t, hw = u.shape
    cm, sm = _dft_tables(seq)
    cs = jnp.asarray(np.concatenate([cm, sm], axis=0), F32).astype(BF16)
    ics = jnp.asarray(np.concatenate([cm.T, -sm.T], axis=1) / seq, F32).astype(BF16)
    seq_spec = pl.BlockSpec((seq, hw), lambda b: (b, 0))
    lay = lambda shape: _const_spec((1,) + shape, lambda b: (l, 0, 0))
    return pl.pallas_call(
        functools.partial(_hyena_kernel, seq),
        grid=(nseq,),
        in_specs=[seq_spec, seq_spec, lay((seq, hw)), lay((seq, hw)), lay((1, hw)), lay((1, hw)),
                  _const_spec(cs.shape, lambda b: (0, 0)), _const_spec(ics.shape, lambda b: (0, 0))],
        out_specs=seq_spec,
        out_shape=jax.ShapeDtypeStruct((t, hw), F32),
        compiler_params=pltpu.CompilerParams(vmem_limit_bytes=VMEM_LIMIT),
        name="hyena_conv",
    )(u, x2, kr, ki, p["hy_d3"], p["hy_norm_w3"], cs, ics)


def _post_kernel(row_w, heads, ff_chunk, last, x_ref, o_ref, sg_ref, y_ref, mod_ref, hn_ref, wout_ref,
                 n2_ref, wup_ref, cw_ref, wdn_ref, fn_ref, out_ref):
    d = x_ref.shape[1]
    hg = o_ref.shape[1]
    dff = wdn_ref.shape[1]
    m = mod_ref[0]
    g1, sh2, sc2, g2 = (m[:, i * d:(i + 1) * d] for i in range(2, 6))
    o = o_ref[...]
    hn = hn_ref[0]
    sg = sg_ref[...]
    mix = _bdot(y_ref[...], wout_ref[0, hg:, :])
    for h in range(heads):
        cols = slice(h * HG_DK, (h + 1) * HG_DK)
        mix = mix + _bdot(_rms(o[:, cols]) * hn[:, cols] * sg[:, cols], wout_ref[0, cols, :])
    x = x_ref[...] + g1 * mix
    h2 = (_rms(x) * n2_ref[0] * (1.0 + sc2) + sh2).astype(BF16)
    cw = cw_ref[0]
    ffn = jnp.zeros_like(x)
    for j in range(dff // ff_chunk):
        a, b = j * ff_chunk, (j + 1) * ff_chunk
        gate = _dwconv3(jnp.dot(h2, wup_ref[0, :, a:b], preferred_element_type=F32), cw[:, a:b], row_w)
        val = _dwconv3(jnp.dot(h2, wup_ref[0, :, dff + a:dff + b], preferred_element_type=F32),
                       cw[:, dff + a:dff + b], row_w)
        ffn = ffn + _bdot(_silu(gate) * val, wdn_ref[0, a:b, :])
    x = x + g2 * ffn
    out_ref[...] = _rms(x) * fn_ref[...] if last else x


def _post(l, x, o, sg, y, mod, mod_row, row_w, seq, tm, last, p):
    t, d = x.shape
    hg = o.shape[1]
    dff = p["ffn_w_down"].shape[1]
    tok = lambda w: pl.BlockSpec((tm, w), lambda i: (i, 0))
    lay = lambda shape: _const_spec((1,) + shape, lambda i: (l, 0, 0))
    return pl.pallas_call(
        functools.partial(_post_kernel, row_w, hg // HG_DK, 256, last),
        grid=(t // tm,),
        in_specs=[
            tok(d), tok(hg), tok(hg), tok(y.shape[1]),
            pl.BlockSpec((1, 1, mod.shape[2]), lambda i: (mod_row(l, i * tm // seq), 0, 0)),
            lay((1, hg)), lay((d, d)), lay((1, d)), lay((d, 2 * dff)), lay((3, 2 * dff)), lay((dff, d)),
            _const_spec((1, d), lambda i: (0, 0)),
        ],
        out_specs=tok(d),
        out_shape=jax.ShapeDtypeStruct((t, d), F32),
        compiler_params=pltpu.CompilerParams(vmem_limit_bytes=VMEM_LIMIT),
        name="post_mixer_ffn",
    )(x, o, sg, y, mod, p["hg_norm_w3"], p["w_out_bf"], p["norm2_w3"], p["ffn_w_up_bf"],
      p["ffn_conv_w"], p["ffn_w_down_bf"], p["final_norm_w2"])


def _group(x3, mod, mod_row, row_w, s0, want_state, filt, tm, p):
    nseq, seq, d = x3.shape
    depth = p["w_in"].shape[0]
    x = x3.reshape(nseq * seq, d)
    kr, ki = filt
    states = []
    for l in range(depth):
        q, kf, kb, lf, lb, v, sg, u, x2 = _pre(l, x, mod, mod_row, row_w, seq, tm, p)
        o, st = _scan(l, seq, nseq, (q, kf, kb, lf, lb, v), s0, want_state)
        y = _hyena(l, seq, nseq, u, x2, kr, ki, p)
        x = _post(l, x, o, sg, y, mod, mod_row, row_w, seq, tm, l == depth - 1, p)
        states.append(st)
    return x.reshape(nseq, seq, d), states


def kernel(x_prompt, x_sample, state_hgrn, c, c_ctx, w_in, w_out, ada_w, ada_b, norm1_w, norm2_w,
           hg_lb_fwd, hg_lb_bwd, hg_norm_w, hy_conv_w, hy_w1, hy_b1, hy_freq1, hy_w2, hy_b2, hy_freq2,
           hy_w3, hy_d, hy_norm_w, ffn_w_up, ffn_conv_w, ffn_w_down, final_norm_w):
    depth, d, _ = w_in.shape
    vec3 = lambda a: a.reshape(a.shape[0], 1, a.shape[1])
    p = dict(w_in=w_in, w_in_bf=w_in.astype(BF16), w_out_bf=w_out.astype(BF16),
             ffn_w_up_bf=ffn_w_up.astype(BF16), ffn_w_down=ffn_w_down, ffn_w_down_bf=ffn_w_down.astype(BF16),
             norm1_w3=vec3(norm1_w), norm2_w3=vec3(norm2_w), hg_norm_w3=vec3(hg_norm_w),
             hy_d=hy_d, hy_d3=vec3(hy_d), hy_norm_w3=vec3(hy_norm_w), final_norm_w2=final_norm_w[None, :],
             hg_lb_fwd=hg_lb_fwd, hg_lb_bwd=hg_lb_bwd, hy_conv_w=hy_conv_w, ffn_conv_w=ffn_conv_w,
             hy_w1=hy_w1, hy_b1=hy_b1, hy_freq1=hy_freq1, hy_w2=hy_w2, hy_b2=hy_b2, hy_freq2=hy_freq2,
             hy_w3=hy_w3)

    n_dec = c.shape[0]
    cv = jnp.concatenate([c, c_ctx[None, :], jnp.zeros((MOD_ROWS - n_dec - 1, d), F32)], axis=0)
    mod = _modulation(cv, ada_w, ada_b).reshape(depth * MOD_ROWS, 1, 6 * d)

    seq_p, seq_s = x_prompt.shape[1], x_sample.shape[1]
    y_p, states = _group(x_prompt, mod, lambda l, b: l * MOD_ROWS + n_dec, seq_p, None, True,
                         _filters(seq_p, p), TOKEN_TILE, p)
    y_s, _ = _group(x_sample, mod, lambda l, b: l * MOD_ROWS + b, GRID_W, state_hgrn, False,
                    _filters(seq_s, p), TOKEN_TILE, p)
    return (y_p, y_s, jnp.stack(states, axis=1))
```

```python
import functools
import math

import numpy as np
import jax
import jax.numpy as jnp
from jax import lax
from jax.experimental import pallas as pl
from jax.experimental.pallas import tpu as pltpu

F32 = jnp.float32
BF16 = jnp.bfloat16

HG_DK = 128
CHUNK = 64
GRID_W = 64
HY_BANDS = 16
HY_TARGET = 1e-2
HY_FAST = 0.3
HY_SLOW = 1.5
EPS = 1e-6

LANES = 128
VMEM_LIMIT = 56 * 1024 * 1024

N_LEVELS = int(math.log2(CHUNK))
MOD_ROWS = 16
TOKEN_TILE = 512


def _bdot(a, b):
    return jnp.dot(a.astype(BF16), b.astype(BF16), preferred_element_type=F32)


def _bdot_nt(a, b):
    return lax.dot_general(a.astype(BF16), b.astype(BF16), (((1,), (1,)), ((), ())),
                           preferred_element_type=F32)


def _bdot_tn(a, b):
    return lax.dot_general(a.astype(BF16), b.astype(BF16), (((0,), (0,)), ((), ())),
                           preferred_element_type=F32)


def _hdot(a, b):
    return jnp.dot(a, b, preferred_element_type=F32, precision=lax.Precision.HIGHEST)


def _silu(x):
    return x * jax.nn.sigmoid(x)


def _rms(x):
    return x * lax.rsqrt(jnp.mean(x * x, axis=-1, keepdims=True) + EPS)


def _const_spec(shape, index_map):
    return pl.BlockSpec(shape, index_map, pipeline_mode=pl.Buffered(1))


def _dwconv3(p, w, row_w):
    n = p.shape[0]
    r = lax.broadcasted_iota(jnp.int32, (n, 1), 0) & (row_w - 1)
    prev = jnp.where(r == 0, 0.0, pltpu.roll(p, 1, 0))
    nxt = jnp.where(r == row_w - 1, 0.0, pltpu.roll(p, n - 1, 0))
    return w[0:1] * prev + w[1:2] * p + w[2:3] * nxt


def _mod_kernel(cv_ref, w_ref, b_ref, o_ref):
    o_ref[0] = _hdot(_silu(cv_ref[...]), w_ref[0]) + b_ref[0]


def _modulation(cv, ada_w, ada_b):
    depth, d, n = ada_w.shape
    tn = 1536
    return pl.pallas_call(
        _mod_kernel,
        grid=(depth, n // tn),
        in_specs=[
            pl.BlockSpec((MOD_ROWS, d), lambda l, j: (0, 0)),
            pl.BlockSpec((1, d, tn), lambda l, j: (l, 0, j)),
            pl.BlockSpec((1, 1, tn), lambda l, j: (l, 0, j)),
        ],
        out_specs=pl.BlockSpec((1, MOD_ROWS, tn), lambda l, j: (l, 0, j)),
        out_shape=jax.ShapeDtypeStruct((depth, MOD_ROWS, n), F32),
        compiler_params=pltpu.CompilerParams(vmem_limit_bytes=VMEM_LIMIT),
        name="modulation",
    )(cv, ada_w, ada_b.reshape(depth, 1, n))


def _dft_tables(L):
    f = np.arange(L, dtype=np.int64)[:, None]
    t = np.arange(L, dtype=np.int64)[None, :]
    ang = 2.0 * np.pi * (((2 * f + 1) * t) % (4 * L)).astype(np.float64) / (4 * L)
    return np.cos(ang), np.sin(ang)


def _filter_feats(L):
    def feats(pos):
        t = pos / (L - 1)
        bands = np.linspace(1e-4, HY_BANDS - 1, HY_BANDS)[None, :]
        ang = 2.0 * np.pi * pos[:, None] * bands / L
        out = np.zeros((pos.shape[0], LANES), np.float64)
        out[:, 0] = t
        out[:, 1:1 + HY_BANDS] = np.cos(ang)
        out[:, 1 + HY_BANDS:1 + 2 * HY_BANDS] = np.sin(ang)
        return out
    pos = np.arange(L, dtype=np.float64)
    return np.concatenate([feats(pos), feats(L - pos)], axis=0).astype(np.float32)


def _filter_taps_kernel(L, hw, ft_ref, w1_ref, b1_ref, f1_ref, w2_ref, b2_ref, f2_ref, w3_ref,
                        dl_ref, kk_ref):
    ft = ft_ref[...]
    h = jnp.sin(f1_ref[0] * (_hdot(ft, w1_ref[0]) + b1_ref[0]))
    h = jnp.sin(f2_ref[0] * (_hdot(h, w2_ref[0]) + b2_ref[0]))
    h = _hdot(h, w3_ref[0])
    dl = dl_ref[...]
    t = ft[:, 0:1]
    k1 = h[:L, :hw] * jnp.exp(-t[:L] * dl)
    row = lax.broadcasted_iota(jnp.int32, (L, 1), 0)
    k2 = jnp.where(row == 0, 0.0, -(h[L:, hw:] * jnp.exp(-t[L:] * dl)))
    norm = jnp.sum(jnp.abs(k1), axis=0, keepdims=True) + jnp.sum(jnp.abs(k2), axis=0, keepdims=True) + EPS
    kk_ref[0, :, :hw] = k1 / norm
    kk_ref[0, :, hw:] = k2 / norm


def _filter_dft_kernel(hw, kk_ref, cm_ref, sm_ref, kr_ref, ki_ref):
    kk = kk_ref[0]
    gc = _hdot(cm_ref[...], kk)
    gs = _hdot(sm_ref[...], kk)
    tf = gc.shape[0]
    row = lax.broadcasted_iota(jnp.int32, (tf, 1), 0)
    alt = jnp.where((row & 1) == 0, 1.0, -1.0)
    kr_ref[0] = gc[:, :hw] - alt * gs[:, hw:]
    ki_ref[0] = -(gs[:, :hw] + alt * gc[:, hw:])


def _filters(L, p):
    depth, _, hw2 = p["hy_w3"].shape
    hw = hw2 // 2
    cm, sm = _dft_tables(L)
    ft = jnp.asarray(_filter_feats(L))
    min_decay = math.log(HY_TARGET) / HY_SLOW
    max_decay = math.log(HY_TARGET) / HY_FAST
    deltas = jnp.asarray(np.abs(np.linspace(min_decay, max_decay, hw)).astype(np.float32)[None, :])

    def pad_to(a, rows, cols):
        return jnp.pad(a, ((0, 0), (0, rows - a.shape[1]), (0, cols - a.shape[2])))

    w1 = pad_to(p["hy_w1"], LANES, LANES)
    w2 = pad_to(p["hy_w2"], LANES, LANES)
    w3 = pad_to(p["hy_w3"], LANES, hw2)
    vec = lambda a: pad_to(a[:, None, :], 1, LANES)
    lay = lambda shape: pl.BlockSpec((1,) + shape, lambda l: (l, 0, 0))
    kk = pl.pallas_call(
        functools.partial(_filter_taps_kernel, L, hw),
        grid=(depth,),
        in_specs=[
            _const_spec((2 * L, LANES), lambda l: (0, 0)),
            lay((LANES, LANES)), lay((1, LANES)), lay((1, LANES)),
            lay((LANES, LANES)), lay((1, LANES)), lay((1, LANES)),
            lay((LANES, hw2)),
            _const_spec((1, hw), lambda l: (0, 0)),
        ],
        out_specs=lay((L, hw2)),
        out_shape=jax.ShapeDtypeStruct((depth, L, hw2), F32),
        compiler_params=pltpu.CompilerParams(vmem_limit_bytes=VMEM_LIMIT),
        name=f"hyena_filter_taps_{L}",
    )(ft, w1, vec(p["hy_b1"]), vec(p["hy_freq1"]), w2, vec(p["hy_b2"]), vec(p["hy_freq2"]), w3, deltas)
    tf = 256
    frq = pl.BlockSpec((tf, L), lambda l, j: (j, 0))
    out = pl.BlockSpec((1, tf, hw), lambda l, j: (l, j, 0))
    return pl.pallas_call(
        functools.partial(_filter_dft_kernel, hw),
        grid=(depth, L // tf),
        in_specs=[pl.BlockSpec((1, L, hw2), lambda l, j: (l, 0, 0)), frq, frq],
        out_specs=[out, out],
        out_shape=[jax.ShapeDtypeStruct((depth, L, hw), F32)] * 2,
        compiler_params=pltpu.CompilerParams(vmem_limit_bytes=VMEM_LIMIT),
        name=f"hyena_filter_dft_{L}",
    )(kk, jnp.asarray(cm, F32), jnp.asarray(sm, F32))


def _lower_bound(lb_ref, l):
    prm = lb_ref[...]
    e = jnp.exp(prm - jnp.max(prm, axis=0, keepdims=True))
    s = e / jnp.sum(e, axis=0, keepdims=True)
    acc = jnp.zeros_like(s[0:1])
    for i in range(1, l + 1):
        acc = acc + s[i:i + 1]
    return acc


def _log_forget(x, lb):
    ls = jnp.minimum(x, 0.0) - jnp.log1p(jnp.exp(-jnp.abs(x)))
    a = jnp.log(lb)
    b = jnp.log1p(-lb) + ls
    return jnp.maximum(a, b) + jnp.log1p(jnp.exp(-jnp.abs(a - b)))


def _pre_kernel(l, row_w, hg, hy, x_ref, mod_ref, n1_ref, win_ref, lbf_ref, lbb_ref, cw_ref,
                q_ref, kf_ref, kb_ref, lf_ref, lb_ref, v_ref, sg_ref, u_ref, x2_ref):
    d = x_ref.shape[1]
    x = x_ref[...]
    m = mod_ref[0]
    sh1, sc1 = m[:, 0:d], m[:, d:2 * d]
    h = (_rms(x) * n1_ref[0] * (1.0 + sc1) + sh1).astype(BF16)

    def proj(i, width):
        return jnp.dot(h, win_ref[0, :, i:i + width], preferred_element_type=F32)

    q_ref[...] = _silu(proj(0, hg)) * (HG_DK ** -0.5)
    for i, (lbp_ref, k_ref, g_ref) in enumerate(((lbf_ref, kf_ref, lf_ref), (lbb_ref, kb_ref, lb_ref))):
        lb = _lower_bound(lbp_ref, l)
        z = proj((1 + i) * hg, hg)
        g_ref[...] = _log_forget(z, lb)
        k_ref[...] = (1.0 - lb) * jax.nn.sigmoid(-z)
    v_ref[...] = proj(3 * hg, hg)
    sg_ref[...] = _silu(proj(4 * hg, hg))
    cw = cw_ref[0]
    conv = lambda j: _dwconv3(proj(5 * hg + j * hy, hy), cw[:, j * hy:(j + 1) * hy], row_w)
    u_ref[...] = conv(1) * conv(0)
    x2_ref[...] = conv(2)


def _pre(l, x, mod, mod_row, row_w, seq, tm, p):
    t, d = x.shape
    hg = p["hg_lb_fwd"].shape[1]
    hy = p["hy_d"].shape[1]
    n_in = p["w_in"].shape[2]
    depth = p["w_in"].shape[0]
    tok = lambda w: pl.BlockSpec((tm, w), lambda i: (i, 0))
    out = jax.ShapeDtypeStruct((t, hg), F32)
    return pl.pallas_call(
        functools.partial(_pre_kernel, l, row_w, hg, hy),
        grid=(t // tm,),
        in_specs=[
            tok(d),
            pl.BlockSpec((1, 1, mod.shape[2]), lambda i: (mod_row(l, i * tm // seq), 0, 0)),
            _const_spec((1, 1, d), lambda i: (l, 0, 0)),
            _const_spec((1, d, n_in), lambda i: (l, 0, 0)),
            _const_spec((depth, hg), lambda i: (0, 0)),
            _const_spec((depth, hg), lambda i: (0, 0)),
            _const_spec((1, 3, 3 * hy), lambda i: (l, 0, 0)),
        ],
        out_specs=[tok(hg)] * 9,
        out_shape=[out] * 9,
        compiler_params=pltpu.CompilerParams(vmem_limit_bytes=VMEM_LIMIT),
        name="pre_mixer",
    )(x, mod, p["norm1_w3"], p["w_in_bf"], p["hg_lb_fwd"], p["hg_lb_bwd"], p["hy_conv_w"])


LEVELS = tuple(CHUNK >> (i + 1) for i in range(N_LEVELS))
SUBLANES = 8
MXU_LEVELS = tuple(s for s in LEVELS if 1 < s < SUBLANES)
LOG2E = 1.4426950408889634


def _scan_tables():
    c = CHUNK
    j = np.arange(c)[:, None]
    i = np.arange(c)[None, :]
    masks = [((j // (2 * s) == i // (2 * s)) & (j % (2 * s) >= s) & (i % (2 * s) < s)).astype(np.float32)
             for s in LEVELS]
    masks.append(np.eye(c, dtype=np.float32))
    mf = np.stack(masks)
    blocks = [np.tril(np.ones((c, c), np.float32))]
    for s in MXU_LEVELS:
        w = np.zeros((c, c), np.float32)
        for t in range(c):
            ref = (t // (2 * s)) * 2 * s + s - 1
            if t % (2 * s) >= s:
                w[t, ref + 1:t + 1] = 1.0
            else:
                w[t, t + 1:ref + 1] = 1.0
        blocks.append(w)
    fwd = np.concatenate(blocks, axis=0)
    bwd = np.concatenate([w[::-1, ::-1] for w in blocks], axis=0)
    sums = np.stack([np.concatenate([fwd] * 3, axis=1), np.concatenate([bwd] * 3, axis=1)])
    return sums, np.stack([mf, mf[:, ::-1, ::-1]])


def _decay_logs(sums, g, d):
    c = g.shape[0]
    fwd = d == 0
    b = sums[:c]
    out = []
    for s in LEVELS:
        if s >= SUBLANES:
            parts = []
            for b0 in range(0, c, 2 * s):
                ref = b0 + s - 1 if fwd else b0 + s
                br, lo, hi = b[ref:ref + 1], b[b0:b0 + s], b[b0 + s:b0 + 2 * s]
                parts += [br - lo, hi - br] if fwd else [lo - br, br - hi]
            out.append(jnp.concatenate(parts, axis=0))
        elif s in MXU_LEVELS:
            i = 1 + MXU_LEVELS.index(s)
            out.append(sums[i * c:(i + 1) * c])
        else:
            r = lax.broadcasted_iota(jnp.int32, (c, 1), 0)
            out.append(jnp.where((r & 1) == (1 if fwd else 0), g, 0.0))
    far = b[c - 1:c] if fwd else b[0:1]
    return out + [b, far - b]


def _scan_kernel(nc, heads, has_s0, want_state, *refs):
    q_ref, kf_ref, kb_ref, lf_ref, lb_ref, v_ref = refs[:6]
    refs = refs[6:]
    if has_s0:
        s0_ref, refs = refs[0], refs[1:]
    sum_ref, m_ref, o_ref = refs[:3]
    refs = refs[3:]
    if want_state:
        st_ref, refs = refs[0], refs[1:]
    s_scr, ob_scr, qs_scr, ks_scr, v_scr, ea_scr = refs
    c = CHUNK
    dirs = ((kf_ref, lf_ref), (kb_ref, lb_ref))
    pairs = [(d, h) for d in range(2) for h in range(heads)]
    cols = lambda h: slice(h * HG_DK, (h + 1) * HG_DK)

    def rows_of(ci, d):
        return pl.ds((ci if d == 0 else nc - 1 - ci) * c, c)

    def prep_start(ci):
        out = []
        for d, (k_ref, g_ref) in enumerate(dirs):
            rows = rows_of(ci, d)
            g = g_ref[rows, :] * LOG2E
            g1 = g.astype(BF16)
            r1 = g - g1.astype(F32)
            g2 = r1.astype(BF16)
            g3 = (r1 - g2.astype(F32)).astype(BF16)
            sums = jnp.dot(sum_ref[d], jnp.concatenate([g1, g2, g3], axis=0), preferred_element_type=F32)
            out.append((g, q_ref[rows, :], k_ref[rows, :], v_ref[rows, :], sums))
        return out

    def prep_finish(slot, prepped):
        for d, (g, q, k, v, sums) in enumerate(prepped):
            e = [jnp.exp2(x) for x in _decay_logs(sums, g, d)]
            for s in range(N_LEVELS):
                qs_scr[slot, d, s] = (q * e[s]).astype(BF16)
                ks_scr[slot, d, s] = (k * e[s]).astype(BF16)
            qs_scr[slot, d, N_LEVELS] = q.astype(BF16)
            ks_scr[slot, d, N_LEVELS] = k.astype(BF16)
            qs_scr[slot, d, N_LEVELS + 1] = (q * e[N_LEVELS]).astype(BF16)
            ks_scr[slot, d, N_LEVELS + 1] = (k * e[N_LEVELS + 1]).astype(BF16)
            v_scr[slot, d] = v.astype(BF16)
            ea_scr[slot, d, 0:1, :] = e[N_LEVELS][c - 1:c] if d == 0 else e[N_LEVELS][0:1]

    def use_start(slot):
        scores, inter, new_state, vals = {}, {}, {}, {}
        for d, h in pairs:
            scores[d, h] = [_bdot_nt(qs_scr[slot, d, s, :, cols(h)], ks_scr[slot, d, s, :, cols(h)])
                            for s in range(N_LEVELS + 1)]
            st = s_scr[d, h]
            vals[d, h] = v_scr[slot, d, :, cols(h)]
            inter[d, h] = _bdot_nt(qs_scr[slot, d, N_LEVELS + 1, :, cols(h)], st)
            new_state[d, h] = (ea_scr[slot, d, 0:1, cols(h)] * st
                               + _bdot_tn(vals[d, h], ks_scr[slot, d, N_LEVELS + 1, :, cols(h)]))
        return scores, inter, new_state, vals

    def use_finish(ci, scores, inter, new_state, vals):
        for d in range(2):
            keep = [m_ref[d, s] > 0.5 for s in range(N_LEVELS + 1)]
            for h in range(heads):
                a = jnp.where(keep[N_LEVELS], scores[d, h][N_LEVELS], 0.0)
                for s in range(N_LEVELS):
                    a = jnp.where(keep[s], scores[d, h][s], a)
                o = _bdot(a, vals[d, h]) + inter[d, h]
                (o_ref if d == 0 else ob_scr)[rows_of(ci, d), cols(h)] = o
                s_scr[d, h] = new_state[d, h]

    for d, h in pairs:
        s_scr[d, h] = s0_ref[0, 0, d, h].T if has_s0 else jnp.zeros(s_scr.shape[2:], F32)

    prep_finish(0, prep_start(0))
    for ci in range(nc):
        slot = ci % 2
        prepped = prep_start(ci + 1) if ci + 1 < nc else None
        part = use_start(slot)
        if prepped is not None:
            prep_finish(1 - slot, prepped)
        use_finish(ci, *part)
    o_ref[...] += ob_scr[...]
    if want_state:
        for d in range(2):
            for h in range(heads):
                st_ref[0, d, h] = s_scr[d, h].T


def _scan(l, seq, nseq, arrs, s0, want_state):
    t, hg = arrs[0].shape
    heads = hg // HG_DK
    nc = seq // CHUNK
    sums, masks = _scan_tables()
    sums, masks = jnp.asarray(sums, F32).astype(BF16), jnp.asarray(masks)
    seq_spec = pl.BlockSpec((seq, hg), lambda b: (b, 0))
    in_specs = [seq_spec] * 6
    args = list(arrs)
    if s0 is not None:
        in_specs.append(pl.BlockSpec((1, 1) + s0.shape[2:], lambda b: (b, l, 0, 0, 0, 0)))
        args.append(s0)
    in_specs += [_const_spec(sums.shape, lambda b: (0, 0, 0)), _const_spec(masks.shape, lambda b: (0, 0, 0, 0))]
    args += [sums, masks]
    out_specs = [seq_spec]
    out_shape = [jax.ShapeDtypeStruct((t, hg), F32)]
    if want_state:
        out_specs.append(pl.BlockSpec((1, 2, heads, HG_DK, HG_DK), lambda b: (b, 0, 0, 0, 0)))
        out_shape.append(jax.ShapeDtypeStruct((nseq, 2, heads, HG_DK, HG_DK), F32))
    res = pl.pallas_call(
        functools.partial(_scan_kernel, nc, heads, s0 is not None, want_state),
        grid=(nseq,),
        in_specs=in_specs,
        out_specs=out_specs,
        out_shape=out_shape,
        scratch_shapes=[pltpu.VMEM((2, heads, HG_DK, HG_DK), F32), pltpu.VMEM((seq, hg), F32),
                        pltpu.VMEM((2, 2, N_LEVELS + 2, CHUNK, hg), BF16),
                        pltpu.VMEM((2, 2, N_LEVELS + 2, CHUNK, hg), BF16),
                        pltpu.VMEM((2, 2, CHUNK, hg), BF16), pltpu.VMEM((2, 2, SUBLANES, hg), F32)],
        compiler_params=pltpu.CompilerParams(vmem_limit_bytes=VMEM_LIMIT),
        name="hgrn2_scan",
    )(*args)
    return res if want_state else (res[0], None)


def _hyena_kernel(L, u_ref, x2_ref, kr_ref, ki_ref, d_ref, nw_ref, cs_ref, ics_ref, y_ref):
    u = u_ref[...]
    uu = jnp.dot(cs_ref[...], u.astype(BF16), preferred_element_type=F32)
    ur, us = uu[:L], uu[L:]
    kr, ki = kr_ref[0], ki_ref[0]
    zr = ur * kr + us * ki
    zi = ur * ki - us * kr
    zz = jnp.concatenate([zr.astype(BF16), zi.astype(BF16)], axis=0)
    z = jnp.dot(ics_ref[...], zz, preferred_element_type=F32)
    y = x2_ref[...] * (z + d_ref[0] * u)
    y_ref[...] = _rms(y) * nw_ref[0]


def _hyena(l, seq, nseq, u, x2, kr, ki, p):
    t, hw = u.shape
    cm, sm = _dft_tables(seq)
    cs = jnp.asarray(np.concatenate([cm, sm], axis=0), F32).astype(BF16)
    ics = jnp.asarray(np.concatenate([cm.T, -sm.T], axis=1) / seq, F32).astype(BF16)
    seq_spec = pl.BlockSpec((seq, hw), lambda b: (b, 0))
    lay = lambda shape: _const_spec((1,) + shape, lambda b: (l, 0, 0))
    return pl.pallas_call(
        functools.partial(_hyena_kernel, seq),
        grid=(nseq,),
        in_specs=[seq_spec, seq_spec, lay((seq, hw)), lay((seq, hw)), lay((1, hw)), lay((1, hw)),
                  _const_spec(cs.shape, lambda b: (0, 0)), _const_spec(ics.shape, lambda b: (0, 0))],
        out_specs=seq_spec,
        out_shape=jax.ShapeDtypeStruct((t, hw), F32),
        compiler_params=pltpu.CompilerParams(vmem_limit_bytes=VMEM_LIMIT),
        name="hyena_conv",
    )(u, x2, kr, ki, p["hy_d3"], p["hy_norm_w3"], cs, ics)


def _post_kernel(row_w, heads, ff_chunk, last, x_ref, o_ref, sg_ref, y_ref, mod_ref, hn_ref, wout_ref,
                 n2_ref, wup_ref, cw_ref, wdn_ref, fn_ref, out_ref):
    d = x_ref.shape[1]
    hg = o_ref.shape[1]
    dff = wdn_ref.shape[1]
    m = mod_ref[0]
    g1, sh2, sc2, g2 = (m[:, i * d:(i + 1) * d] for i in range(2, 6))
    o = o_ref[...]
    hn = hn_ref[0]
    sg = sg_ref[...]
    mix = _bdot(y_ref[...], wout_ref[0, hg:, :])
    for h in range(heads):
        cols = slice(h * HG_DK, (h + 1) * HG_DK)
        mix = mix + _bdot(_rms(o[:, cols]) * hn[:, cols] * sg[:, cols], wout_ref[0, cols, :])
    x = x_ref[...] + g1 * mix
    h2 = (_rms(x) * n2_ref[0] * (1.0 + sc2) + sh2).astype(BF16)
    cw = cw_ref[0]
    ffn = jnp.zeros_like(x)
    n_chunks = dff // ff_chunk

    def up(j):
        a, b = j * ff_chunk, (j + 1) * ff_chunk
        return (jnp.dot(h2, wup_ref[0, :, a:b], preferred_element_type=F32),
                jnp.dot(h2, wup_ref[0, :, dff + a:dff + b], preferred_element_type=F32))

    ahead = up(0)
    for j in range(n_chunks):
        a, b = j * ff_chunk, (j + 1) * ff_chunk
        gate, val = ahead
        if j + 1 < n_chunks:
            ahead = up(j + 1)
        gate = _dwconv3(gate, cw[:, a:b], row_w)
        val = _dwconv3(val, cw[:, dff + a:dff + b], row_w)
        ffn = ffn + _bdot(_silu(gate) * val, wdn_ref[0, a:b, :])
    x = x + g2 * ffn
    out_ref[...] = _rms(x) * fn_ref[...] if last else x


def _post(l, x, o, sg, y, mod, mod_row, row_w, seq, tm, last, p):
    t, d = x.shape
    hg = o.shape[1]
    dff = p["ffn_w_down"].shape[1]
    tok = lambda w: pl.BlockSpec((tm, w), lambda i: (i, 0))
    lay = lambda shape: _const_spec((1,) + shape, lambda i: (l, 0, 0))
    return pl.pallas_call(
        functools.partial(_post_kernel, row_w, hg // HG_DK, 256, last),
        grid=(t // tm,),
        in_specs=[
            tok(d), tok(hg), tok(hg), tok(y.shape[1]),
            pl.BlockSpec((1, 1, mod.shape[2]), lambda i: (mod_row(l, i * tm // seq), 0, 0)),
            lay((1, hg)), lay((d, d)), lay((1, d)), lay((d, 2 * dff)), lay((3, 2 * dff)), lay((dff, d)),
            _const_spec((1, d), lambda i: (0, 0)),
        ],
        out_specs=tok(d),
        out_shape=jax.ShapeDtypeStruct((t, d), F32),
        compiler_params=pltpu.CompilerParams(vmem_limit_bytes=VMEM_LIMIT),
        name="post_mixer_ffn",
    )(x, o, sg, y, mod, p["hg_norm_w3"], p["w_out_bf"], p["norm2_w3"], p["ffn_w_up_bf"],
      p["ffn_conv_w"], p["ffn_w_down_bf"], p["final_norm_w2"])


def _group(x3, mod, mod_row, row_w, s0, want_state, filt, tm, p):
    nseq, seq, d = x3.shape
    depth = p["w_in"].shape[0]
    x = x3.reshape(nseq * seq, d)
    kr, ki = filt
    states = []
    for l in range(depth):
        q, kf, kb, lf, lb, v, sg, u, x2 = _pre(l, x, mod, mod_row, row_w, seq, tm, p)
        o, st = _scan(l, seq, nseq, (q, kf, kb, lf, lb, v), s0, want_state)
        y = _hyena(l, seq, nseq, u, x2, kr, ki, p)
        x = _post(l, x, o, sg, y, mod, mod_row, row_w, seq, tm, l == depth - 1, p)
        states.append(st)
    return x.reshape(nseq, seq, d), states


def kernel(x_prompt, x_sample, state_hgrn, c, c_ctx, w_in, w_out, ada_w, ada_b, norm1_w, norm2_w,
           hg_lb_fwd, hg_lb_bwd, hg_norm_w, hy_conv_w, hy_w1, hy_b1, hy_freq1, hy_w2, hy_b2, hy_freq2,
           hy_w3, hy_d, hy_norm_w, ffn_w_up, ffn_conv_w, ffn_w_down, final_norm_w):
    depth, d, _ = w_in.shape
    vec3 = lambda a: a.reshape(a.shape[0], 1, a.shape[1])
    p = dict(w_in=w_in, w_in_bf=w_in.astype(BF16), w_out_bf=w_out.astype(BF16),
             ffn_w_up_bf=ffn_w_up.astype(BF16), ffn_w_down=ffn_w_down, ffn_w_down_bf=ffn_w_down.astype(BF16),
             norm1_w3=vec3(norm1_w), norm2_w3=vec3(norm2_w), hg_norm_w3=vec3(hg_norm_w),
             hy_d=hy_d, hy_d3=vec3(hy_d), hy_norm_w3=vec3(hy_norm_w), final_norm_w2=final_norm_w[None, :],
             hg_lb_fwd=hg_lb_fwd, hg_lb_bwd=hg_lb_bwd, hy_conv_w=hy_conv_w, ffn_conv_w=ffn_conv_w,
             hy_w1=hy_w1, hy_b1=hy_b1, hy_freq1=hy_freq1, hy_w2=hy_w2, hy_b2=hy_b2, hy_freq2=hy_freq2,
             hy_w3=hy_w3)

    n_dec = c.shape[0]
    cv = jnp.concatenate([c, c_ctx[None, :], jnp.zeros((MOD_ROWS - n_dec - 1, d), F32)], axis=0)
    mod = _modulation(cv, ada_w, ada_b).reshape(depth * MOD_ROWS, 1, 6 * d)

    seq_p, seq_s = x_prompt.shape[1], x_sample.shape[1]
    y_p, states = _group(x_prompt, mod, lambda l, b: l * MOD_ROWS + n_dec, seq_p, None, True,
                         _filters(seq_p, p), TOKEN_TILE, p)
    y_s, _ = _group(x_sample, mod, lambda l, b: l * MOD_ROWS + b, GRID_W, state_hgrn, False,
                    _filters(seq_s, p), TOKEN_TILE, p)
    return (y_p, y_s, jnp.stack(states, axis=1))
```

```python
import functools
import math

import numpy as np
import jax
import jax.numpy as jnp
from jax import lax
from jax.experimental import pallas as pl
from jax.experimental.pallas import tpu as pltpu

F32 = jnp.float32
BF16 = jnp.bfloat16

HG_DK = 128
CHUNK = 64
GRID_W = 64
HY_BANDS = 16
HY_TARGET = 1e-2
HY_FAST = 0.3
HY_SLOW = 1.5
EPS = 1e-6

LANES = 128
VMEM_LIMIT = 56 * 1024 * 1024

N_LEVELS = int(math.log2(CHUNK))
MOD_ROWS = 16
TOKEN_TILE = 512


def _bdot(a, b):
    return jnp.dot(a.astype(BF16), b.astype(BF16), preferred_element_type=F32)


def _bdot_nt(a, b):
    return lax.dot_general(a.astype(BF16), b.astype(BF16), (((1,), (1,)), ((), ())),
                           preferred_element_type=F32)


def _bdot_tn(a, b):
    return lax.dot_general(a.astype(BF16), b.astype(BF16), (((0,), (0,)), ((), ())),
                           preferred_element_type=F32)


def _hdot(a, b):
    return jnp.dot(a, b, preferred_element_type=F32, precision=lax.Precision.HIGHEST)


def _silu(x):
    return x * jax.nn.sigmoid(x)


def _rms(x):
    return x * lax.rsqrt(jnp.mean(x * x, axis=-1, keepdims=True) + EPS)


def _const_spec(shape, index_map):
    return pl.BlockSpec(shape, index_map, pipeline_mode=pl.Buffered(1))


def _dwconv3(p, w, row_w):
    n = p.shape[0]
    r = lax.broadcasted_iota(jnp.int32, (n, 1), 0) & (row_w - 1)
    prev = jnp.where(r == 0, 0.0, pltpu.roll(p, 1, 0))
    nxt = jnp.where(r == row_w - 1, 0.0, pltpu.roll(p, n - 1, 0))
    return w[0:1] * prev + w[1:2] * p + w[2:3] * nxt


def _mod_kernel(cv_ref, w_ref, b_ref, o_ref):
    o_ref[0] = _hdot(_silu(cv_ref[...]), w_ref[0]) + b_ref[0]


def _modulation(cv, ada_w, ada_b):
    depth, d, n = ada_w.shape
    tn = 1536
    return pl.pallas_call(
        _mod_kernel,
        grid=(depth, n // tn),
        in_specs=[
            pl.BlockSpec((MOD_ROWS, d), lambda l, j: (0, 0)),
            pl.BlockSpec((1, d, tn), lambda l, j: (l, 0, j)),
            pl.BlockSpec((1, 1, tn), lambda l, j: (l, 0, j)),
        ],
        out_specs=pl.BlockSpec((1, MOD_ROWS, tn), lambda l, j: (l, 0, j)),
        out_shape=jax.ShapeDtypeStruct((depth, MOD_ROWS, n), F32),
        compiler_params=pltpu.CompilerParams(vmem_limit_bytes=VMEM_LIMIT),
        name="modulation",
    )(cv, ada_w, ada_b.reshape(depth, 1, n))


def _dft_tables(L):
    f = np.arange(L, dtype=np.int64)[:, None]
    t = np.arange(L, dtype=np.int64)[None, :]
    ang = 2.0 * np.pi * (((2 * f + 1) * t) % (4 * L)).astype(np.float64) / (4 * L)
    return np.cos(ang), np.sin(ang)


def _filter_feats(L):
    def feats(pos):
        t = pos / (L - 1)
        bands = np.linspace(1e-4, HY_BANDS - 1, HY_BANDS)[None, :]
        ang = 2.0 * np.pi * pos[:, None] * bands / L
        out = np.zeros((pos.shape[0], LANES), np.float64)
        out[:, 0] = t
        out[:, 1:1 + HY_BANDS] = np.cos(ang)
        out[:, 1 + HY_BANDS:1 + 2 * HY_BANDS] = np.sin(ang)
        return out
    pos = np.arange(L, dtype=np.float64)
    return np.concatenate([feats(pos), feats(L - pos)], axis=0).astype(np.float32)


def _filter_taps_kernel(L, hw, ft_ref, w1_ref, b1_ref, f1_ref, w2_ref, b2_ref, f2_ref, w3_ref,
                        dl_ref, kk_ref):
    ft = ft_ref[...]
    h = jnp.sin(f1_ref[0] * (_hdot(ft, w1_ref[0]) + b1_ref[0]))
    h = jnp.sin(f2_ref[0] * (_hdot(h, w2_ref[0]) + b2_ref[0]))
    h = _hdot(h, w3_ref[0])
    dl = dl_ref[...]
    t = ft[:, 0:1]
    k1 = h[:L, :hw] * jnp.exp(-t[:L] * dl)
    row = lax.broadcasted_iota(jnp.int32, (L, 1), 0)
    k2 = jnp.where(row == 0, 0.0, -(h[L:, hw:] * jnp.exp(-t[L:] * dl)))
    norm = jnp.sum(jnp.abs(k1), axis=0, keepdims=True) + jnp.sum(jnp.abs(k2), axis=0, keepdims=True) + EPS
    kk_ref[0, :, :hw] = k1 / norm
    kk_ref[0, :, hw:] = k2 / norm


def _filter_dft_kernel(hw, kk_ref, cm_ref, sm_ref, kr_ref, ki_ref):
    kk = kk_ref[0]
    gc = _hdot(cm_ref[...], kk)
    gs = _hdot(sm_ref[...], kk)
    tf = gc.shape[0]
    row = lax.broadcasted_iota(jnp.int32, (tf, 1), 0)
    alt = jnp.where((row & 1) == 0, 1.0, -1.0)
    kr_ref[0] = gc[:, :hw] - alt * gs[:, hw:]
    ki_ref[0] = -(gs[:, :hw] + alt * gc[:, hw:])


def _filters(L, p):
    depth, _, hw2 = p["hy_w3"].shape
    hw = hw2 // 2
    cm, sm = _dft_tables(L)
    ft = jnp.asarray(_filter_feats(L))
    min_decay = math.log(HY_TARGET) / HY_SLOW
    max_decay = math.log(HY_TARGET) / HY_FAST
    deltas = jnp.asarray(np.abs(np.linspace(min_decay, max_decay, hw)).astype(np.float32)[None, :])

    def pad_to(a, rows, cols):
        return jnp.pad(a, ((0, 0), (0, rows - a.shape[1]), (0, cols - a.shape[2])))

    w1 = pad_to(p["hy_w1"], LANES, LANES)
    w2 = pad_to(p["hy_w2"], LANES, LANES)
    w3 = pad_to(p["hy_w3"], LANES, hw2)
    vec = lambda a: pad_to(a[:, None, :], 1, LANES)
    lay = lambda shape: pl.BlockSpec((1,) + shape, lambda l: (l, 0, 0))
    kk = pl.pallas_call(
        functools.partial(_filter_taps_kernel, L, hw),
        grid=(depth,),
        in_specs=[
            _const_spec((2 * L, LANES), lambda l: (0, 0)),
            lay((LANES, LANES)), lay((1, LANES)), lay((1, LANES)),
            lay((LANES, LANES)), lay((1, LANES)), lay((1, LANES)),
            lay((LANES, hw2)),
            _const_spec((1, hw), lambda l: (0, 0)),
        ],
        out_specs=lay((L, hw2)),
        out_shape=jax.ShapeDtypeStruct((depth, L, hw2), F32),
        compiler_params=pltpu.CompilerParams(vmem_limit_bytes=VMEM_LIMIT),
        name=f"hyena_filter_taps_{L}",
    )(ft, w1, vec(p["hy_b1"]), vec(p["hy_freq1"]), w2, vec(p["hy_b2"]), vec(p["hy_freq2"]), w3, deltas)
    tf = 256
    frq = pl.BlockSpec((tf, L), lambda l, j: (j, 0))
    out = pl.BlockSpec((1, tf, hw), lambda l, j: (l, j, 0))
    return pl.pallas_call(
        functools.partial(_filter_dft_kernel, hw),
        grid=(depth, L // tf),
        in_specs=[pl.BlockSpec((1, L, hw2), lambda l, j: (l, 0, 0)), frq, frq],
        out_specs=[out, out],
        out_shape=[jax.ShapeDtypeStruct((depth, L, hw), F32)] * 2,
        compiler_params=pltpu.CompilerParams(vmem_limit_bytes=VMEM_LIMIT),
        name=f"hyena_filter_dft_{L}",
    )(kk, jnp.asarray(cm, F32), jnp.asarray(sm, F32))


def _lower_bound(lb_ref, l):
    prm = lb_ref[...]
    e = jnp.exp(prm - jnp.max(prm, axis=0, keepdims=True))
    s = e / jnp.sum(e, axis=0, keepdims=True)
    acc = jnp.zeros_like(s[0:1])
    for i in range(1, l + 1):
        acc = acc + s[i:i + 1]
    return acc


def _forget_gate(z, lb):
    t = jnp.exp(-jnp.abs(z))
    one_t = 1.0 + t
    log_sig = jnp.minimum(z, 0.0) - jnp.log(one_t)
    a = jnp.log(lb)
    b = jnp.log1p(-lb) + log_sig
    log_f = jnp.maximum(a, b) + jnp.log(1.0 + jnp.exp(-jnp.abs(a - b)))
    return log_f, (1.0 - lb) * (jnp.where(z >= 0.0, t, 1.0) / one_t)


def _pre_kernel(l, row_w, hg, hy, x_ref, mod_ref, n1_ref, win_ref, lbf_ref, lbb_ref, cw_ref,
                q_ref, kf_ref, kb_ref, lf_ref, lb_ref, v_ref, sg_ref, u_ref, x2_ref, h_scr):
    d = x_ref.shape[1]
    m = mod_ref[0]
    sh1, sc1 = m[:, 0:d], m[:, d:2 * d]
    h_scr[...] = (_rms(x_ref[...]) * n1_ref[0] * (1.0 + sc1) + sh1).astype(BF16)

    def proj(i, width):
        return jnp.dot(h_scr[...], win_ref[0, :, i:i + width], preferred_element_type=F32)

    q_ref[...] = _silu(proj(0, hg)) * (HG_DK ** -0.5)
    for i, (lbp_ref, k_ref, g_ref) in enumerate(((lbf_ref, kf_ref, lf_ref), (lbb_ref, kb_ref, lb_ref))):
        g_ref[...], k_ref[...] = _forget_gate(proj((1 + i) * hg, hg), _lower_bound(lbp_ref, l))
    v_ref[...] = proj(3 * hg, hg)
    sg_ref[...] = _silu(proj(4 * hg, hg))
    cw = cw_ref[0]
    conv = lambda j: _dwconv3(proj(5 * hg + j * hy, hy), cw[:, j * hy:(j + 1) * hy], row_w)
    u_ref[...] = conv(1) * conv(0)
    x2_ref[...] = conv(2)


def _pre(l, x, mod, mod_row, row_w, seq, tm, p):
    t, d = x.shape
    hg = p["hg_lb_fwd"].shape[1]
    hy = p["hy_d"].shape[1]
    n_in = p["w_in"].shape[2]
    depth = p["w_in"].shape[0]
    tok = lambda w: pl.BlockSpec((tm, w), lambda i: (i, 0))
    out = jax.ShapeDtypeStruct((t, hg), F32)
    return pl.pallas_call(
        functools.partial(_pre_kernel, l, row_w, hg, hy),
        grid=(t // tm,),
        in_specs=[
            tok(d),
            pl.BlockSpec((1, 1, mod.shape[2]), lambda i: (mod_row(l, i * tm // seq), 0, 0)),
            _const_spec((1, 1, d), lambda i: (l, 0, 0)),
            _const_spec((1, d, n_in), lambda i: (l, 0, 0)),
            _const_spec((depth, hg), lambda i: (0, 0)),
            _const_spec((depth, hg), lambda i: (0, 0)),
            _const_spec((1, 3, 3 * hy), lambda i: (l, 0, 0)),
        ],
        out_specs=[tok(hg)] * 9,
        out_shape=[out] * 9,
        scratch_shapes=[pltpu.VMEM((tm, d), BF16)],
        compiler_params=pltpu.CompilerParams(vmem_limit_bytes=VMEM_LIMIT),
        name="pre_mixer",
    )(x, mod, p["norm1_w3"], p["w_in_bf"], p["hg_lb_fwd"], p["hg_lb_bwd"], p["hy_conv_w"])


LEVELS = tuple(CHUNK >> (i + 1) for i in range(N_LEVELS))
SUBLANES = 8
MXU_LEVELS = tuple(s for s in LEVELS if 1 < s < SUBLANES)
LOG2E = 1.4426950408889634


def _scan_tables():
    c = CHUNK
    j = np.arange(c)[:, None]
    i = np.arange(c)[None, :]
    masks = [((j // (2 * s) == i // (2 * s)) & (j % (2 * s) >= s) & (i % (2 * s) < s)).astype(np.float32)
             for s in LEVELS]
    masks.append(np.eye(c, dtype=np.float32))
    mf = np.stack(masks)
    blocks = [np.tril(np.ones((c, c), np.float32))]
    for s in MXU_LEVELS:
        w = np.zeros((c, c), np.float32)
        for t in range(c):
            ref = (t // (2 * s)) * 2 * s + s - 1
            if t % (2 * s) >= s:
                w[t, ref + 1:t + 1] = 1.0
            else:
                w[t, t + 1:ref + 1] = 1.0
        blocks.append(w)
    fwd = np.concatenate(blocks, axis=0)
    bwd = np.concatenate([w[::-1, ::-1] for w in blocks], axis=0)
    sums = np.stack([np.concatenate([fwd] * 3, axis=1), np.concatenate([bwd] * 3, axis=1)])
    return sums, np.stack([mf, mf[:, ::-1, ::-1]])


def _decay_logs(sums, g, d):
    c = g.shape[0]
    fwd = d == 0
    b = sums[:c]
    out = []
    for s in LEVELS:
        if s >= SUBLANES:
            parts = []
            for b0 in range(0, c, 2 * s):
                ref = b0 + s - 1 if fwd else b0 + s
                br, lo, hi = b[ref:ref + 1], b[b0:b0 + s], b[b0 + s:b0 + 2 * s]
                parts += [br - lo, hi - br] if fwd else [lo - br, br - hi]
            out.append(jnp.concatenate(parts, axis=0))
        elif s in MXU_LEVELS:
            i = 1 + MXU_LEVELS.index(s)
            out.append(sums[i * c:(i + 1) * c])
        else:
            r = lax.broadcasted_iota(jnp.int32, (c, 1), 0)
            out.append(jnp.where((r & 1) == (1 if fwd else 0), g, 0.0))
    far = b[c - 1:c] if fwd else b[0:1]
    return out + [b, far - b]


def _scan_kernel(nc, heads, has_s0, want_state, *refs):
    q_ref, kf_ref, kb_ref, lf_ref, lb_ref, v_ref = refs[:6]
    refs = refs[6:]
    if has_s0:
        s0_ref, refs = refs[0], refs[1:]
    sum_ref, m_ref, o_ref = refs[:3]
    refs = refs[3:]
    if want_state:
        st_ref, refs = refs[0], refs[1:]
    s_scr, ob_scr, qs_scr, ks_scr, v_scr, ea_scr = refs
    c = CHUNK
    dirs = ((kf_ref, lf_ref), (kb_ref, lb_ref))
    pairs = [(d, h) for d in range(2) for h in range(heads)]
    cols = lambda h: slice(h * HG_DK, (h + 1) * HG_DK)

    def rows_of(ci, d):
        return pl.ds((ci if d == 0 else nc - 1 - ci) * c, c)

    def prep_start(ci):
        out = []
        for d, (k_ref, g_ref) in enumerate(dirs):
            rows = rows_of(ci, d)
            g = g_ref[rows, :] * LOG2E
            g1 = g.astype(BF16)
            r1 = g - g1.astype(F32)
            g2 = r1.astype(BF16)
            g3 = (r1 - g2.astype(F32)).astype(BF16)
            sums = jnp.dot(sum_ref[d], jnp.concatenate([g1, g2, g3], axis=0), preferred_element_type=F32)
            out.append((g, q_ref[rows, :], k_ref[rows, :], v_ref[rows, :], sums))
        return out

    def prep_finish(slot, prepped):
        for d, (g, q, k, v, sums) in enumerate(prepped):
            e = [jnp.exp2(x) for x in _decay_logs(sums, g, d)]
            for s in range(N_LEVELS):
                qs_scr[slot, d, s] = (q * e[s]).astype(BF16)
                ks_scr[slot, d, s] = (k * e[s]).astype(BF16)
            qs_scr[slot, d, N_LEVELS] = q.astype(BF16)
            ks_scr[slot, d, N_LEVELS] = k.astype(BF16)
            qs_scr[slot, d, N_LEVELS + 1] = (q * e[N_LEVELS]).astype(BF16)
            ks_scr[slot, d, N_LEVELS + 1] = (k * e[N_LEVELS + 1]).astype(BF16)
            v_scr[slot, d] = v.astype(BF16)
            ea_scr[slot, d, 0:1, :] = e[N_LEVELS][c - 1:c] if d == 0 else e[N_LEVELS][0:1]

    def use_start(slot):
        scores, inter, new_state, vals = {}, {}, {}, {}
        for d, h in pairs:
            scores[d, h] = [_bdot_nt(qs_scr[slot, d, s, :, cols(h)], ks_scr[slot, d, s, :, cols(h)])
                            for s in range(N_LEVELS + 1)]
            st = s_scr[d, h]
            vals[d, h] = v_scr[slot, d, :, cols(h)]
            inter[d, h] = _bdot_nt(qs_scr[slot, d, N_LEVELS + 1, :, cols(h)], st)
            new_state[d, h] = (ea_scr[slot, d, 0:1, cols(h)] * st
                               + _bdot_tn(vals[d, h], ks_scr[slot, d, N_LEVELS + 1, :, cols(h)]))
        return scores, inter, new_state, vals

    def use_finish(ci, scores, inter, new_state, vals):
        for d in range(2):
            keep = [m_ref[d, s] > 0.5 for s in range(N_LEVELS + 1)]
            for h in range(heads):
                a = jnp.where(keep[N_LEVELS], scores[d, h][N_LEVELS], 0.0)
                for s in range(N_LEVELS):
                    a = jnp.where(keep[s], scores[d, h][s], a)
                o = _bdot(a, vals[d, h]) + inter[d, h]
                (o_ref if d == 0 else ob_scr)[rows_of(ci, d), cols(h)] = o
                s_scr[d, h] = new_state[d, h]

    for d, h in pairs:
        s_scr[d, h] = s0_ref[0, 0, d, h].T if has_s0 else jnp.zeros(s_scr.shape[2:], F32)

    prep_finish(0, prep_start(0))
    for ci in range(nc):
        slot = ci % 2
        prepped = prep_start(ci + 1) if ci + 1 < nc else None
        part = use_start(slot)
        if prepped is not None:
            prep_finish(1 - slot, prepped)
        use_finish(ci, *part)
    o_ref[...] += ob_scr[...]
    if want_state:
        for d in range(2):
            for h in range(heads):
                st_ref[0, d, h] = s_scr[d, h].T


def _scan(l, seq, nseq, arrs, s0, want_state):
    t, hg = arrs[0].shape
    heads = hg // HG_DK
    nc = seq // CHUNK
    sums, masks = _scan_tables()
    sums, masks = jnp.asarray(sums, F32).astype(BF16), jnp.asarray(masks)
    seq_spec = pl.BlockSpec((seq, hg), lambda b: (b, 0))
    in_specs = [seq_spec] * 6
    args = list(arrs)
    if s0 is not None:
        in_specs.append(pl.BlockSpec((1, 1) + s0.shape[2:], lambda b: (b, l, 0, 0, 0, 0)))
        args.append(s0)
    in_specs += [_const_spec(sums.shape, lambda b: (0, 0, 0)), _const_spec(masks.shape, lambda b: (0, 0, 0, 0))]
    args += [sums, masks]
    out_specs = [seq_spec]
    out_shape = [jax.ShapeDtypeStruct((t, hg), F32)]
    if want_state:
        out_specs.append(pl.BlockSpec((1, 2, heads, HG_DK, HG_DK), lambda b: (b, 0, 0, 0, 0)))
        out_shape.append(jax.ShapeDtypeStruct((nseq, 2, heads, HG_DK, HG_DK), F32))
    res = pl.pallas_call(
        functools.partial(_scan_kernel, nc, heads, s0 is not None, want_state),
        grid=(nseq,),
        in_specs=in_specs,
        out_specs=out_specs,
        out_shape=out_shape,
        scratch_shapes=[pltpu.VMEM((2, heads, HG_DK, HG_DK), F32), pltpu.VMEM((seq, hg), F32),
                        pltpu.VMEM((2, 2, N_LEVELS + 2, CHUNK, hg), BF16),
                        pltpu.VMEM((2, 2, N_LEVELS + 2, CHUNK, hg), BF16),
                        pltpu.VMEM((2, 2, CHUNK, hg), BF16), pltpu.VMEM((2, 2, SUBLANES, hg), F32)],
        compiler_params=pltpu.CompilerParams(vmem_limit_bytes=VMEM_LIMIT),
        name="hgrn2_scan",
    )(*args)
    return res if want_state else (res[0], None)


def _hyena_kernel(L, u_ref, x2_ref, kr_ref, ki_ref, d_ref, nw_ref, cs_ref, ics_ref, y_ref):
    u = u_ref[...]
    uu = jnp.dot(cs_ref[...], u.astype(BF16), preferred_element_type=F32)
    ur, us = uu[:L], uu[L:]
    kr, ki = kr_ref[0], ki_ref[0]
    zr = ur * kr + us * ki
    zi = ur * ki - us * kr
    zz = jnp.concatenate([zr.astype(BF16), zi.astype(BF16)], axis=0)
    z = jnp.dot(ics_ref[...], zz, preferred_element_type=F32)
    y = x2_ref[...] * (z + d_ref[0] * u)
    y_ref[...] = _rms(y) * nw_ref[0]


def _hyena(l, seq, nseq, u, x2, kr, ki, p):
    t, hw = u.shape
    cm, sm = _dft_tables(seq)
    cs = jnp.asarray(np.concatenate([cm, sm], axis=0), F32).astype(BF16)
    ics = jnp.asarray(np.concatenate([cm.T, -sm.T], axis=1) / seq, F32).astype(BF16)
    seq_spec = pl.BlockSpec((seq, hw), lambda b: (b, 0))
    lay = lambda shape: _const_spec((1,) + shape, lambda b: (l, 0, 0))
    return pl.pallas_call(
        functools.partial(_hyena_kernel, seq),
        grid=(nseq,),
        in_specs=[seq_spec, seq_spec, lay((seq, hw)), lay((seq, hw)), lay((1, hw)), lay((1, hw)),
                  _const_spec(cs.shape, lambda b: (0, 0)), _const_spec(ics.shape, lambda b: (0, 0))],
        out_specs=seq_spec,
        out_shape=jax.ShapeDtypeStruct((t, hw), F32),
        compiler_params=pltpu.CompilerParams(vmem_limit_bytes=VMEM_LIMIT),
        name="hyena_conv",
    )(u, x2, kr, ki, p["hy_d3"], p["hy_norm_w3"], cs, ics)


def _post_kernel(row_w, heads, ff_chunk, last, x_ref, o_ref, sg_ref, y_ref, mod_ref, hn_ref, wout_ref,
                 n2_ref, wup_ref, cw_ref, wdn_ref, fn_ref, out_ref, x_scr, h2_scr, act_scr):
    d = x_ref.shape[1]
    hg = o_ref.shape[1]
    dff = wdn_ref.shape[1]
    m = mod_ref[0]
    g1, sh2, sc2, g2 = (m[:, i * d:(i + 1) * d] for i in range(2, 6))
    hn = hn_ref[0]
    mix = _bdot(y_ref[...], wout_ref[0, hg:, :])
    for h in range(heads):
        cols = slice(h * HG_DK, (h + 1) * HG_DK)
        mix = mix + _bdot(_rms(o_ref[:, cols]) * hn[:, cols] * sg_ref[:, cols], wout_ref[0, cols, :])
    x_scr[...] = x_ref[...] + g1 * mix
    h2_scr[...] = (_rms(x_scr[...]) * n2_ref[0] * (1.0 + sc2) + sh2).astype(BF16)
    cw = cw_ref[0]

    for j in range(dff // ff_chunk):
        a, b = j * ff_chunk, (j + 1) * ff_chunk
        gate = _dwconv3(jnp.dot(h2_scr[...], wup_ref[0, :, a:b], preferred_element_type=F32),
                        cw[:, a:b], row_w)
        val = _dwconv3(jnp.dot(h2_scr[...], wup_ref[0, :, dff + a:dff + b], preferred_element_type=F32),
                       cw[:, dff + a:dff + b], row_w)
        act_scr[:, a:b] = (_silu(gate) * val).astype(BF16)
    x = x_scr[...] + g2 * jnp.dot(act_scr[...], wdn_ref[0], preferred_element_type=F32)
    out_ref[...] = _rms(x) * fn_ref[...] if last else x


def _post(l, x, o, sg, y, mod, mod_row, row_w, seq, tm, last, p):
    t, d = x.shape
    hg = o.shape[1]
    dff = p["ffn_w_down"].shape[1]
    tok = lambda w: pl.BlockSpec((tm, w), lambda i: (i, 0))
    lay = lambda shape: _const_spec((1,) + shape, lambda i: (l, 0, 0))
    return pl.pallas_call(
        functools.partial(_post_kernel, row_w, hg // HG_DK, 256, last),
        grid=(t // tm,),
        in_specs=[
            tok(d), tok(hg), tok(hg), tok(y.shape[1]),
            pl.BlockSpec((1, 1, mod.shape[2]), lambda i: (mod_row(l, i * tm // seq), 0, 0)),
            lay((1, hg)), lay((d, d)), lay((1, d)), lay((d, 2 * dff)), lay((3, 2 * dff)), lay((dff, d)),
            _const_spec((1, d), lambda i: (0, 0)),
        ],
        out_specs=tok(d),
        out_shape=jax.ShapeDtypeStruct((t, d), F32),
        scratch_shapes=[pltpu.VMEM((tm, d), F32), pltpu.VMEM((tm, d), BF16), pltpu.VMEM((tm, dff), BF16)],
        compiler_params=pltpu.CompilerParams(vmem_limit_bytes=VMEM_LIMIT),
        name="post_mixer_ffn",
    )(x, o, sg, y, mod, p["hg_norm_w3"], p["w_out_bf"], p["norm2_w3"], p["ffn_w_up_bf"],
      p["ffn_conv_w"], p["ffn_w_down_bf"], p["final_norm_w2"])


def _group(x3, mod, mod_row, row_w, s0, want_state, filt, tm, p):
    nseq, seq, d = x3.shape
    depth = p["w_in"].shape[0]
    x = x3.reshape(nseq * seq, d)
    kr, ki = filt
    states = []
    for l in range(depth):
        q, kf, kb, lf, lb, v, sg, u, x2 = _pre(l, x, mod, mod_row, row_w, seq, tm, p)
        o, st = _scan(l, seq, nseq, (q, kf, kb, lf, lb, v), s0, want_state)
        y = _hyena(l, seq, nseq, u, x2, kr, ki, p)
        x = _post(l, x, o, sg, y, mod, mod_row, row_w, seq, tm, l == depth - 1, p)
        states.append(st)
    return x.reshape(nseq, seq, d), states


def kernel(x_prompt, x_sample, state_hgrn, c, c_ctx, w_in, w_out, ada_w, ada_b, norm1_w, norm2_w,
           hg_lb_fwd, hg_lb_bwd, hg_norm_w, hy_conv_w, hy_w1, hy_b1, hy_freq1, hy_w2, hy_b2, hy_freq2,
           hy_w3, hy_d, hy_norm_w, ffn_w_up, ffn_conv_w, ffn_w_down, final_norm_w):
    depth, d, _ = w_in.shape
    vec3 = lambda a: a.reshape(a.shape[0], 1, a.shape[1])
    p = dict(w_in=w_in, w_in_bf=w_in.astype(BF16), w_out_bf=w_out.astype(BF16),
             ffn_w_up_bf=ffn_w_up.astype(BF16), ffn_w_down=ffn_w_down, ffn_w_down_bf=ffn_w_down.astype(BF16),
             norm1_w3=vec3(norm1_w), norm2_w3=vec3(norm2_w), hg_norm_w3=vec3(hg_norm_w),
             hy_d=hy_d, hy_d3=vec3(hy_d), hy_norm_w3=vec3(hy_norm_w), final_norm_w2=final_norm_w[None, :],
             hg_lb_fwd=hg_lb_fwd, hg_lb_bwd=hg_lb_bwd, hy_conv_w=hy_conv_w, ffn_conv_w=ffn_conv_w,
             hy_w1=hy_w1, hy_b1=hy_b1, hy_freq1=hy_freq1, hy_w2=hy_w2, hy_b2=hy_b2, hy_freq2=hy_freq2,
             hy_w3=hy_w3)

    n_dec = c.shape[0]
    cv = jnp.concatenate([c, c_ctx[None, :], jnp.zeros((MOD_ROWS - n_dec - 1, d), F32)], axis=0)
    mod = _modulation(cv, ada_w, ada_b).reshape(depth * MOD_ROWS, 1, 6 * d)

    seq_p, seq_s = x_prompt.shape[1], x_sample.shape[1]
    y_p, states = _group(x_prompt, mod, lambda l, b: l * MOD_ROWS + n_dec, seq_p, None, True,
                         _filters(seq_p, p), TOKEN_TILE, p)
    y_s, _ = _group(x_sample, mod, lambda l, b: l * MOD_ROWS + b, GRID_W, state_hgrn, False,
                    _filters(seq_s, p), TOKEN_TILE, p)
    return (y_p, y_s, jnp.stack(states, axis=1))
```

```python
import functools
import math

import numpy as np
import jax
import jax.numpy as jnp
from jax import lax
from jax.experimental import pallas as pl
from jax.experimental.pallas import tpu as pltpu

F32 = jnp.float32
BF16 = jnp.bfloat16

HG_DK = 128
CHUNK = 64
GRID_W = 64
HY_BANDS = 16
HY_TARGET = 1e-2
HY_FAST = 0.3
HY_SLOW = 1.5
EPS = 1e-6

LANES = 128
VMEM_LIMIT = 56 * 1024 * 1024

N_LEVELS = int(math.log2(CHUNK))
MOD_ROWS = 16
TOKEN_TILE = 512
PROJ_CHUNK = 256


def _bdot(a, b):
    return jnp.dot(a.astype(BF16), b.astype(BF16), preferred_element_type=F32)


def _bdot_nt(a, b):
    return lax.dot_general(a.astype(BF16), b.astype(BF16), (((1,), (1,)), ((), ())),
                           preferred_element_type=F32)


def _bdot_tn(a, b):
    return lax.dot_general(a.astype(BF16), b.astype(BF16), (((0,), (0,)), ((), ())),
                           preferred_element_type=F32)


def _hdot(a, b):
    return jnp.dot(a, b, preferred_element_type=F32, precision=lax.Precision.HIGHEST)


def _silu(x):
    return x * jax.nn.sigmoid(x)


def _rms(x):
    return x * lax.rsqrt(jnp.mean(x * x, axis=-1, keepdims=True) + EPS)


def _const_spec(shape, index_map):
    return pl.BlockSpec(shape, index_map, pipeline_mode=pl.Buffered(1))


def _dwconv3(p, w, row_w):
    n = p.shape[0]
    r = lax.broadcasted_iota(jnp.int32, (n, 1), 0) & (row_w - 1)
    prev = jnp.where(r == 0, 0.0, pltpu.roll(p, 1, 0))
    nxt = jnp.where(r == row_w - 1, 0.0, pltpu.roll(p, n - 1, 0))
    return w[0:1] * prev + w[1:2] * p + w[2:3] * nxt


def _mod_kernel(cv_ref, w_ref, b_ref, o_ref):
    o_ref[0] = _hdot(_silu(cv_ref[...]), w_ref[0]) + b_ref[0]


def _modulation(cv, ada_w, ada_b):
    depth, d, n = ada_w.shape
    tn = 1536
    return pl.pallas_call(
        _mod_kernel,
        grid=(depth, n // tn),
        in_specs=[
            pl.BlockSpec((MOD_ROWS, d), lambda l, j: (0, 0)),
            pl.BlockSpec((1, d, tn), lambda l, j: (l, 0, j)),
            pl.BlockSpec((1, 1, tn), lambda l, j: (l, 0, j)),
        ],
        out_specs=pl.BlockSpec((1, MOD_ROWS, tn), lambda l, j: (l, 0, j)),
        out_shape=jax.ShapeDtypeStruct((depth, MOD_ROWS, n), F32),
        compiler_params=pltpu.CompilerParams(vmem_limit_bytes=VMEM_LIMIT),
        name="modulation",
    )(cv, ada_w, ada_b.reshape(depth, 1, n))


def _dft_tables(L):
    f = np.arange(L, dtype=np.int64)[:, None]
    t = np.arange(L, dtype=np.int64)[None, :]
    ang = 2.0 * np.pi * (((2 * f + 1) * t) % (4 * L)).astype(np.float64) / (4 * L)
    return np.cos(ang), np.sin(ang)


def _filter_feats(L):
    half = LANES // 2

    def feats(pos):
        t = pos / (L - 1)
        bands = np.linspace(1e-4, HY_BANDS - 1, HY_BANDS)[None, :]
        ang = 2.0 * np.pi * pos[:, None] * bands / L
        out = np.zeros((pos.shape[0], half), np.float64)
        out[:, 0] = t
        out[:, 1:1 + HY_BANDS] = np.cos(ang)
        out[:, 1 + HY_BANDS:1 + 2 * HY_BANDS] = np.sin(ang)
        return out
    pos = np.arange(L, dtype=np.float64)
    return np.concatenate([feats(pos), feats(L - pos)], axis=1).astype(np.float32)


def _filter_taps_kernel(L, hw, ft_ref, w1_ref, b1_ref, f1_ref, w2_ref, b2_ref, f2_ref, w3_ref,
                        dl_ref, kk_ref):
    half = LANES // 2
    ft = ft_ref[...]
    h = jnp.sin(f1_ref[0] * (_hdot(ft, w1_ref[0]) + b1_ref[0]))
    h = jnp.sin(f2_ref[0] * (_hdot(h, w2_ref[0]) + b2_ref[0]))
    h = _hdot(h, w3_ref[0])
    dl = dl_ref[...]
    k1 = h[:, :hw] * jnp.exp(-ft[:, 0:1] * dl)
    row = lax.broadcasted_iota(jnp.int32, (L, 1), 0)
    k2 = jnp.where(row == 0, 0.0, -(h[:, hw:] * jnp.exp(-ft[:, half:half + 1] * dl)))
    norm = jnp.sum(jnp.abs(k1), axis=0, keepdims=True) + jnp.sum(jnp.abs(k2), axis=0, keepdims=True) + EPS
    kk_ref[0, :, :hw] = (k1 / norm).astype(BF16)
    kk_ref[0, :, hw:] = (k2 / norm).astype(BF16)


def _filter_dft_kernel(hw, kk_ref, cm_ref, sm_ref, kr_ref, ki_ref):
    kk = kk_ref[0]
    gc = jnp.dot(cm_ref[...], kk, preferred_element_type=F32)
    gs = jnp.dot(sm_ref[...], kk, preferred_element_type=F32)
    tf = gc.shape[0]
    row = lax.broadcasted_iota(jnp.int32, (tf, 1), 0)
    alt = jnp.where((row & 1) == 0, 1.0, -1.0)
    kr_ref[0] = gc[:, :hw] - alt * gs[:, hw:]
    ki_ref[0] = -(gs[:, :hw] + alt * gc[:, hw:])


def _filters(L, p):
    depth, order, hw2 = p["hy_w3"].shape
    hw = hw2 // 2
    half = LANES // 2
    cm, sm = _dft_tables(L)
    ft = jnp.asarray(_filter_feats(L))
    min_decay = math.log(HY_TARGET) / HY_SLOW
    max_decay = math.log(HY_TARGET) / HY_FAST
    deltas = jnp.asarray(np.abs(np.linspace(min_decay, max_decay, hw)).astype(np.float32)[None, :])

    def pad_to(a, rows, cols):
        return jnp.pad(a, ((0, 0), (0, rows - a.shape[1]), (0, cols - a.shape[2])))

    def block_diag(a, b):
        za = jnp.zeros(a.shape[:2] + (b.shape[2],), F32)
        zb = jnp.zeros(b.shape[:2] + (a.shape[2],), F32)
        return jnp.concatenate([jnp.concatenate([a, za], axis=2), jnp.concatenate([zb, b], axis=2)], axis=1)

    w1 = pad_to(p["hy_w1"], half, half)
    w2 = pad_to(p["hy_w2"], half, half)
    w3 = pad_to(p["hy_w3"], half, hw2)
    w1, w2, w3 = block_diag(w1, w1), block_diag(w2, w2), block_diag(w3[:, :, :hw], w3[:, :, hw:])
    vec = lambda a: jnp.tile(pad_to(a[:, None, :], 1, half), (1, 1, 2))
    lay = lambda shape: pl.BlockSpec((1,) + shape, lambda l: (l, 0, 0))
    kk = pl.pallas_call(
        functools.partial(_filter_taps_kernel, L, hw),
        grid=(depth,),
        in_specs=[
            _const_spec((L, LANES), lambda l: (0, 0)),
            lay((LANES, LANES)), lay((1, LANES)), lay((1, LANES)),
            lay((LANES, LANES)), lay((1, LANES)), lay((1, LANES)),
            lay((LANES, hw2)),
            _const_spec((1, hw), lambda l: (0, 0)),
        ],
        out_specs=lay((L, hw2)),
        out_shape=jax.ShapeDtypeStruct((depth, L, hw2), BF16),
        compiler_params=pltpu.CompilerParams(vmem_limit_bytes=VMEM_LIMIT),
        name=f"hyena_filter_taps_{L}",
    )(ft, w1, vec(p["hy_b1"]), vec(p["hy_freq1"]), w2, vec(p["hy_b2"]), vec(p["hy_freq2"]), w3, deltas)
    tf = 256
    frq = pl.BlockSpec((tf, L), lambda l, j: (j, 0))
    out = pl.BlockSpec((1, tf, hw), lambda l, j: (l, j, 0))
    return pl.pallas_call(
        functools.partial(_filter_dft_kernel, hw),
        grid=(depth, L // tf),
        in_specs=[pl.BlockSpec((1, L, hw2), lambda l, j: (l, 0, 0)), frq, frq],
        out_specs=[out, out],
        out_shape=[jax.ShapeDtypeStruct((depth, L, hw), F32)] * 2,
        compiler_params=pltpu.CompilerParams(vmem_limit_bytes=VMEM_LIMIT),
        name=f"hyena_filter_dft_{L}",
    )(kk, jnp.asarray(cm, F32).astype(BF16), jnp.asarray(sm, F32).astype(BF16))


def _lower_bound(lb_ref, l):
    prm = lb_ref[...]
    e = jnp.exp(prm - jnp.max(prm, axis=0, keepdims=True))
    s = e / jnp.sum(e, axis=0, keepdims=True)
    acc = jnp.zeros_like(s[0:1])
    for i in range(1, l + 1):
        acc = acc + s[i:i + 1]
    return acc


def _forget_gate(z, lb):
    t = jnp.exp(-jnp.abs(z))
    one_t = 1.0 + t
    log_sig = jnp.minimum(z, 0.0) - jnp.log(one_t)
    a = jnp.log(lb)
    b = jnp.log1p(-lb) + log_sig
    log_f = jnp.maximum(a, b) + jnp.log(1.0 + jnp.exp(-jnp.abs(a - b)))
    return log_f, (1.0 - lb) * (jnp.where(z >= 0.0, t, 1.0) / one_t)


def _pre_kernel(l, row_w, hg, hy, x_ref, mod_ref, n1_ref, win_ref, lbf_ref, lbb_ref, cw_ref,
                q_ref, kf_ref, kb_ref, lf_ref, lb_ref, v_ref, sg_ref, u_ref, x2_ref, h_scr):
    d = x_ref.shape[1]
    m = mod_ref[0]
    sh1, sc1 = m[:, 0:d], m[:, d:2 * d]
    h_scr[...] = (_rms(x_ref[...]) * n1_ref[0] * (1.0 + sc1) + sh1).astype(BF16)

    w = PROJ_CHUNK
    cw = cw_ref[0]
    lbs = (_lower_bound(lbf_ref, l), _lower_bound(lbb_ref, l))
    conv = lambda r, j, c: _dwconv3(r, cw[:, j * hy + c:j * hy + c + w], row_w)

    tasks = []
    for c in range(0, hg, w):
        cs = slice(c, c + w)

        def q_task(r, cs=cs):
            q_ref[:, cs] = _silu(r[0]) * (HG_DK ** -0.5)

        def gate_task(r, cs=cs, i=0):
            g_ref, k_ref = ((lf_ref, kf_ref), (lb_ref, kb_ref))[i]
            g_ref[:, cs], k_ref[:, cs] = _forget_gate(r[0], lbs[i][:, cs])

        def v_task(r, cs=cs):
            v_ref[:, cs] = r[0]

        def sg_task(r, cs=cs):
            sg_ref[:, cs] = _silu(r[0])

        def u_task(r, cs=cs, c=c):
            u_ref[:, cs] = conv(r[1], 1, c) * conv(r[0], 0, c)

        def x2_task(r, cs=cs, c=c):
            x2_ref[:, cs] = conv(r[0], 2, c)

        tasks += [((c,), q_task), ((hg + c,), gate_task), ((2 * hg + c,), functools.partial(gate_task, i=1)),
                  ((3 * hg + c,), v_task), ((4 * hg + c,), sg_task),
                  ((5 * hg + c, 5 * hg + hy + c), u_task), ((5 * hg + 2 * hy + c,), x2_task)]

    issue = lambda offs: [jnp.dot(h_scr[...], win_ref[0, :, o:o + w], preferred_element_type=F32) for o in offs]
    pending = issue(tasks[0][0])
    for i, (_, consume) in enumerate(tasks):
        ahead = issue(tasks[i + 1][0]) if i + 1 < len(tasks) else None
        consume(pending)
        pending = ahead


def _pre(l, x, mod, mod_row, row_w, seq, tm, p):
    t, d = x.shape
    hg = p["hg_lb_fwd"].shape[1]
    hy = p["hy_d"].shape[1]
    n_in = p["w_in"].shape[2]
    depth = p["w_in"].shape[0]
    tok = lambda w: pl.BlockSpec((tm, w), lambda i: (i, 0))
    out = jax.ShapeDtypeStruct((t, hg), F32)
    return pl.pallas_call(
        functools.partial(_pre_kernel, l, row_w, hg, hy),
        grid=(t // tm,),
        in_specs=[
            tok(d),
            pl.BlockSpec((1, 1, mod.shape[2]), lambda i: (mod_row(l, i * tm // seq), 0, 0)),
            _const_spec((1, 1, d), lambda i: (l, 0, 0)),
            _const_spec((1, d, n_in), lambda i: (l, 0, 0)),
            _const_spec((depth, hg), lambda i: (0, 0)),
            _const_spec((depth, hg), lambda i: (0, 0)),
            _const_spec((1, 3, 3 * hy), lambda i: (l, 0, 0)),
        ],
        out_specs=[tok(hg)] * 9,
        out_shape=[out] * 9,
        scratch_shapes=[pltpu.VMEM((tm, d), BF16)],
        compiler_params=pltpu.CompilerParams(vmem_limit_bytes=VMEM_LIMIT),
        name="pre_mixer",
    )(x, mod, p["norm1_w3"], p["w_in_bf"], p["hg_lb_fwd"], p["hg_lb_bwd"], p["hy_conv_w"])


LEVELS = tuple(CHUNK >> (i + 1) for i in range(N_LEVELS))
SUBLANES = 8
MXU_LEVELS = tuple(s for s in LEVELS if 1 < s < SUBLANES)
LOG2E = 1.4426950408889634


def _scan_tables():
    c = CHUNK
    j = np.arange(c)[:, None]
    i = np.arange(c)[None, :]
    masks = [((j // (2 * s) == i // (2 * s)) & (j % (2 * s) >= s) & (i % (2 * s) < s)).astype(np.float32)
             for s in LEVELS]
    masks.append(np.eye(c, dtype=np.float32))
    mf = np.stack(masks)
    blocks = [np.tril(np.ones((c, c), np.float32))]
    for s in MXU_LEVELS:
        w = np.zeros((c, c), np.float32)
        for t in range(c):
            ref = (t // (2 * s)) * 2 * s + s - 1
            if t % (2 * s) >= s:
                w[t, ref + 1:t + 1] = 1.0
            else:
                w[t, t + 1:ref + 1] = 1.0
        blocks.append(w)
    fwd = np.concatenate(blocks, axis=0)
    bwd = np.concatenate([w[::-1, ::-1] for w in blocks], axis=0)
    sums = np.stack([np.concatenate([fwd] * 3, axis=1), np.concatenate([bwd] * 3, axis=1)])
    return sums, np.stack([mf, mf[:, ::-1, ::-1]])


def _decay_logs(sums, g, d):
    c = g.shape[0]
    fwd = d == 0
    b = sums[:c]
    out = []
    for s in LEVELS:
        if s >= SUBLANES:
            parts = []
            for b0 in range(0, c, 2 * s):
                ref = b0 + s - 1 if fwd else b0 + s
                br, lo, hi = b[ref:ref + 1], b[b0:b0 + s], b[b0 + s:b0 + 2 * s]
                parts += [br - lo, hi - br] if fwd else [lo - br, br - hi]
            out.append(jnp.concatenate(parts, axis=0))
        elif s in MXU_LEVELS:
            i = 1 + MXU_LEVELS.index(s)
            out.append(sums[i * c:(i + 1) * c])
        else:
            r = lax.broadcasted_iota(jnp.int32, (c, 1), 0)
            out.append(jnp.where((r & 1) == (1 if fwd else 0), g, 0.0))
    far = b[c - 1:c] if fwd else b[0:1]
    return out + [b, far - b]


def _scan_kernel(nc, heads, has_s0, want_state, *refs):
    q_ref, kf_ref, kb_ref, lf_ref, lb_ref, v_ref = refs[:6]
    refs = refs[6:]
    if has_s0:
        s0_ref, refs = refs[0], refs[1:]
    sum_ref, m_ref, o_ref = refs[:3]
    refs = refs[3:]
    if want_state:
        st_ref, refs = refs[0], refs[1:]
    s_scr, ob_scr, qs_scr, ks_scr, v_scr, ea_scr = refs
    c = CHUNK
    dirs = ((kf_ref, lf_ref), (kb_ref, lb_ref))
    pairs = [(d, h) for d in range(2) for h in range(heads)]
    cols = lambda h: slice(h * HG_DK, (h + 1) * HG_DK)

    def rows_of(ci, d):
        return pl.ds((ci if d == 0 else nc - 1 - ci) * c, c)

    def prep_start(ci):
        out = []
        for d, (k_ref, g_ref) in enumerate(dirs):
            rows = rows_of(ci, d)
            g = g_ref[rows, :] * LOG2E
            g1 = g.astype(BF16)
            r1 = g - g1.astype(F32)
            g2 = r1.astype(BF16)
            g3 = (r1 - g2.astype(F32)).astype(BF16)
            sums = jnp.dot(sum_ref[d], jnp.concatenate([g1, g2, g3], axis=0), preferred_element_type=F32)
            out.append((g, q_ref[rows, :], k_ref[rows, :], v_ref[rows, :], sums))
        return out

    def prep_finish(slot, prepped):
        for d, (g, q, k, v, sums) in enumerate(prepped):
            e = [jnp.exp2(x) for x in _decay_logs(sums, g, d)]
            eb = [x.astype(BF16) for x in e]
            qb, kb = q.astype(BF16), k.astype(BF16)
            for s in range(N_LEVELS):
                qs_scr[slot, d, s] = qb * eb[s]
                ks_scr[slot, d, s] = kb * eb[s]
            qs_scr[slot, d, N_LEVELS] = qb
            ks_scr[slot, d, N_LEVELS] = kb
            qs_scr[slot, d, N_LEVELS + 1] = qb * eb[N_LEVELS]
            ks_scr[slot, d, N_LEVELS + 1] = kb * eb[N_LEVELS + 1]
            v_scr[slot, d] = v.astype(BF16)
            ea_scr[slot, d, 0:1, :] = e[N_LEVELS][c - 1:c] if d == 0 else e[N_LEVELS][0:1]

    def use_start(slot):
        scores, inter, new_state, vals = {}, {}, {}, {}
        for d, h in pairs:
            scores[d, h] = [_bdot_nt(qs_scr[slot, d, s, :, cols(h)], ks_scr[slot, d, s, :, cols(h)])
                            for s in range(N_LEVELS + 1)]
            st = s_scr[d, h]
            vals[d, h] = v_scr[slot, d, :, cols(h)]
            inter[d, h] = _bdot_nt(qs_scr[slot, d, N_LEVELS + 1, :, cols(h)], st)
            new_state[d, h] = (ea_scr[slot, d, 0:1, cols(h)] * st
                               + _bdot_tn(vals[d, h], ks_scr[slot, d, N_LEVELS + 1, :, cols(h)]))
        return scores, inter, new_state, vals

    def use_finish(ci, scores, inter, new_state, vals):
        for d in range(2):
            keep = [m_ref[d, s] > 0.5 for s in range(N_LEVELS + 1)]
            for h in range(heads):
                a = jnp.where(keep[N_LEVELS], scores[d, h][N_LEVELS], 0.0)
                for s in range(N_LEVELS):
                    a = jnp.where(keep[s], scores[d, h][s], a)
                o = _bdot(a, vals[d, h]) + inter[d, h]
                (o_ref if d == 0 else ob_scr)[rows_of(ci, d), cols(h)] = o
                s_scr[d, h] = new_state[d, h]

    for d, h in pairs:
        s_scr[d, h] = s0_ref[0, 0, d, h].T if has_s0 else jnp.zeros(s_scr.shape[2:], F32)

    prep_finish(0, prep_start(0))
    for ci in range(nc):
        slot = ci % 2
        prepped = prep_start(ci + 1) if ci + 1 < nc else None
        part = use_start(slot)
        if prepped is not None:
            prep_finish(1 - slot, prepped)
        use_finish(ci, *part)
    o_ref[...] += ob_scr[...]
    if want_state:
        for d in range(2):
            for h in range(heads):
                st_ref[0, d, h] = s_scr[d, h].T


def _scan(l, seq, nseq, arrs, s0, want_state):
    t, hg = arrs[0].shape
    heads = hg // HG_DK
    nc = seq // CHUNK
    sums, masks = _scan_tables()
    sums, masks = jnp.asarray(sums, F32).astype(BF16), jnp.asarray(masks)
    seq_spec = pl.BlockSpec((seq, hg), lambda b: (b, 0))
    in_specs = [seq_spec] * 6
    args = list(arrs)
    if s0 is not None:
        in_specs.append(pl.BlockSpec((1, 1) + s0.shape[2:], lambda b: (b, l, 0, 0, 0, 0)))
        args.append(s0)
    in_specs += [_const_spec(sums.shape, lambda b: (0, 0, 0)), _const_spec(masks.shape, lambda b: (0, 0, 0, 0))]
    args += [sums, masks]
    out_specs = [seq_spec]
    out_shape = [jax.ShapeDtypeStruct((t, hg), F32)]
    if want_state:
        out_specs.append(pl.BlockSpec((1, 2, heads, HG_DK, HG_DK), lambda b: (b, 0, 0, 0, 0)))
        out_shape.append(jax.ShapeDtypeStruct((nseq, 2, heads, HG_DK, HG_DK), F32))
    res = pl.pallas_call(
        functools.partial(_scan_kernel, nc, heads, s0 is not None, want_state),
        grid=(nseq,),
        in_specs=in_specs,
        out_specs=out_specs,
        out_shape=out_shape,
        scratch_shapes=[pltpu.VMEM((2, heads, HG_DK, HG_DK), F32), pltpu.VMEM((seq, hg), F32),
                        pltpu.VMEM((2, 2, N_LEVELS + 2, CHUNK, hg), BF16),
                        pltpu.VMEM((2, 2, N_LEVELS + 2, CHUNK, hg), BF16),
                        pltpu.VMEM((2, 2, CHUNK, hg), BF16), pltpu.VMEM((2, 2, SUBLANES, hg), F32)],
        compiler_params=pltpu.CompilerParams(vmem_limit_bytes=VMEM_LIMIT),
        name="hgrn2_scan",
    )(*args)
    return res if want_state else (res[0], None)


def _hyena_kernel(L, u_ref, x2_ref, kr_ref, ki_ref, d_ref, nw_ref, cs_ref, ics_ref, y_ref):
    u = u_ref[...]
    uu = jnp.dot(cs_ref[...], u.astype(BF16), preferred_element_type=F32)
    ur, us = uu[:L], uu[L:]
    kr, ki = kr_ref[0], ki_ref[0]
    zr = ur * kr + us * ki
    zi = ur * ki - us * kr
    zz = jnp.concatenate([zr.astype(BF16), zi.astype(BF16)], axis=0)
    z = jnp.dot(ics_ref[...], zz, preferred_element_type=F32)
    y = x2_ref[...] * (z + d_ref[0] * u)
    y_ref[...] = _rms(y) * nw_ref[0]


def _hyena(l, seq, nseq, u, x2, kr, ki, p):
    t, hw = u.shape
    cm, sm = _dft_tables(seq)
    cs = jnp.asarray(np.concatenate([cm, sm], axis=0), F32).astype(BF16)
    ics = jnp.asarray(np.concatenate([cm.T, -sm.T], axis=1) / seq, F32).astype(BF16)
    seq_spec = pl.BlockSpec((seq, hw), lambda b: (b, 0))
    lay = lambda shape: _const_spec((1,) + shape, lambda b: (l, 0, 0))
    return pl.pallas_call(
        functools.partial(_hyena_kernel, seq),
        grid=(nseq,),
        in_specs=[seq_spec, seq_spec, lay((seq, hw)), lay((seq, hw)), lay((1, hw)), lay((1, hw)),
                  _const_spec(cs.shape, lambda b: (0, 0)), _const_spec(ics.shape, lambda b: (0, 0))],
        out_specs=seq_spec,
        out_shape=jax.ShapeDtypeStruct((t, hw), F32),
        compiler_params=pltpu.CompilerParams(vmem_limit_bytes=VMEM_LIMIT),
        name="hyena_conv",
    )(u, x2, kr, ki, p["hy_d3"], p["hy_norm_w3"], cs, ics)


def _post_kernel(row_w, heads, ff_chunk, last, x_ref, o_ref, sg_ref, y_ref, mod_ref, hn_ref, wout_ref,
                 n2_ref, wup_ref, cw_ref, wdn_ref, fn_ref, out_ref, x_scr, h2_scr, act_scr):
    d = x_ref.shape[1]
    hg = o_ref.shape[1]
    dff = wdn_ref.shape[1]
    m = mod_ref[0]
    g1, sh2, sc2, g2 = (m[:, i * d:(i + 1) * d] for i in range(2, 6))
    hn = hn_ref[0]
    mix = _bdot(y_ref[...], wout_ref[0, hg:, :])
    for h in range(heads):
        cols = slice(h * HG_DK, (h + 1) * HG_DK)
        mix = mix + _bdot(_rms(o_ref[:, cols]) * hn[:, cols] * sg_ref[:, cols], wout_ref[0, cols, :])
    x_scr[...] = x_ref[...] + g1 * mix
    h2_scr[...] = (_rms(x_scr[...]) * n2_ref[0] * (1.0 + sc2) + sh2).astype(BF16)
    cw = cw_ref[0]

    for j in range(dff // ff_chunk):
        a, b = j * ff_chunk, (j + 1) * ff_chunk
        gate = _dwconv3(jnp.dot(h2_scr[...], wup_ref[0, :, a:b], preferred_element_type=F32),
                        cw[:, a:b], row_w)
        val = _dwconv3(jnp.dot(h2_scr[...], wup_ref[0, :, dff + a:dff + b], preferred_element_type=F32),
                       cw[:, dff + a:dff + b], row_w)
        act_scr[:, a:b] = (_silu(gate) * val).astype(BF16)
    x = x_scr[...] + g2 * jnp.dot(act_scr[...], wdn_ref[0], preferred_element_type=F32)
    out_ref[...] = _rms(x) * fn_ref[...] if last else x


def _post(l, x, o, sg, y, mod, mod_row, row_w, seq, tm, last, p):
    t, d = x.shape
    hg = o.shape[1]
    dff = p["ffn_w_down"].shape[1]
    tok = lambda w: pl.BlockSpec((tm, w), lambda i: (i, 0))
    lay = lambda shape: _const_spec((1,) + shape, lambda i: (l, 0, 0))
    return pl.pallas_call(
        functools.partial(_post_kernel, row_w, hg // HG_DK, 256, last),
        grid=(t // tm,),
        in_specs=[
            tok(d), tok(hg), tok(hg), tok(y.shape[1]),
            pl.BlockSpec((1, 1, mod.shape[2]), lambda i: (mod_row(l, i * tm // seq), 0, 0)),
            lay((1, hg)), lay((d, d)), lay((1, d)), lay((d, 2 * dff)), lay((3, 2 * dff)), lay((dff, d)),
            _const_spec((1, d), lambda i: (0, 0)),
        ],
        out_specs=tok(d),
        out_shape=jax.ShapeDtypeStruct((t, d), F32),
        scratch_shapes=[pltpu.VMEM((tm, d), F32), pltpu.VMEM((tm, d), BF16), pltpu.VMEM((tm, dff), BF16)],
        compiler_params=pltpu.CompilerParams(vmem_limit_bytes=VMEM_LIMIT),
        name="post_mixer_ffn",
    )(x, o, sg, y, mod, p["hg_norm_w3"], p["w_out_bf"], p["norm2_w3"], p["ffn_w_up_bf"],
      p["ffn_conv_w"], p["ffn_w_down_bf"], p["final_norm_w2"])


def _group(x3, mod, mod_row, row_w, s0, want_state, filt, tm, p):
    nseq, seq, d = x3.shape
    depth = p["w_in"].shape[0]
    x = x3.reshape(nseq * seq, d)
    kr, ki = filt
    states = []
    for l in range(depth):
        q, kf, kb, lf, lb, v, sg, u, x2 = _pre(l, x, mod, mod_row, row_w, seq, tm, p)
        o, st = _scan(l, seq, nseq, (q, kf, kb, lf, lb, v), s0, want_state)
        y = _hyena(l, seq, nseq, u, x2, kr, ki, p)
        x = _post(l, x, o, sg, y, mod, mod_row, row_w, seq, tm, l == depth - 1, p)
        states.append(st)
    return x.reshape(nseq, seq, d), states


def kernel(x_prompt, x_sample, state_hgrn, c, c_ctx, w_in, w_out, ada_w, ada_b, norm1_w, norm2_w,
           hg_lb_fwd, hg_lb_bwd, hg_norm_w, hy_conv_w, hy_w1, hy_b1, hy_freq1, hy_w2, hy_b2, hy_freq2,
           hy_w3, hy_d, hy_norm_w, ffn_w_up, ffn_conv_w, ffn_w_down, final_norm_w):
    depth, d, _ = w_in.shape
    vec3 = lambda a: a.reshape(a.shape[0], 1, a.shape[1])
    p = dict(w_in=w_in, w_in_bf=w_in.astype(BF16), w_out_bf=w_out.astype(BF16),
             ffn_w_up_bf=ffn_w_up.astype(BF16), ffn_w_down=ffn_w_down, ffn_w_down_bf=ffn_w_down.astype(BF16),
             norm1_w3=vec3(norm1_w), norm2_w3=vec3(norm2_w), hg_norm_w3=vec3(hg_norm_w),
             hy_d=hy_d, hy_d3=vec3(hy_d), hy_norm_w3=vec3(hy_norm_w), final_norm_w2=final_norm_w[None, :],
             hg_lb_fwd=hg_lb_fwd, hg_lb_bwd=hg_lb_bwd, hy_conv_w=hy_conv_w, ffn_conv_w=ffn_conv_w,
             hy_w1=hy_w1, hy_b1=hy_b1, hy_freq1=hy_freq1, hy_w2=hy_w2, hy_b2=hy_b2, hy_freq2=hy_freq2,
             hy_w3=hy_w3)

    n_dec = c.shape[0]
    cv = jnp.concatenate([c, c_ctx[None, :], jnp.zeros((MOD_ROWS - n_dec - 1, d), F32)], axis=0)
    mod = _modulation(cv, ada_w, ada_b).reshape(depth * MOD_ROWS, 1, 6 * d)

    seq_p, seq_s = x_prompt.shape[1], x_sample.shape[1]
    y_p, states = _group(x_prompt, mod, lambda l, b: l * MOD_ROWS + n_dec, seq_p, None, True,
                         _filters(seq_p, p), TOKEN_TILE, p)
    y_s, _ = _group(x_sample, mod, lambda l, b: l * MOD_ROWS + b, GRID_W, state_hgrn, False,
                    _filters(seq_s, p), TOKEN_TILE, p)
    return (y_p, y_s, jnp.stack(states, axis=1))
```

```python
import functools
import math

import numpy as np
import jax
import jax.numpy as jnp
from jax import lax
from jax.experimental import pallas as pl
from jax.experimental.pallas import tpu as pltpu

F32 = jnp.float32
BF16 = jnp.bfloat16

HG_DK = 128
CHUNK = 64
GRID_W = 64
HY_BANDS = 16
HY_TARGET = 1e-2
HY_FAST = 0.3
HY_SLOW = 1.5
EPS = 1e-6

LANES = 128
VMEM_LIMIT = 56 * 1024 * 1024

N_LEVELS = int(math.log2(CHUNK))
MOD_ROWS = 16
TOKEN_TILE = 512
PROJ_CHUNK = 256


def _bdot(a, b):
    return jnp.dot(a.astype(BF16), b.astype(BF16), preferred_element_type=F32)


def _bdot_nt(a, b):
    return lax.dot_general(a.astype(BF16), b.astype(BF16), (((1,), (1,)), ((), ())),
                           preferred_element_type=F32)


def _bdot_tn(a, b):
    return lax.dot_general(a.astype(BF16), b.astype(BF16), (((0,), (0,)), ((), ())),
                           preferred_element_type=F32)


def _hdot(a, b):
    return jnp.dot(a, b, preferred_element_type=F32, precision=lax.Precision.HIGHEST)


def _silu(x):
    return x * jax.nn.sigmoid(x)


def _rms(x):
    return x * lax.rsqrt(jnp.mean(x * x, axis=-1, keepdims=True) + EPS)


def _const_spec(shape, index_map):
    return pl.BlockSpec(shape, index_map, pipeline_mode=pl.Buffered(1))


def _dwconv3(p, w, row_w):
    n = p.shape[0]
    r = lax.broadcasted_iota(jnp.int32, (n, 1), 0) & (row_w - 1)
    prev = jnp.where(r == 0, 0.0, pltpu.roll(p, 1, 0))
    nxt = jnp.where(r == row_w - 1, 0.0, pltpu.roll(p, n - 1, 0))
    return w[0:1] * prev + w[1:2] * p + w[2:3] * nxt


def _mod_kernel(cv_ref, w_ref, b_ref, o_ref):
    s = _silu(cv_ref[...])
    w = w_ref[0]
    s_hi = s.astype(BF16)
    s_lo = (s - s_hi.astype(F32)).astype(BF16)
    w_hi = w.astype(BF16)
    w_lo = (w - w_hi.astype(F32)).astype(BF16)
    rows = s.shape[0]
    hi = jnp.dot(jnp.concatenate([s_hi, s_lo], axis=0), w_hi, preferred_element_type=F32)
    o_ref[0] = hi[:rows] + hi[rows:] + jnp.dot(s_hi, w_lo, preferred_element_type=F32) + b_ref[0]


def _modulation(cv, ada_w, ada_b):
    depth, d, n = ada_w.shape
    tn = 1536
    return pl.pallas_call(
        _mod_kernel,
        grid=(depth, n // tn),
        in_specs=[
            pl.BlockSpec((MOD_ROWS, d), lambda l, j: (0, 0)),
            pl.BlockSpec((1, d, tn), lambda l, j: (l, 0, j)),
            pl.BlockSpec((1, 1, tn), lambda l, j: (l, 0, j)),
        ],
        out_specs=pl.BlockSpec((1, MOD_ROWS, tn), lambda l, j: (l, 0, j)),
        out_shape=jax.ShapeDtypeStruct((depth, MOD_ROWS, n), F32),
        compiler_params=pltpu.CompilerParams(vmem_limit_bytes=VMEM_LIMIT),
        name="modulation",
    )(cv, ada_w, ada_b.reshape(depth, 1, n))


def _dft_tables(L):
    f = np.arange(L, dtype=np.int64)[:, None]
    t = np.arange(L, dtype=np.int64)[None, :]
    ang = 2.0 * np.pi * (((2 * f + 1) * t) % (4 * L)).astype(np.float64) / (4 * L)
    return np.cos(ang), np.sin(ang)


def _filter_feats(L):
    half = LANES // 2

    def feats(pos):
        t = pos / (L - 1)
        bands = np.linspace(1e-4, HY_BANDS - 1, HY_BANDS)[None, :]
        ang = 2.0 * np.pi * pos[:, None] * bands / L
        out = np.zeros((pos.shape[0], half), np.float64)
        out[:, 0] = t
        out[:, 1:1 + HY_BANDS] = np.cos(ang)
        out[:, 1 + HY_BANDS:1 + 2 * HY_BANDS] = np.sin(ang)
        return out
    pos = np.arange(L, dtype=np.float64)
    return np.concatenate([feats(pos), feats(L - pos)], axis=1).astype(np.float32)


def _filter_taps_kernel(L, hw, ft_ref, w1_ref, b1_ref, f1_ref, w2_ref, b2_ref, f2_ref, w3_ref,
                        dl_ref, kk_ref):
    half = LANES // 2
    ft = ft_ref[...]
    h = jnp.sin(f1_ref[0] * (_hdot(ft, w1_ref[0]) + b1_ref[0]))
    h = jnp.sin(f2_ref[0] * (_hdot(h, w2_ref[0]) + b2_ref[0]))
    h = _hdot(h, w3_ref[0])
    dl = dl_ref[...]
    k1 = h[:, :hw] * jnp.exp(-ft[:, 0:1] * dl)
    row = lax.broadcasted_iota(jnp.int32, (L, 1), 0)
    k2 = jnp.where(row == 0, 0.0, -(h[:, hw:] * jnp.exp(-ft[:, half:half + 1] * dl)))
    norm = jnp.sum(jnp.abs(k1), axis=0, keepdims=True) + jnp.sum(jnp.abs(k2), axis=0, keepdims=True) + EPS
    kk_ref[0, :, :hw] = (k1 / norm).astype(BF16)
    kk_ref[0, :, hw:] = (k2 / norm).astype(BF16)


def _filter_dft_kernel(hw, kk_ref, cm_ref, sm_ref, kr_ref, ki_ref):
    kk = kk_ref[0]
    gc = jnp.dot(cm_ref[...], kk, preferred_element_type=F32)
    gs = jnp.dot(sm_ref[...], kk, preferred_element_type=F32)
    tf = gc.shape[0]
    row = lax.broadcasted_iota(jnp.int32, (tf, 1), 0)
    alt = jnp.where((row & 1) == 0, 1.0, -1.0)
    kr_ref[0] = gc[:, :hw] - alt * gs[:, hw:]
    ki_ref[0] = -(gs[:, :hw] + alt * gc[:, hw:])


def _filters(L, p):
    depth, order, hw2 = p["hy_w3"].shape
    hw = hw2 // 2
    half = LANES // 2
    cm, sm = _dft_tables(L)
    ft = jnp.asarray(_filter_feats(L))
    min_decay = math.log(HY_TARGET) / HY_SLOW
    max_decay = math.log(HY_TARGET) / HY_FAST
    deltas = jnp.asarray(np.abs(np.linspace(min_decay, max_decay, hw)).astype(np.float32)[None, :])

    def pad_to(a, rows, cols):
        return jnp.pad(a, ((0, 0), (0, rows - a.shape[1]), (0, cols - a.shape[2])))

    def block_diag(a, b):
        za = jnp.zeros(a.shape[:2] + (b.shape[2],), F32)
        zb = jnp.zeros(b.shape[:2] + (a.shape[2],), F32)
        return jnp.concatenate([jnp.concatenate([a, za], axis=2), jnp.concatenate([zb, b], axis=2)], axis=1)

    w1 = pad_to(p["hy_w1"], half, half)
    w2 = pad_to(p["hy_w2"], half, half)
    w3 = pad_to(p["hy_w3"], half, hw2)
    w1, w2, w3 = block_diag(w1, w1), block_diag(w2, w2), block_diag(w3[:, :, :hw], w3[:, :, hw:])
    vec = lambda a: jnp.tile(pad_to(a[:, None, :], 1, half), (1, 1, 2))
    lay = lambda shape: pl.BlockSpec((1,) + shape, lambda l: (l, 0, 0))
    kk = pl.pallas_call(
        functools.partial(_filter_taps_kernel, L, hw),
        grid=(depth,),
        in_specs=[
            _const_spec((L, LANES), lambda l: (0, 0)),
            lay((LANES, LANES)), lay((1, LANES)), lay((1, LANES)),
            lay((LANES, LANES)), lay((1, LANES)), lay((1, LANES)),
            lay((LANES, hw2)),
            _const_spec((1, hw), lambda l: (0, 0)),
        ],
        out_specs=lay((L, hw2)),
        out_shape=jax.ShapeDtypeStruct((depth, L, hw2), BF16),
        compiler_params=pltpu.CompilerParams(vmem_limit_bytes=VMEM_LIMIT),
        name=f"hyena_filter_taps_{L}",
    )(ft, w1, vec(p["hy_b1"]), vec(p["hy_freq1"]), w2, vec(p["hy_b2"]), vec(p["hy_freq2"]), w3, deltas)
    tf = 256
    frq = pl.BlockSpec((tf, L), lambda l, j: (j, 0))
    out = pl.BlockSpec((1, tf, hw), lambda l, j: (l, j, 0))
    return pl.pallas_call(
        functools.partial(_filter_dft_kernel, hw),
        grid=(depth, L // tf),
        in_specs=[pl.BlockSpec((1, L, hw2), lambda l, j: (l, 0, 0)), frq, frq],
        out_specs=[out, out],
        out_shape=[jax.ShapeDtypeStruct((depth, L, hw), F32)] * 2,
        compiler_params=pltpu.CompilerParams(vmem_limit_bytes=VMEM_LIMIT),
        name=f"hyena_filter_dft_{L}",
    )(kk, jnp.asarray(cm, F32).astype(BF16), jnp.asarray(sm, F32).astype(BF16))


def _lower_bound(lb_ref, l):
    prm = lb_ref[...]
    e = jnp.exp(prm - jnp.max(prm, axis=0, keepdims=True))
    s = e / jnp.sum(e, axis=0, keepdims=True)
    acc = jnp.zeros_like(s[0:1])
    for i in range(1, l + 1):
        acc = acc + s[i:i + 1]
    return acc


def _forget_gate(z, lb):
    t = jnp.exp(-jnp.abs(z))
    one_t = 1.0 + t
    log_sig = jnp.minimum(z, 0.0) - jnp.log(one_t)
    a = jnp.log(lb)
    b = jnp.log1p(-lb) + log_sig
    log_f = jnp.maximum(a, b) + jnp.log(1.0 + jnp.exp(-jnp.abs(a - b)))
    return log_f, (1.0 - lb) * (jnp.where(z >= 0.0, t, 1.0) / one_t)


def _pre_kernel(l, row_w, hg, hy, x_ref, mod_ref, n1_ref, win_ref, lbf_ref, lbb_ref, cw_ref,
                q_ref, kf_ref, kb_ref, lf_ref, lb_ref, v_ref, sg_ref, u_ref, x2_ref, h_scr):
    d = x_ref.shape[1]
    m = mod_ref[0]
    sh1, sc1 = m[:, 0:d], m[:, d:2 * d]
    h_scr[...] = (_rms(x_ref[...]) * n1_ref[0] * (1.0 + sc1) + sh1).astype(BF16)

    w = PROJ_CHUNK
    cw = cw_ref[0]
    lbs = (_lower_bound(lbf_ref, l), _lower_bound(lbb_ref, l))
    conv = lambda r, j, c: _dwconv3(r, cw[:, j * hy + c:j * hy + c + w], row_w)

    tasks = []
    for c in range(0, hg, w):
        cs = slice(c, c + w)

        def q_task(r, cs=cs):
            q_ref[:, cs] = (_silu(r[0]) * (HG_DK ** -0.5)).astype(q_ref.dtype)

        def gate_task(r, cs=cs, i=0):
            g_ref, k_ref = ((lf_ref, kf_ref), (lb_ref, kb_ref))[i]
            log_f, k = _forget_gate(r[0], lbs[i][:, cs])
            g_ref[:, cs], k_ref[:, cs] = log_f, k.astype(k_ref.dtype)

        def v_task(r, cs=cs):
            v_ref[:, cs] = r[0].astype(v_ref.dtype)

        def sg_task(r, cs=cs):
            sg_ref[:, cs] = _silu(r[0]).astype(sg_ref.dtype)

        def u_task(r, cs=cs, c=c):
            u_ref[:, cs] = (conv(r[1], 1, c) * conv(r[0], 0, c)).astype(u_ref.dtype)

        def x2_task(r, cs=cs, c=c):
            x2_ref[:, cs] = conv(r[0], 2, c).astype(x2_ref.dtype)

        tasks += [((c,), q_task), ((hg + c,), gate_task), ((2 * hg + c,), functools.partial(gate_task, i=1)),
                  ((3 * hg + c,), v_task), ((4 * hg + c,), sg_task),
                  ((5 * hg + c, 5 * hg + hy + c), u_task), ((5 * hg + 2 * hy + c,), x2_task)]

    issue = lambda offs: [jnp.dot(h_scr[...], win_ref[0, :, o:o + w], preferred_element_type=F32) for o in offs]
    pending = issue(tasks[0][0])
    for i, (_, consume) in enumerate(tasks):
        ahead = issue(tasks[i + 1][0]) if i + 1 < len(tasks) else None
        consume(pending)
        pending = ahead


def _pre(l, x, mod, mod_row, row_w, seq, tm, p):
    t, d = x.shape
    hg = p["hg_lb_fwd"].shape[1]
    hy = p["hy_d"].shape[1]
    n_in = p["w_in"].shape[2]
    depth = p["w_in"].shape[0]
    tok = lambda w: pl.BlockSpec((tm, w), lambda i: (i, 0))
    out = lambda dt: jax.ShapeDtypeStruct((t, hg), dt)
    return pl.pallas_call(
        functools.partial(_pre_kernel, l, row_w, hg, hy),
        grid=(t // tm,),
        in_specs=[
            tok(d),
            pl.BlockSpec((1, 1, mod.shape[2]), lambda i: (mod_row(l, i * tm // seq), 0, 0)),
            _const_spec((1, 1, d), lambda i: (l, 0, 0)),
            _const_spec((1, d, n_in), lambda i: (l, 0, 0)),
            _const_spec((depth, hg), lambda i: (0, 0)),
            _const_spec((depth, hg), lambda i: (0, 0)),
            _const_spec((1, 3, 3 * hy), lambda i: (l, 0, 0)),
        ],
        out_specs=[tok(hg)] * 9,
        out_shape=[out(BF16)] * 3 + [out(F32)] * 2 + [out(BF16)] * 4,
        scratch_shapes=[pltpu.VMEM((tm, d), BF16)],
        compiler_params=pltpu.CompilerParams(vmem_limit_bytes=VMEM_LIMIT),
        name="pre_mixer",
    )(x, mod, p["norm1_w3"], p["w_in_bf"], p["hg_lb_fwd"], p["hg_lb_bwd"], p["hy_conv_w"])


LEVELS = tuple(CHUNK >> (i + 1) for i in range(N_LEVELS))
SUBLANES = 8
MXU_LEVELS = tuple(s for s in LEVELS if 1 < s < SUBLANES)
LOG2E = 1.4426950408889634


def _scan_tables():
    c = CHUNK
    j = np.arange(c)[:, None]
    i = np.arange(c)[None, :]
    masks = [((j // (2 * s) == i // (2 * s)) & (j % (2 * s) >= s) & (i % (2 * s) < s)).astype(np.float32)
             for s in LEVELS]
    masks.append(np.eye(c, dtype=np.float32))
    mf = np.stack(masks)
    blocks = [np.tril(np.ones((c, c), np.float32))]
    for s in MXU_LEVELS:
        w = np.zeros((c, c), np.float32)
        for t in range(c):
            ref = (t // (2 * s)) * 2 * s + s - 1
            if t % (2 * s) >= s:
                w[t, ref + 1:t + 1] = 1.0
            else:
                w[t, t + 1:ref + 1] = 1.0
        blocks.append(w)
    fwd = np.concatenate(blocks, axis=0)
    bwd = np.concatenate([w[::-1, ::-1] for w in blocks], axis=0)
    sums = np.stack([np.concatenate([fwd] * 3, axis=1), np.concatenate([bwd] * 3, axis=1)])
    return sums, np.stack([mf, mf[:, ::-1, ::-1]])


def _decay_logs(sums, g, d):
    c = g.shape[0]
    fwd = d == 0
    b = sums[:c]
    out = []
    for s in LEVELS:
        if s >= SUBLANES:
            parts = []
            for b0 in range(0, c, 2 * s):
                ref = b0 + s - 1 if fwd else b0 + s
                br, lo, hi = b[ref:ref + 1], b[b0:b0 + s], b[b0 + s:b0 + 2 * s]
                parts += [br - lo, hi - br] if fwd else [lo - br, br - hi]
            out.append(jnp.concatenate(parts, axis=0))
        elif s in MXU_LEVELS:
            i = 1 + MXU_LEVELS.index(s)
            out.append(sums[i * c:(i + 1) * c])
        else:
            r = lax.broadcasted_iota(jnp.int32, (c, 1), 0)
            out.append(jnp.where((r & 1) == (1 if fwd else 0), g, 0.0))
    far = b[c - 1:c] if fwd else b[0:1]
    return out + [b, far - b]


def _scan_kernel(nc, heads, has_s0, want_state, *refs):
    q_ref, kf_ref, kb_ref, lf_ref, lb_ref, v_ref = refs[:6]
    refs = refs[6:]
    if has_s0:
        s0_ref, refs = refs[0], refs[1:]
    sum_ref, m_ref, o_ref = refs[:3]
    refs = refs[3:]
    if want_state:
        st_ref, refs = refs[0], refs[1:]
    s_scr, ob_scr, qs_scr, ks_scr, v_scr, ea_scr = refs
    c = CHUNK
    dirs = ((kf_ref, lf_ref), (kb_ref, lb_ref))
    pairs = [(d, h) for d in range(2) for h in range(heads)]
    cols = lambda h: slice(h * HG_DK, (h + 1) * HG_DK)

    def rows_of(ci, d):
        return pl.ds((ci if d == 0 else nc - 1 - ci) * c, c)

    def prep_start(ci):
        out = []
        for d, (k_ref, g_ref) in enumerate(dirs):
            rows = rows_of(ci, d)
            g = g_ref[rows, :] * LOG2E
            g1 = g.astype(BF16)
            r1 = g - g1.astype(F32)
            g2 = r1.astype(BF16)
            g3 = (r1 - g2.astype(F32)).astype(BF16)
            sums = jnp.dot(sum_ref[d], jnp.concatenate([g1, g2, g3], axis=0), preferred_element_type=F32)
            out.append((g, q_ref[rows, :], k_ref[rows, :], v_ref[rows, :], sums))
        return out

    def prep_finish(slot, prepped):
        for d, (g, q, k, v, sums) in enumerate(prepped):
            e = [jnp.exp2(x) for x in _decay_logs(sums, g, d)]
            eb = [x.astype(BF16) for x in e]
            qb, kb = q.astype(BF16), k.astype(BF16)
            for s in range(N_LEVELS):
                qs_scr[slot, d, s] = qb * eb[s]
                ks_scr[slot, d, s] = kb * eb[s]
            qs_scr[slot, d, N_LEVELS] = qb
            ks_scr[slot, d, N_LEVELS] = kb
            qs_scr[slot, d, N_LEVELS + 1] = qb * eb[N_LEVELS]
            ks_scr[slot, d, N_LEVELS + 1] = kb * eb[N_LEVELS + 1]
            v_scr[slot, d] = v.astype(BF16)
            ea_scr[slot, d, 0:1, :] = e[N_LEVELS][c - 1:c] if d == 0 else e[N_LEVELS][0:1]

    def use_start(slot):
        scores, inter, new_state, vals = {}, {}, {}, {}
        for d, h in pairs:
            scores[d, h] = [_bdot_nt(qs_scr[slot, d, s, :, cols(h)], ks_scr[slot, d, s, :, cols(h)])
                            for s in range(N_LEVELS + 1)]
            st = s_scr[d, h]
            vals[d, h] = v_scr[slot, d, :, cols(h)]
            inter[d, h] = _bdot_nt(qs_scr[slot, d, N_LEVELS + 1, :, cols(h)], st)
            new_state[d, h] = (ea_scr[slot, d, 0:1, cols(h)] * st
                               + _bdot_tn(vals[d, h], ks_scr[slot, d, N_LEVELS + 1, :, cols(h)]))
        return scores, inter, new_state, vals

    def use_finish(ci, scores, inter, new_state, vals):
        for d in range(2):
            keep = [m_ref[d, s] > 0.5 for s in range(N_LEVELS + 1)]
            for h in range(heads):
                a = jnp.where(keep[N_LEVELS], scores[d, h][N_LEVELS], 0.0)
                for s in range(N_LEVELS):
                    a = jnp.where(keep[s], scores[d, h][s], a)
                o = _bdot(a, vals[d, h]) + inter[d, h]
                (o_ref if d == 0 else ob_scr)[rows_of(ci, d), cols(h)] = o
                s_scr[d, h] = new_state[d, h]

    for d, h in pairs:
        s_scr[d, h] = s0_ref[0, 0, d, h].T if has_s0 else jnp.zeros(s_scr.shape[2:], F32)

    prep_finish(0, prep_start(0))
    for ci in range(nc):
        slot = ci % 2
        prepped = prep_start(ci + 1) if ci + 1 < nc else None
        part = use_start(slot)
        if prepped is not None:
            prep_finish(1 - slot, prepped)
        use_finish(ci, *part)
    o_ref[...] += ob_scr[...]
    if want_state:
        for d in range(2):
            for h in range(heads):
                st_ref[0, d, h] = s_scr[d, h].T


def _scan(l, seq, nseq, arrs, s0, want_state):
    t, hg = arrs[0].shape
    heads = hg // HG_DK
    nc = seq // CHUNK
    sums, masks = _scan_tables()
    sums, masks = jnp.asarray(sums, F32).astype(BF16), jnp.asarray(masks)
    seq_spec = pl.BlockSpec((seq, hg), lambda b: (b, 0))
    in_specs = [seq_spec] * 6
    args = list(arrs)
    if s0 is not None:
        in_specs.append(pl.BlockSpec((1, 1) + s0.shape[2:], lambda b: (b, l, 0, 0, 0, 0)))
        args.append(s0)
    in_specs += [_const_spec(sums.shape, lambda b: (0, 0, 0)), _const_spec(masks.shape, lambda b: (0, 0, 0, 0))]
    args += [sums, masks]
    out_specs = [seq_spec]
    out_shape = [jax.ShapeDtypeStruct((t, hg), F32)]
    if want_state:
        out_specs.append(pl.BlockSpec((1, 2, heads, HG_DK, HG_DK), lambda b: (b, 0, 0, 0, 0)))
        out_shape.append(jax.ShapeDtypeStruct((nseq, 2, heads, HG_DK, HG_DK), F32))
    res = pl.pallas_call(
        functools.partial(_scan_kernel, nc, heads, s0 is not None, want_state),
        grid=(nseq,),
        in_specs=in_specs,
        out_specs=out_specs,
        out_shape=out_shape,
        scratch_shapes=[pltpu.VMEM((2, heads, HG_DK, HG_DK), F32), pltpu.VMEM((seq, hg), F32),
                        pltpu.VMEM((2, 2, N_LEVELS + 2, CHUNK, hg), BF16),
                        pltpu.VMEM((2, 2, N_LEVELS + 2, CHUNK, hg), BF16),
                        pltpu.VMEM((2, 2, CHUNK, hg), BF16), pltpu.VMEM((2, 2, SUBLANES, hg), F32)],
        compiler_params=pltpu.CompilerParams(vmem_limit_bytes=VMEM_LIMIT),
        name="hgrn2_scan",
    )(*args)
    return res if want_state else (res[0], None)


def _hyena_kernel(L, u_ref, x2_ref, kr_ref, ki_ref, d_ref, nw_ref, cs_ref, ics_ref, y_ref):
    u = u_ref[...]
    uu = jnp.dot(cs_ref[...], u.astype(BF16), preferred_element_type=F32)
    ur, us = uu[:L], uu[L:]
    kr, ki = kr_ref[0], ki_ref[0]
    zr = ur * kr + us * ki
    zi = ur * ki - us * kr
    zz = jnp.concatenate([zr.astype(BF16), zi.astype(BF16)], axis=0)
    z = jnp.dot(ics_ref[...], zz, preferred_element_type=F32)
    y = x2_ref[...] * (z + d_ref[0] * u)
    y_ref[...] = (_rms(y) * nw_ref[0]).astype(y_ref.dtype)


def _hyena(l, seq, nseq, u, x2, kr, ki, p):
    t, hw = u.shape
    cm, sm = _dft_tables(seq)
    cs = jnp.asarray(np.concatenate([cm, sm], axis=0), F32).astype(BF16)
    ics = jnp.asarray(np.concatenate([cm.T, -sm.T], axis=1) / seq, F32).astype(BF16)
    seq_spec = pl.BlockSpec((seq, hw), lambda b: (b, 0))
    lay = lambda shape: _const_spec((1,) + shape, lambda b: (l, 0, 0))
    return pl.pallas_call(
        functools.partial(_hyena_kernel, seq),
        grid=(nseq,),
        in_specs=[seq_spec, seq_spec, lay((seq, hw)), lay((seq, hw)), lay((1, hw)), lay((1, hw)),
                  _const_spec(cs.shape, lambda b: (0, 0)), _const_spec(ics.shape, lambda b: (0, 0))],
        out_specs=seq_spec,
        out_shape=jax.ShapeDtypeStruct((t, hw), BF16),
        compiler_params=pltpu.CompilerParams(vmem_limit_bytes=VMEM_LIMIT),
        name="hyena_conv",
    )(u, x2, kr, ki, p["hy_d3"], p["hy_norm_w3"], cs, ics)


def _post_kernel(row_w, heads, ff_chunk, last, x_ref, o_ref, sg_ref, y_ref, mod_ref, hn_ref, wout_ref,
                 n2_ref, wup_ref, cw_ref, wdn_ref, fn_ref, out_ref, x_scr, h2_scr, act_scr):
    d = x_ref.shape[1]
    hg = o_ref.shape[1]
    dff = wdn_ref.shape[1]
    m = mod_ref[0]
    g1, sh2, sc2, g2 = (m[:, i * d:(i + 1) * d] for i in range(2, 6))
    hn = hn_ref[0]
    mix = _bdot(y_ref[...], wout_ref[0, hg:, :])
    for h in range(heads):
        cols = slice(h * HG_DK, (h + 1) * HG_DK)
        mix = mix + _bdot(_rms(o_ref[:, cols]) * hn[:, cols] * sg_ref[:, cols], wout_ref[0, cols, :])
    x_scr[...] = x_ref[...] + g1 * mix
    h2_scr[...] = (_rms(x_scr[...]) * n2_ref[0] * (1.0 + sc2) + sh2).astype(BF16)
    cw = cw_ref[0]

    for j in range(dff // ff_chunk):
        a, b = j * ff_chunk, (j + 1) * ff_chunk
        gate = _dwconv3(jnp.dot(h2_scr[...], wup_ref[0, :, a:b], preferred_element_type=F32),
                        cw[:, a:b], row_w)
        val = _dwconv3(jnp.dot(h2_scr[...], wup_ref[0, :, dff + a:dff + b], preferred_element_type=F32),
                       cw[:, dff + a:dff + b], row_w)
        act_scr[:, a:b] = (_silu(gate) * val).astype(BF16)
    x = x_scr[...] + g2 * jnp.dot(act_scr[...], wdn_ref[0], preferred_element_type=F32)
    out_ref[...] = _rms(x) * fn_ref[...] if last else x


def _post(l, x, o, sg, y, mod, mod_row, row_w, seq, tm, last, p):
    t, d = x.shape
    hg = o.shape[1]
    dff = p["ffn_w_down"].shape[1]
    tok = lambda w: pl.BlockSpec((tm, w), lambda i: (i, 0))
    lay = lambda shape: _const_spec((1,) + shape, lambda i: (l, 0, 0))
    return pl.pallas_call(
        functools.partial(_post_kernel, row_w, hg // HG_DK, 256, last),
        grid=(t // tm,),
        in_specs=[
            tok(d), tok(hg), tok(hg), tok(y.shape[1]),
            pl.BlockSpec((1, 1, mod.shape[2]), lambda i: (mod_row(l, i * tm // seq), 0, 0)),
            lay((1, hg)), lay((d, d)), lay((1, d)), lay((d, 2 * dff)), lay((3, 2 * dff)), lay((dff, d)),
            _const_spec((1, d), lambda i: (0, 0)),
        ],
        out_specs=tok(d),
        out_shape=jax.ShapeDtypeStruct((t, d), F32),
        scratch_shapes=[pltpu.VMEM((tm, d), F32), pltpu.VMEM((tm, d), BF16), pltpu.VMEM((tm, dff), BF16)],
        compiler_params=pltpu.CompilerParams(vmem_limit_bytes=VMEM_LIMIT),
        name="post_mixer_ffn",
    )(x, o, sg, y, mod, p["hg_norm_w3"], p["w_out_bf"], p["norm2_w3"], p["ffn_w_up_bf"],
      p["ffn_conv_w"], p["ffn_w_down_bf"], p["final_norm_w2"])


def _group(x3, mod, mod_row, row_w, s0, want_state, filt, tm, p):
    nseq, seq, d = x3.shape
    depth = p["w_in"].shape[0]
    x = x3.reshape(nseq * seq, d)
    kr, ki = filt
    states = []
    for l in range(depth):
        q, kf, kb, lf, lb, v, sg, u, x2 = _pre(l, x, mod, mod_row, row_w, seq, tm, p)
        o, st = _scan(l, seq, nseq, (q, kf, kb, lf, lb, v), s0, want_state)
        y = _hyena(l, seq, nseq, u, x2, kr, ki, p)
        x = _post(l, x, o, sg, y, mod, mod_row, row_w, seq, tm, l == depth - 1, p)
        states.append(st)
    return x.reshape(nseq, seq, d), states


def kernel(x_prompt, x_sample, state_hgrn, c, c_ctx, w_in, w_out, ada_w, ada_b, norm1_w, norm2_w,
           hg_lb_fwd, hg_lb_bwd, hg_norm_w, hy_conv_w, hy_w1, hy_b1, hy_freq1, hy_w2, hy_b2, hy_freq2,
           hy_w3, hy_d, hy_norm_w, ffn_w_up, ffn_conv_w, ffn_w_down, final_norm_w):
    depth, d, _ = w_in.shape
    vec3 = lambda a: a.reshape(a.shape[0], 1, a.shape[1])
    p = dict(w_in=w_in, w_in_bf=w_in.astype(BF16), w_out_bf=w_out.astype(BF16),
             ffn_w_up_bf=ffn_w_up.astype(BF16), ffn_w_down=ffn_w_down, ffn_w_down_bf=ffn_w_down.astype(BF16),
             norm1_w3=vec3(norm1_w), norm2_w3=vec3(norm2_w), hg_norm_w3=vec3(hg_norm_w),
             hy_d=hy_d, hy_d3=vec3(hy_d), hy_norm_w3=vec3(hy_norm_w), final_norm_w2=final_norm_w[None, :],
             hg_lb_fwd=hg_lb_fwd, hg_lb_bwd=hg_lb_bwd, hy_conv_w=hy_conv_w, ffn_conv_w=ffn_conv_w,
             hy_w1=hy_w1, hy_b1=hy_b1, hy_freq1=hy_freq1, hy_w2=hy_w2, hy_b2=hy_b2, hy_freq2=hy_freq2,
             hy_w3=hy_w3)

    n_dec = c.shape[0]
    cv = jnp.concatenate([c, c_ctx[None, :], jnp.zeros((MOD_ROWS - n_dec - 1, d), F32)], axis=0)
    mod = _modulation(cv, ada_w, ada_b).reshape(depth * MOD_ROWS, 1, 6 * d)

    seq_p, seq_s = x_prompt.shape[1], x_sample.shape[1]
    y_p, states = _group(x_prompt, mod, lambda l, b: l * MOD_ROWS + n_dec, seq_p, None, True,
                         _filters(seq_p, p), TOKEN_TILE, p)
    y_s, _ = _group(x_sample, mod, lambda l, b: l * MOD_ROWS + b, GRID_W, state_hgrn, False,
                    _filters(seq_s, p), TOKEN_TILE, p)
    return (y_p, y_s, jnp.stack(states, axis=1))
```

```python
import functools
import math

import numpy as np
import jax
import jax.numpy as jnp
from jax import lax
from jax.experimental import pallas as pl
from jax.experimental.pallas import tpu as pltpu

F32 = jnp.float32
BF16 = jnp.bfloat16

HG_DK = 128
CHUNK = 64
GRID_W = 64
HY_BANDS = 16
HY_TARGET = 1e-2
HY_FAST = 0.3
HY_SLOW = 1.5
EPS = 1e-6

LANES = 128
VMEM_LIMIT = 56 * 1024 * 1024

N_LEVELS = int(math.log2(CHUNK))
MOD_ROWS = 16
PRE_TILE = 1024
POST_TILE = 512
PROJ_CHUNK = 256


def _bdot(a, b):
    return jnp.dot(a.astype(BF16), b.astype(BF16), preferred_element_type=F32)


def _bdot_nt(a, b):
    return lax.dot_general(a.astype(BF16), b.astype(BF16), (((1,), (1,)), ((), ())),
                           preferred_element_type=F32)


def _bdot_tn(a, b):
    return lax.dot_general(a.astype(BF16), b.astype(BF16), (((0,), (0,)), ((), ())),
                           preferred_element_type=F32)


def _hdot(a, b):
    return jnp.dot(a, b, preferred_element_type=F32, precision=lax.Precision.HIGHEST)


def _silu(x):
    return x * jax.nn.sigmoid(x)


def _rms(x):
    return x * lax.rsqrt(jnp.mean(x * x, axis=-1, keepdims=True) + EPS)


def _const_spec(shape, index_map):
    return pl.BlockSpec(shape, index_map, pipeline_mode=pl.Buffered(1))


def _dwconv3(p, w, row_w):
    n = p.shape[0]
    r = lax.broadcasted_iota(jnp.int32, (n, 1), 0) & (row_w - 1)
    prev = jnp.where(r == 0, 0.0, pltpu.roll(p, 1, 0))
    nxt = jnp.where(r == row_w - 1, 0.0, pltpu.roll(p, n - 1, 0))
    return w[0:1] * prev + w[1:2] * p + w[2:3] * nxt


def _mod_kernel(cv_ref, w_ref, b_ref, o_ref):
    s = _silu(cv_ref[...])
    w = w_ref[0]
    s_hi = s.astype(BF16)
    s_lo = (s - s_hi.astype(F32)).astype(BF16)
    w_hi = w.astype(BF16)
    w_lo = (w - w_hi.astype(F32)).astype(BF16)
    rows = s.shape[0]
    hi = jnp.dot(jnp.concatenate([s_hi, s_lo], axis=0), w_hi, preferred_element_type=F32)
    o_ref[0] = hi[:rows] + hi[rows:] + jnp.dot(s_hi, w_lo, preferred_element_type=F32) + b_ref[0]


def _modulation(cv, ada_w, ada_b):
    depth, d, n = ada_w.shape
    tn = 1536
    return pl.pallas_call(
        _mod_kernel,
        grid=(depth, n // tn),
        in_specs=[
            pl.BlockSpec((MOD_ROWS, d), lambda l, j: (0, 0)),
            pl.BlockSpec((1, d, tn), lambda l, j: (l, 0, j)),
            pl.BlockSpec((1, 1, tn), lambda l, j: (l, 0, j)),
        ],
        out_specs=pl.BlockSpec((1, MOD_ROWS, tn), lambda l, j: (l, 0, j)),
        out_shape=jax.ShapeDtypeStruct((depth, MOD_ROWS, n), F32),
        compiler_params=pltpu.CompilerParams(vmem_limit_bytes=VMEM_LIMIT),
        name="modulation",
    )(cv, ada_w, ada_b.reshape(depth, 1, n))


def _dft_tables(L):
    f = np.arange(L, dtype=np.int64)[:, None]
    t = np.arange(L, dtype=np.int64)[None, :]
    ang = 2.0 * np.pi * (((2 * f + 1) * t) % (4 * L)).astype(np.float64) / (4 * L)
    return np.cos(ang), np.sin(ang)


def _filter_feats(L):
    half = LANES // 2

    def feats(pos):
        t = pos / (L - 1)
        bands = np.linspace(1e-4, HY_BANDS - 1, HY_BANDS)[None, :]
        ang = 2.0 * np.pi * pos[:, None] * bands / L
        out = np.zeros((pos.shape[0], half), np.float64)
        out[:, 0] = t
        out[:, 1:1 + HY_BANDS] = np.cos(ang)
        out[:, 1 + HY_BANDS:1 + 2 * HY_BANDS] = np.sin(ang)
        return out
    pos = np.arange(L, dtype=np.float64)
    return np.concatenate([feats(pos), feats(L - pos)], axis=1).astype(np.float32)


def _filter_taps_kernel(L, hw, ft_ref, w1_ref, b1_ref, f1_ref, w2_ref, b2_ref, f2_ref, w3_ref,
                        dl_ref, kk_ref):
    half = LANES // 2
    ft = ft_ref[...]
    h = jnp.sin(f1_ref[0] * (_hdot(ft, w1_ref[0]) + b1_ref[0]))
    h = jnp.sin(f2_ref[0] * (_hdot(h, w2_ref[0]) + b2_ref[0]))
    h = _hdot(h, w3_ref[0])
    dl = dl_ref[...]
    k1 = h[:, :hw] * jnp.exp(-ft[:, 0:1] * dl)
    row = lax.broadcasted_iota(jnp.int32, (L, 1), 0)
    k2 = jnp.where(row == 0, 0.0, -(h[:, hw:] * jnp.exp(-ft[:, half:half + 1] * dl)))
    norm = jnp.sum(jnp.abs(k1), axis=0, keepdims=True) + jnp.sum(jnp.abs(k2), axis=0, keepdims=True) + EPS
    kk_ref[0, :, :hw] = (k1 / norm).astype(BF16)
    kk_ref[0, :, hw:] = (k2 / norm).astype(BF16)


def _filter_dft_kernel(hw, kk_ref, cm_ref, sm_ref, kr_ref, ki_ref):
    kk = kk_ref[0]
    gc = jnp.dot(cm_ref[...], kk, preferred_element_type=F32)
    gs = jnp.dot(sm_ref[...], kk, preferred_element_type=F32)
    tf = gc.shape[0]
    row = lax.broadcasted_iota(jnp.int32, (tf, 1), 0)
    alt = jnp.where((row & 1) == 0, 1.0, -1.0)
    kr_ref[0] = gc[:, :hw] - alt * gs[:, hw:]
    ki_ref[0] = -(gs[:, :hw] + alt * gc[:, hw:])


def _filters(L, p):
    depth, order, hw2 = p["hy_w3"].shape
    hw = hw2 // 2
    half = LANES // 2
    cm, sm = _dft_tables(L)
    ft = jnp.asarray(_filter_feats(L))
    min_decay = math.log(HY_TARGET) / HY_SLOW
    max_decay = math.log(HY_TARGET) / HY_FAST
    deltas = jnp.asarray(np.abs(np.linspace(min_decay, max_decay, hw)).astype(np.float32)[None, :])

    def pad_to(a, rows, cols):
        return jnp.pad(a, ((0, 0), (0, rows - a.shape[1]), (0, cols - a.shape[2])))

    def block_diag(a, b):
        za = jnp.zeros(a.shape[:2] + (b.shape[2],), F32)
        zb = jnp.zeros(b.shape[:2] + (a.shape[2],), F32)
        return jnp.concatenate([jnp.concatenate([a, za], axis=2), jnp.concatenate([zb, b], axis=2)], axis=1)

    w1 = pad_to(p["hy_w1"], half, half)
    w2 = pad_to(p["hy_w2"], half, half)
    w3 = pad_to(p["hy_w3"], half, hw2)
    w1, w2, w3 = block_diag(w1, w1), block_diag(w2, w2), block_diag(w3[:, :, :hw], w3[:, :, hw:])
    vec = lambda a: jnp.tile(pad_to(a[:, None, :], 1, half), (1, 1, 2))
    lay = lambda shape: pl.BlockSpec((1,) + shape, lambda l: (l, 0, 0))
    kk = pl.pallas_call(
        functools.partial(_filter_taps_kernel, L, hw),
        grid=(depth,),
        in_specs=[
            _const_spec((L, LANES), lambda l: (0, 0)),
            lay((LANES, LANES)), lay((1, LANES)), lay((1, LANES)),
            lay((LANES, LANES)), lay((1, LANES)), lay((1, LANES)),
            lay((LANES, hw2)),
            _const_spec((1, hw), lambda l: (0, 0)),
        ],
        out_specs=lay((L, hw2)),
        out_shape=jax.ShapeDtypeStruct((depth, L, hw2), BF16),
        compiler_params=pltpu.CompilerParams(vmem_limit_bytes=VMEM_LIMIT),
        name=f"hyena_filter_taps_{L}",
    )(ft, w1, vec(p["hy_b1"]), vec(p["hy_freq1"]), w2, vec(p["hy_b2"]), vec(p["hy_freq2"]), w3, deltas)
    tf = 256
    frq = pl.BlockSpec((tf, L), lambda l, j: (j, 0))
    out = pl.BlockSpec((1, tf, hw), lambda l, j: (l, j, 0))
    return pl.pallas_call(
        functools.partial(_filter_dft_kernel, hw),
        grid=(depth, L // tf),
        in_specs=[pl.BlockSpec((1, L, hw2), lambda l, j: (l, 0, 0)), frq, frq],
        out_specs=[out, out],
        out_shape=[jax.ShapeDtypeStruct((depth, L, hw), F32)] * 2,
        compiler_params=pltpu.CompilerParams(vmem_limit_bytes=VMEM_LIMIT),
        name=f"hyena_filter_dft_{L}",
    )(kk, jnp.asarray(cm, F32).astype(BF16), jnp.asarray(sm, F32).astype(BF16))


def _lower_bound(lb_ref, l):
    prm = lb_ref[...]
    e = jnp.exp(prm - jnp.max(prm, axis=0, keepdims=True))
    s = e / jnp.sum(e, axis=0, keepdims=True)
    acc = jnp.zeros_like(s[0:1])
    for i in range(1, l + 1):
        acc = acc + s[i:i + 1]
    return acc


def _forget_gate(z, lb):
    t = jnp.exp(-jnp.abs(z))
    one_t = 1.0 + t
    pos = z >= 0.0
    num = jnp.where(pos, 1.0 + lb * t, lb + t)
    log_f = jnp.where(num > 0.0, jnp.log(num), z) - jnp.log(one_t)
    return log_f, (1.0 - lb) * (jnp.where(pos, t, 1.0) / one_t)


def _pre_kernel(l, row_w, hg, hy, x_ref, mod_ref, n1_ref, win_ref, lbf_ref, lbb_ref, cw_ref,
                q_ref, kf_ref, kb_ref, lf_ref, lb_ref, v_ref, sg_ref, u_ref, x2_ref, h_scr):
    d = x_ref.shape[1]
    m = mod_ref[0]
    sh1, sc1 = m[:, 0:d], m[:, d:2 * d]
    h_scr[...] = (_rms(x_ref[...]) * n1_ref[0] * (1.0 + sc1) + sh1).astype(BF16)

    w = PROJ_CHUNK
    cw = cw_ref[0]
    lbs = (_lower_bound(lbf_ref, l), _lower_bound(lbb_ref, l))
    conv = lambda r, j, c: _dwconv3(r, cw[:, j * hy + c:j * hy + c + w], row_w)

    tasks = []
    for c in range(0, hg, w):
        cs = slice(c, c + w)

        def q_task(r, cs=cs):
            q_ref[:, cs] = (_silu(r[0]) * (HG_DK ** -0.5)).astype(q_ref.dtype)

        def gate_task(r, cs=cs, i=0):
            g_ref, k_ref = ((lf_ref, kf_ref), (lb_ref, kb_ref))[i]
            log_f, k = _forget_gate(r[0], lbs[i][:, cs])
            g_ref[:, cs], k_ref[:, cs] = log_f, k.astype(k_ref.dtype)

        def v_task(r, cs=cs):
            v_ref[:, cs] = r[0].astype(v_ref.dtype)

        def sg_task(r, cs=cs):
            sg_ref[:, cs] = _silu(r[0]).astype(sg_ref.dtype)

        def u_task(r, cs=cs, c=c):
            u_ref[:, cs] = (conv(r[1], 1, c) * conv(r[0], 0, c)).astype(u_ref.dtype)

        def x2_task(r, cs=cs, c=c):
            x2_ref[:, cs] = conv(r[0], 2, c).astype(x2_ref.dtype)

        tasks += [((c,), q_task), ((hg + c,), gate_task), ((2 * hg + c,), functools.partial(gate_task, i=1)),
                  ((3 * hg + c,), v_task), ((4 * hg + c,), sg_task),
                  ((5 * hg + c, 5 * hg + hy + c), u_task), ((5 * hg + 2 * hy + c,), x2_task)]

    issue = lambda offs: [jnp.dot(h_scr[...], win_ref[0, :, o:o + w], preferred_element_type=F32) for o in offs]
    pending = issue(tasks[0][0])
    for i, (_, consume) in enumerate(tasks):
        ahead = issue(tasks[i + 1][0]) if i + 1 < len(tasks) else None
        consume(pending)
        pending = ahead


def _pre(l, x, mod, mod_row, row_w, seq, tm, p):
    t, d = x.shape
    hg = p["hg_lb_fwd"].shape[1]
    hy = p["hy_d"].shape[1]
    n_in = p["w_in"].shape[2]
    depth = p["w_in"].shape[0]
    tok = lambda w: pl.BlockSpec((tm, w), lambda i: (i, 0))
    out = lambda dt: jax.ShapeDtypeStruct((t, hg), dt)
    return pl.pallas_call(
        functools.partial(_pre_kernel, l, row_w, hg, hy),
        grid=(t // tm,),
        in_specs=[
            tok(d),
            pl.BlockSpec((1, 1, mod.shape[2]), lambda i: (mod_row(l, i * tm // seq), 0, 0)),
            _const_spec((1, 1, d), lambda i: (l, 0, 0)),
            _const_spec((1, d, n_in), lambda i: (l, 0, 0)),
            _const_spec((depth, hg), lambda i: (0, 0)),
            _const_spec((depth, hg), lambda i: (0, 0)),
            _const_spec((1, 3, 3 * hy), lambda i: (l, 0, 0)),
        ],
        out_specs=[tok(hg)] * 9,
        out_shape=[out(BF16)] * 3 + [out(F32)] * 2 + [out(BF16)] * 4,
        scratch_shapes=[pltpu.VMEM((tm, d), BF16)],
        compiler_params=pltpu.CompilerParams(vmem_limit_bytes=VMEM_LIMIT),
        name="pre_mixer",
    )(x, mod, p["norm1_w3"], p["w_in_bf"], p["hg_lb_fwd"], p["hg_lb_bwd"], p["hy_conv_w"])


LEVELS = tuple(CHUNK >> (i + 1) for i in range(N_LEVELS))
SUBLANES = 8
MXU_LEVELS = tuple(s for s in LEVELS if 1 < s < SUBLANES)
LOG2E = 1.4426950408889634


def _scan_tables():
    c = CHUNK
    j = np.arange(c)[:, None]
    i = np.arange(c)[None, :]
    masks = [((j // (2 * s) == i // (2 * s)) & (j % (2 * s) >= s) & (i % (2 * s) < s)).astype(np.float32)
             for s in LEVELS]
    masks.append(np.eye(c, dtype=np.float32))
    mf = np.stack(masks)
    blocks = [np.tril(np.ones((c, c), np.float32))]
    for s in MXU_LEVELS:
        w = np.zeros((c, c), np.float32)
        for t in range(c):
            ref = (t // (2 * s)) * 2 * s + s - 1
            if t % (2 * s) >= s:
                w[t, ref + 1:t + 1] = 1.0
            else:
                w[t, t + 1:ref + 1] = 1.0
        blocks.append(w)
    fwd = np.concatenate(blocks, axis=0)
    bwd = np.concatenate([w[::-1, ::-1] for w in blocks], axis=0)
    sums = np.stack([np.concatenate([fwd] * 3, axis=1), np.concatenate([bwd] * 3, axis=1)])
    return sums, np.stack([mf, mf[:, ::-1, ::-1]])


def _decay_logs(sums, g, d):
    c = g.shape[0]
    fwd = d == 0
    b = sums[:c]
    out = []
    for s in LEVELS:
        if s >= SUBLANES:
            parts = []
            for b0 in range(0, c, 2 * s):
                ref = b0 + s - 1 if fwd else b0 + s
                br, lo, hi = b[ref:ref + 1], b[b0:b0 + s], b[b0 + s:b0 + 2 * s]
                parts += [br - lo, hi - br] if fwd else [lo - br, br - hi]
            out.append(jnp.concatenate(parts, axis=0))
        elif s in MXU_LEVELS:
            i = 1 + MXU_LEVELS.index(s)
            out.append(sums[i * c:(i + 1) * c])
        else:
            r = lax.broadcasted_iota(jnp.int32, (c, 1), 0)
            out.append(jnp.where((r & 1) == (1 if fwd else 0), g, 0.0))
    far = b[c - 1:c] if fwd else b[0:1]
    return out + [b, far - b]


def _scan_kernel(nc, heads, has_s0, want_state, *refs):
    q_ref, kf_ref, kb_ref, lf_ref, lb_ref, v_ref = refs[:6]
    refs = refs[6:]
    if has_s0:
        s0_ref, refs = refs[0], refs[1:]
    sum_ref, m_ref, o_ref = refs[:3]
    refs = refs[3:]
    if want_state:
        st_ref, refs = refs[0], refs[1:]
    s_scr, ob_scr, qs_scr, ks_scr, v_scr, ea_scr = refs
    c = CHUNK
    dirs = ((kf_ref, lf_ref), (kb_ref, lb_ref))
    pairs = [(d, h) for d in range(2) for h in range(heads)]
    cols = lambda h: slice(h * HG_DK, (h + 1) * HG_DK)

    def rows_of(ci, d):
        return pl.ds((ci if d == 0 else nc - 1 - ci) * c, c)

    def prep_start(ci):
        out = []
        for d, (k_ref, g_ref) in enumerate(dirs):
            rows = rows_of(ci, d)
            g = g_ref[rows, :] * LOG2E
            g1 = g.astype(BF16)
            r1 = g - g1.astype(F32)
            g2 = r1.astype(BF16)
            g3 = (r1 - g2.astype(F32)).astype(BF16)
            sums = jnp.dot(sum_ref[d], jnp.concatenate([g1, g2, g3], axis=0), preferred_element_type=F32)
            out.append((g, q_ref[rows, :], k_ref[rows, :], v_ref[rows, :], sums))
        return out

    def prep_finish(slot, prepped):
        for d, (g, q, k, v, sums) in enumerate(prepped):
            e = [jnp.exp2(x) for x in _decay_logs(sums, g, d)]
            eb = [x.astype(BF16) for x in e]
            qb, kb = q.astype(BF16), k.astype(BF16)
            for s in range(N_LEVELS):
                qs_scr[slot, d, s] = qb * eb[s]
                ks_scr[slot, d, s] = kb * eb[s]
            qs_scr[slot, d, N_LEVELS] = qb
            ks_scr[slot, d, N_LEVELS] = kb
            qs_scr[slot, d, N_LEVELS + 1] = qb * eb[N_LEVELS]
            ks_scr[slot, d, N_LEVELS + 1] = kb * eb[N_LEVELS + 1]
            v_scr[slot, d] = v.astype(BF16)
            ea_scr[slot, d, 0:1, :] = e[N_LEVELS][c - 1:c] if d == 0 else e[N_LEVELS][0:1]

    def use_start(slot):
        scores, inter, new_state, vals = {}, {}, {}, {}
        for d, h in pairs:
            scores[d, h] = [_bdot_nt(qs_scr[slot, d, s, :, cols(h)], ks_scr[slot, d, s, :, cols(h)])
                            for s in range(N_LEVELS + 1)]
            st = s_scr[d, h]
            vals[d, h] = v_scr[slot, d, :, cols(h)]
            inter[d, h] = _bdot_nt(qs_scr[slot, d, N_LEVELS + 1, :, cols(h)], st)
            new_state[d, h] = (ea_scr[slot, d, 0:1, cols(h)] * st
                               + _bdot_tn(vals[d, h], ks_scr[slot, d, N_LEVELS + 1, :, cols(h)]))
        return scores, inter, new_state, vals

    def use_finish(ci, scores, inter, new_state, vals):
        for d in range(2):
            keep = [m_ref[d, s] > 0.5 for s in range(N_LEVELS + 1)]
            for h in range(heads):
                a = jnp.where(keep[N_LEVELS], scores[d, h][N_LEVELS], 0.0)
                for s in range(N_LEVELS):
                    a = jnp.where(keep[s], scores[d, h][s], a)
                o = _bdot(a, vals[d, h]) + inter[d, h]
                (o_ref if d == 0 else ob_scr)[rows_of(ci, d), cols(h)] = o
                s_scr[d, h] = new_state[d, h]

    for d, h in pairs:
        s_scr[d, h] = s0_ref[0, 0, d, h].T if has_s0 else jnp.zeros(s_scr.shape[2:], F32)

    prep_finish(0, prep_start(0))
    for ci in range(nc):
        slot = ci % 2
        prepped = prep_start(ci + 1) if ci + 1 < nc else None
        part = use_start(slot)
        if prepped is not None:
            prep_finish(1 - slot, prepped)
        use_finish(ci, *part)
    o_ref[...] += ob_scr[...]
    if want_state:
        for d in range(2):
            for h in range(heads):
                st_ref[0, d, h] = s_scr[d, h].T


def _scan(l, seq, nseq, arrs, s0, want_state):
    t, hg = arrs[0].shape
    heads = hg // HG_DK
    nc = seq // CHUNK
    sums, masks = _scan_tables()
    sums, masks = jnp.asarray(sums, F32).astype(BF16), jnp.asarray(masks)
    seq_spec = pl.BlockSpec((seq, hg), lambda b: (b, 0))
    in_specs = [seq_spec] * 6
    args = list(arrs)
    if s0 is not None:
        in_specs.append(pl.BlockSpec((1, 1) + s0.shape[2:], lambda b: (b, l, 0, 0, 0, 0)))
        args.append(s0)
    in_specs += [_const_spec(sums.shape, lambda b: (0, 0, 0)), _const_spec(masks.shape, lambda b: (0, 0, 0, 0))]
    args += [sums, masks]
    out_specs = [seq_spec]
    out_shape = [jax.ShapeDtypeStruct((t, hg), F32)]
    if want_state:
        out_specs.append(pl.BlockSpec((1, 2, heads, HG_DK, HG_DK), lambda b: (b, 0, 0, 0, 0)))
        out_shape.append(jax.ShapeDtypeStruct((nseq, 2, heads, HG_DK, HG_DK), F32))
    res = pl.pallas_call(
        functools.partial(_scan_kernel, nc, heads, s0 is not None, want_state),
        grid=(nseq,),
        in_specs=in_specs,
        out_specs=out_specs,
        out_shape=out_shape,
        scratch_shapes=[pltpu.VMEM((2, heads, HG_DK, HG_DK), F32), pltpu.VMEM((seq, hg), F32),
                        pltpu.VMEM((2, 2, N_LEVELS + 2, CHUNK, hg), BF16),
                        pltpu.VMEM((2, 2, N_LEVELS + 2, CHUNK, hg), BF16),
                        pltpu.VMEM((2, 2, CHUNK, hg), BF16), pltpu.VMEM((2, 2, SUBLANES, hg), F32)],
        compiler_params=pltpu.CompilerParams(vmem_limit_bytes=VMEM_LIMIT),
        name="hgrn2_scan",
    )(*args)
    return res if want_state else (res[0], None)


def _hyena_kernel(L, u_ref, x2_ref, kr_ref, ki_ref, d_ref, nw_ref, cs_ref, ics_ref, y_ref):
    u = u_ref[...]
    uu = jnp.dot(cs_ref[...], u.astype(BF16), preferred_element_type=F32)
    ur, us = uu[:L], uu[L:]
    kr, ki = kr_ref[0], ki_ref[0]
    zr = ur * kr + us * ki
    zi = ur * ki - us * kr
    zz = jnp.concatenate([zr.astype(BF16), zi.astype(BF16)], axis=0)
    z = jnp.dot(ics_ref[...], zz, preferred_element_type=F32)
    y = x2_ref[...] * (z + d_ref[0] * u)
    y_ref[...] = (_rms(y) * nw_ref[0]).astype(y_ref.dtype)


def _hyena(l, seq, nseq, u, x2, kr, ki, p):
    t, hw = u.shape
    cm, sm = _dft_tables(seq)
    cs = jnp.asarray(np.concatenate([cm, sm], axis=0), F32).astype(BF16)
    ics = jnp.asarray(np.concatenate([cm.T, -sm.T], axis=1) / seq, F32).astype(BF16)
    seq_spec = pl.BlockSpec((seq, hw), lambda b: (b, 0))
    lay = lambda shape: _const_spec((1,) + shape, lambda b: (l, 0, 0))
    return pl.pallas_call(
        functools.partial(_hyena_kernel, seq),
        grid=(nseq,),
        in_specs=[seq_spec, seq_spec, lay((seq, hw)), lay((seq, hw)), lay((1, hw)), lay((1, hw)),
                  _const_spec(cs.shape, lambda b: (0, 0)), _const_spec(ics.shape, lambda b: (0, 0))],
        out_specs=seq_spec,
        out_shape=jax.ShapeDtypeStruct((t, hw), BF16),
        compiler_params=pltpu.CompilerParams(vmem_limit_bytes=VMEM_LIMIT),
        name="hyena_conv",
    )(u, x2, kr, ki, p["hy_d3"], p["hy_norm_w3"], cs, ics)


def _post_kernel(row_w, heads, ff_chunk, last, x_ref, o_ref, sg_ref, y_ref, mod_ref, hn_ref, wout_ref,
                 n2_ref, wup_ref, cw_ref, wdn_ref, fn_ref, out_ref, x_scr, h2_scr, act_scr):
    d = x_ref.shape[1]
    hg = o_ref.shape[1]
    dff = wdn_ref.shape[1]
    m = mod_ref[0]
    g1, sh2, sc2, g2 = (m[:, i * d:(i + 1) * d] for i in range(2, 6))
    hn = hn_ref[0]
    mix = _bdot(y_ref[...], wout_ref[0, hg:, :])
    for h in range(heads):
        cols = slice(h * HG_DK, (h + 1) * HG_DK)
        mix = mix + _bdot(_rms(o_ref[:, cols]) * hn[:, cols] * sg_ref[:, cols], wout_ref[0, cols, :])
    x_scr[...] = x_ref[...] + g1 * mix
    h2_scr[...] = (_rms(x_scr[...]) * n2_ref[0] * (1.0 + sc2) + sh2).astype(BF16)
    cw = cw_ref[0]

    for j in range(dff // ff_chunk):
        a, b = j * ff_chunk, (j + 1) * ff_chunk
        gate = _dwconv3(jnp.dot(h2_scr[...], wup_ref[0, :, a:b], preferred_element_type=F32),
                        cw[:, a:b], row_w)
        val = _dwconv3(jnp.dot(h2_scr[...], wup_ref[0, :, dff + a:dff + b], preferred_element_type=F32),
                       cw[:, dff + a:dff + b], row_w)
        act_scr[:, a:b] = (_silu(gate) * val).astype(BF16)
    x = x_scr[...] + g2 * jnp.dot(act_scr[...], wdn_ref[0], preferred_element_type=F32)
    out_ref[...] = _rms(x) * fn_ref[...] if last else x


def _post(l, x, o, sg, y, mod, mod_row, row_w, seq, tm, last, p):
    t, d = x.shape
    hg = o.shape[1]
    dff = p["ffn_w_down"].shape[1]
    tok = lambda w: pl.BlockSpec((tm, w), lambda i: (i, 0))
    lay = lambda shape: _const_spec((1,) + shape, lambda i: (l, 0, 0))
    return pl.pallas_call(
        functools.partial(_post_kernel, row_w, hg // HG_DK, 256, last),
        grid=(t // tm,),
        in_specs=[
            tok(d), tok(hg), tok(hg), tok(y.shape[1]),
            pl.BlockSpec((1, 1, mod.shape[2]), lambda i: (mod_row(l, i * tm // seq), 0, 0)),
            lay((1, hg)), lay((d, d)), lay((1, d)), lay((d, 2 * dff)), lay((3, 2 * dff)), lay((dff, d)),
            _const_spec((1, d), lambda i: (0, 0)),
        ],
        out_specs=tok(d),
        out_shape=jax.ShapeDtypeStruct((t, d), F32),
        scratch_shapes=[pltpu.VMEM((tm, d), F32), pltpu.VMEM((tm, d), BF16), pltpu.VMEM((tm, dff), BF16)],
        compiler_params=pltpu.CompilerParams(vmem_limit_bytes=VMEM_LIMIT),
        name="post_mixer_ffn",
    )(x, o, sg, y, mod, p["hg_norm_w3"], p["w_out_bf"], p["norm2_w3"], p["ffn_w_up_bf"],
      p["ffn_conv_w"], p["ffn_w_down_bf"], p["final_norm_w2"])


def _group(x3, mod, mod_row, row_w, s0, want_state, filt, p):
    nseq, seq, d = x3.shape
    depth = p["w_in"].shape[0]
    x = x3.reshape(nseq * seq, d)
    kr, ki = filt
    states = []
    for l in range(depth):
        q, kf, kb, lf, lb, v, sg, u, x2 = _pre(l, x, mod, mod_row, row_w, seq, PRE_TILE, p)
        o, st = _scan(l, seq, nseq, (q, kf, kb, lf, lb, v), s0, want_state)
        y = _hyena(l, seq, nseq, u, x2, kr, ki, p)
        x = _post(l, x, o, sg, y, mod, mod_row, row_w, seq, POST_TILE, l == depth - 1, p)
        states.append(st)
    return x.reshape(nseq, seq, d), states


def kernel(x_prompt, x_sample, state_hgrn, c, c_ctx, w_in, w_out, ada_w, ada_b, norm1_w, norm2_w,
           hg_lb_fwd, hg_lb_bwd, hg_norm_w, hy_conv_w, hy_w1, hy_b1, hy_freq1, hy_w2, hy_b2, hy_freq2,
           hy_w3, hy_d, hy_norm_w, ffn_w_up, ffn_conv_w, ffn_w_down, final_norm_w):
    depth, d, _ = w_in.shape
    vec3 = lambda a: a.reshape(a.shape[0], 1, a.shape[1])
    p = dict(w_in=w_in, w_in_bf=w_in.astype(BF16), w_out_bf=w_out.astype(BF16),
             ffn_w_up_bf=ffn_w_up.astype(BF16), ffn_w_down=ffn_w_down, ffn_w_down_bf=ffn_w_down.astype(BF16),
             norm1_w3=vec3(norm1_w), norm2_w3=vec3(norm2_w), hg_norm_w3=vec3(hg_norm_w),
             hy_d=hy_d, hy_d3=vec3(hy_d), hy_norm_w3=vec3(hy_norm_w), final_norm_w2=final_norm_w[None, :],
             hg_lb_fwd=hg_lb_fwd, hg_lb_bwd=hg_lb_bwd, hy_conv_w=hy_conv_w, ffn_conv_w=ffn_conv_w,
             hy_w1=hy_w1, hy_b1=hy_b1, hy_freq1=hy_freq1, hy_w2=hy_w2, hy_b2=hy_b2, hy_freq2=hy_freq2,
             hy_w3=hy_w3)

    n_dec = c.shape[0]
    cv = jnp.concatenate([c, c_ctx[None, :], jnp.zeros((MOD_ROWS - n_dec - 1, d), F32)], axis=0)
    mod = _modulation(cv, ada_w, ada_b).reshape(depth * MOD_ROWS, 1, 6 * d)

    seq_p, seq_s = x_prompt.shape[1], x_sample.shape[1]
    y_p, states = _group(x_prompt, mod, lambda l, b: l * MOD_ROWS + n_dec, seq_p, None, True,
                         _filters(seq_p, p), p)
    y_s, _ = _group(x_sample, mod, lambda l, b: l * MOD_ROWS + b, GRID_W, state_hgrn, False,
                    _filters(seq_s, p), p)
    return (y_p, y_s, jnp.stack(states, axis=1))
```

```python
import functools
import math

import numpy as np
import jax
import jax.numpy as jnp
from jax import lax
from jax.experimental import pallas as pl
from jax.experimental.pallas import tpu as pltpu

F32 = jnp.float32
BF16 = jnp.bfloat16

HG_DK = 128
CHUNK = 64
GRID_W = 64
HY_BANDS = 16
HY_TARGET = 1e-2
HY_FAST = 0.3
HY_SLOW = 1.5
EPS = 1e-6

LANES = 128
VMEM_LIMIT = 56 * 1024 * 1024

N_LEVELS = int(math.log2(CHUNK))
MOD_ROWS = 16
PRE_TILE = 1024
POST_TILE = 512
PROJ_CHUNK = 256


def _bdot(a, b):
    return jnp.dot(a.astype(BF16), b.astype(BF16), preferred_element_type=F32)


def _bdot_nt(a, b):
    return lax.dot_general(a.astype(BF16), b.astype(BF16), (((1,), (1,)), ((), ())),
                           preferred_element_type=F32)


def _bdot_tn(a, b):
    return lax.dot_general(a.astype(BF16), b.astype(BF16), (((0,), (0,)), ((), ())),
                           preferred_element_type=F32)


def _hdot(a, b):
    return jnp.dot(a, b, preferred_element_type=F32, precision=lax.Precision.HIGHEST)


def _silu(x):
    return x * jax.nn.sigmoid(x)


def _rms(x):
    return x * lax.rsqrt(jnp.mean(x * x, axis=-1, keepdims=True) + EPS)


def _const_spec(shape, index_map):
    return pl.BlockSpec(shape, index_map, pipeline_mode=pl.Buffered(1))


def _dwconv3(p, w, row_w):
    n = p.shape[0]
    r = lax.broadcasted_iota(jnp.int32, (n, 1), 0) & (row_w - 1)
    prev = jnp.where(r == 0, 0.0, pltpu.roll(p, 1, 0))
    nxt = jnp.where(r == row_w - 1, 0.0, pltpu.roll(p, n - 1, 0))
    return w[0:1] * prev + w[1:2] * p + w[2:3] * nxt


def _mod_kernel(cv_ref, w_ref, b_ref, o_ref):
    s = _silu(cv_ref[...])
    w = w_ref[0]
    s_hi = s.astype(BF16)
    s_lo = (s - s_hi.astype(F32)).astype(BF16)
    w_hi = w.astype(BF16)
    w_lo = (w - w_hi.astype(F32)).astype(BF16)
    rows = s.shape[0]
    hi = jnp.dot(jnp.concatenate([s_hi, s_lo], axis=0), w_hi, preferred_element_type=F32)
    o_ref[0] = hi[:rows] + hi[rows:] + jnp.dot(s_hi, w_lo, preferred_element_type=F32) + b_ref[0]


def _modulation(cv, ada_w, ada_b):
    depth, d, n = ada_w.shape
    tn = 1536
    return pl.pallas_call(
        _mod_kernel,
        grid=(depth, n // tn),
        in_specs=[
            pl.BlockSpec((MOD_ROWS, d), lambda l, j: (0, 0)),
            pl.BlockSpec((1, d, tn), lambda l, j: (l, 0, j)),
            pl.BlockSpec((1, 1, tn), lambda l, j: (l, 0, j)),
        ],
        out_specs=pl.BlockSpec((1, MOD_ROWS, tn), lambda l, j: (l, 0, j)),
        out_shape=jax.ShapeDtypeStruct((depth, MOD_ROWS, n), F32),
        compiler_params=pltpu.CompilerParams(vmem_limit_bytes=VMEM_LIMIT),
        name="modulation",
    )(cv, ada_w, ada_b.reshape(depth, 1, n))


def _dft_tables(L):
    f = np.arange(L, dtype=np.int64)[:, None]
    t = np.arange(L, dtype=np.int64)[None, :]
    ang = 2.0 * np.pi * (((2 * f + 1) * t) % (4 * L)).astype(np.float64) / (4 * L)
    return np.cos(ang), np.sin(ang)


def _paired_freq_order(L):
    half = np.arange(L // 2)
    return np.concatenate([half, L - 1 - half])


def _filter_feats(L):
    half = LANES // 2

    def feats(pos):
        t = pos / (L - 1)
        bands = np.linspace(1e-4, HY_BANDS - 1, HY_BANDS)[None, :]
        ang = 2.0 * np.pi * pos[:, None] * bands / L
        out = np.zeros((pos.shape[0], half), np.float64)
        out[:, 0] = t
        out[:, 1:1 + HY_BANDS] = np.cos(ang)
        out[:, 1 + HY_BANDS:1 + 2 * HY_BANDS] = np.sin(ang)
        return out
    pos = np.arange(L, dtype=np.float64)
    return np.concatenate([feats(pos), feats(L - pos)], axis=1).astype(np.float32)


def _filter_taps_kernel(L, hw, ft_ref, w1_ref, b1_ref, f1_ref, w2_ref, b2_ref, f2_ref, w3_ref,
                        dl_ref, kk_ref):
    half = LANES // 2
    ft = ft_ref[...]
    h = jnp.sin(f1_ref[0] * (_hdot(ft, w1_ref[0]) + b1_ref[0]))
    h = jnp.sin(f2_ref[0] * (_hdot(h, w2_ref[0]) + b2_ref[0]))
    h = _hdot(h, w3_ref[0])
    dl = dl_ref[...]
    k1 = h[:, :hw] * jnp.exp(-ft[:, 0:1] * dl)
    row = lax.broadcasted_iota(jnp.int32, (L, 1), 0)
    k2 = jnp.where(row == 0, 0.0, -(h[:, hw:] * jnp.exp(-ft[:, half:half + 1] * dl)))
    norm = jnp.sum(jnp.abs(k1), axis=0, keepdims=True) + jnp.sum(jnp.abs(k2), axis=0, keepdims=True) + EPS
    kk_ref[0, :, :hw] = (k1 / norm).astype(BF16)
    kk_ref[0, :, hw:] = (k2 / norm).astype(BF16)


def _filter_dft_kernel(hw, kk_ref, cm_ref, sm_ref, kr_ref, ki_ref):
    kk = kk_ref[0]
    gc = jnp.dot(cm_ref[...], kk, preferred_element_type=F32)
    gs = jnp.dot(sm_ref[...], kk, preferred_element_type=F32)
    tf = gc.shape[0]
    row = lax.broadcasted_iota(jnp.int32, (tf, 1), 0)
    second = pl.program_id(1) * tf + row >= kk.shape[0] // 2
    alt = jnp.where(((row & 1) == 0) != second, 1.0, -1.0)
    kr_ref[0] = gc[:, :hw] - alt * gs[:, hw:]
    ki_ref[0] = -(gs[:, :hw] + alt * gc[:, hw:])


def _filters(L, p):
    depth, order, hw2 = p["hy_w3"].shape
    hw = hw2 // 2
    half = LANES // 2
    cm, sm = _dft_tables(L)
    order = _paired_freq_order(L)
    ft = jnp.asarray(_filter_feats(L))
    min_decay = math.log(HY_TARGET) / HY_SLOW
    max_decay = math.log(HY_TARGET) / HY_FAST
    deltas = jnp.asarray(np.abs(np.linspace(min_decay, max_decay, hw)).astype(np.float32)[None, :])

    def pad_to(a, rows, cols):
        return jnp.pad(a, ((0, 0), (0, rows - a.shape[1]), (0, cols - a.shape[2])))

    def block_diag(a, b):
        za = jnp.zeros(a.shape[:2] + (b.shape[2],), F32)
        zb = jnp.zeros(b.shape[:2] + (a.shape[2],), F32)
        return jnp.concatenate([jnp.concatenate([a, za], axis=2), jnp.concatenate([zb, b], axis=2)], axis=1)

    w1 = pad_to(p["hy_w1"], half, half)
    w2 = pad_to(p["hy_w2"], half, half)
    w3 = pad_to(p["hy_w3"], half, hw2)
    w1, w2, w3 = block_diag(w1, w1), block_diag(w2, w2), block_diag(w3[:, :, :hw], w3[:, :, hw:])
    vec = lambda a: jnp.tile(pad_to(a[:, None, :], 1, half), (1, 1, 2))
    lay = lambda shape: pl.BlockSpec((1,) + shape, lambda l: (l, 0, 0))
    kk = pl.pallas_call(
        functools.partial(_filter_taps_kernel, L, hw),
        grid=(depth,),
        in_specs=[
            _const_spec((L, LANES), lambda l: (0, 0)),
            lay((LANES, LANES)), lay((1, LANES)), lay((1, LANES)),
            lay((LANES, LANES)), lay((1, LANES)), lay((1, LANES)),
            lay((LANES, hw2)),
            _const_spec((1, hw), lambda l: (0, 0)),
        ],
        out_specs=lay((L, hw2)),
        out_shape=jax.ShapeDtypeStruct((depth, L, hw2), BF16),
        compiler_params=pltpu.CompilerParams(vmem_limit_bytes=VMEM_LIMIT),
        name=f"hyena_filter_taps_{L}",
    )(ft, w1, vec(p["hy_b1"]), vec(p["hy_freq1"]), w2, vec(p["hy_b2"]), vec(p["hy_freq2"]), w3, deltas)
    tf = 256
    frq = pl.BlockSpec((tf, L), lambda l, j: (j, 0))
    out = pl.BlockSpec((1, tf, hw), lambda l, j: (l, j, 0))
    return pl.pallas_call(
        functools.partial(_filter_dft_kernel, hw),
        grid=(depth, L // tf),
        in_specs=[pl.BlockSpec((1, L, hw2), lambda l, j: (l, 0, 0)), frq, frq],
        out_specs=[out, out],
        out_shape=[jax.ShapeDtypeStruct((depth, L, hw), F32)] * 2,
        compiler_params=pltpu.CompilerParams(vmem_limit_bytes=VMEM_LIMIT),
        name=f"hyena_filter_dft_{L}",
    )(kk, jnp.asarray(cm[order], F32).astype(BF16), jnp.asarray(sm[order], F32).astype(BF16))


def _lower_bound(lb_ref, l):
    prm = lb_ref[...]
    e = jnp.exp(prm - jnp.max(prm, axis=0, keepdims=True))
    s = e / jnp.sum(e, axis=0, keepdims=True)
    acc = jnp.zeros_like(s[0:1])
    for i in range(1, l + 1):
        acc = acc + s[i:i + 1]
    return acc


def _forget_gate(z, lb):
    t = jnp.exp(-jnp.abs(z))
    one_t = 1.0 + t
    pos = z >= 0.0
    num = jnp.where(pos, 1.0 + lb * t, lb + t)
    log_f = jnp.where(num > 0.0, jnp.log(num), z) - jnp.log(one_t)
    return log_f, (1.0 - lb) * (jnp.where(pos, t, 1.0) / one_t)


def _pre_kernel(l, row_w, hg, hy, x_ref, mod_ref, n1_ref, win_ref, lbf_ref, lbb_ref, cw_ref,
                q_ref, kf_ref, kb_ref, lf_ref, lb_ref, v_ref, sg_ref, u_ref, x2_ref, h_scr):
    d = x_ref.shape[1]
    m = mod_ref[0]
    sh1, sc1 = m[:, 0:d], m[:, d:2 * d]
    h_scr[...] = (_rms(x_ref[...]) * n1_ref[0] * (1.0 + sc1) + sh1).astype(BF16)

    w = PROJ_CHUNK
    cw = cw_ref[0]
    lbs = (_lower_bound(lbf_ref, l), _lower_bound(lbb_ref, l))
    conv = lambda r, j, c: _dwconv3(r, cw[:, j * hy + c:j * hy + c + w], row_w)

    tasks = []
    for c in range(0, hg, w):
        cs = slice(c, c + w)

        def q_task(r, cs=cs):
            q_ref[:, cs] = (_silu(r[0]) * (HG_DK ** -0.5)).astype(q_ref.dtype)

        def gate_task(r, cs=cs, i=0):
            g_ref, k_ref = ((lf_ref, kf_ref), (lb_ref, kb_ref))[i]
            log_f, k = _forget_gate(r[0], lbs[i][:, cs])
            g_ref[:, cs], k_ref[:, cs] = log_f, k.astype(k_ref.dtype)

        def v_task(r, cs=cs):
            v_ref[:, cs] = r[0].astype(v_ref.dtype)

        def sg_task(r, cs=cs):
            sg_ref[:, cs] = _silu(r[0]).astype(sg_ref.dtype)

        def u_task(r, cs=cs, c=c):
            u_ref[:, cs] = (conv(r[1], 1, c) * conv(r[0], 0, c)).astype(u_ref.dtype)

        def x2_task(r, cs=cs, c=c):
            x2_ref[:, cs] = conv(r[0], 2, c).astype(x2_ref.dtype)

        tasks += [((c,), q_task), ((hg + c,), gate_task), ((2 * hg + c,), functools.partial(gate_task, i=1)),
                  ((3 * hg + c,), v_task), ((4 * hg + c,), sg_task),
                  ((5 * hg + c, 5 * hg + hy + c), u_task), ((5 * hg + 2 * hy + c,), x2_task)]

    issue = lambda offs: [jnp.dot(h_scr[...], win_ref[0, :, o:o + w], preferred_element_type=F32) for o in offs]
    pending = issue(tasks[0][0])
    for i, (_, consume) in enumerate(tasks):
        ahead = issue(tasks[i + 1][0]) if i + 1 < len(tasks) else None
        consume(pending)
        pending = ahead


def _pre(l, x, mod, mod_row, row_w, seq, tm, p):
    t, d = x.shape
    hg = p["hg_lb_fwd"].shape[1]
    hy = p["hy_d"].shape[1]
    n_in = p["w_in"].shape[2]
    depth = p["w_in"].shape[0]
    tok = lambda w: pl.BlockSpec((tm, w), lambda i: (i, 0))
    out = lambda dt: jax.ShapeDtypeStruct((t, hg), dt)
    return pl.pallas_call(
        functools.partial(_pre_kernel, l, row_w, hg, hy),
        grid=(t // tm,),
        in_specs=[
            tok(d),
            pl.BlockSpec((1, 1, mod.shape[2]), lambda i: (mod_row(l, i * tm // seq), 0, 0)),
            _const_spec((1, 1, d), lambda i: (l, 0, 0)),
            _const_spec((1, d, n_in), lambda i: (l, 0, 0)),
            _const_spec((depth, hg), lambda i: (0, 0)),
            _const_spec((depth, hg), lambda i: (0, 0)),
            _const_spec((1, 3, 3 * hy), lambda i: (l, 0, 0)),
        ],
        out_specs=[tok(hg)] * 9,
        out_shape=[out(BF16)] * 3 + [out(F32)] * 2 + [out(BF16)] * 4,
        scratch_shapes=[pltpu.VMEM((tm, d), BF16)],
        compiler_params=pltpu.CompilerParams(vmem_limit_bytes=VMEM_LIMIT),
        name="pre_mixer",
    )(x, mod, p["norm1_w3"], p["w_in_bf"], p["hg_lb_fwd"], p["hg_lb_bwd"], p["hy_conv_w"])


LEVELS = tuple(CHUNK >> (i + 1) for i in range(N_LEVELS))
SUBLANES = 8
MXU_LEVELS = tuple(s for s in LEVELS if 1 < s < SUBLANES)
LOG2E = 1.4426950408889634


def _scan_tables():
    c = CHUNK
    j = np.arange(c)[:, None]
    i = np.arange(c)[None, :]
    masks = [((j // (2 * s) == i // (2 * s)) & (j % (2 * s) >= s) & (i % (2 * s) < s)).astype(np.float32)
             for s in LEVELS]
    masks.append(np.eye(c, dtype=np.float32))
    mf = np.stack(masks)
    blocks = [np.tril(np.ones((c, c), np.float32))]
    for s in MXU_LEVELS:
        w = np.zeros((c, c), np.float32)
        for t in range(c):
            ref = (t // (2 * s)) * 2 * s + s - 1
            if t % (2 * s) >= s:
                w[t, ref + 1:t + 1] = 1.0
            else:
                w[t, t + 1:ref + 1] = 1.0
        blocks.append(w)
    fwd = np.concatenate(blocks, axis=0)
    bwd = np.concatenate([w[::-1, ::-1] for w in blocks], axis=0)
    sums = np.stack([np.concatenate([fwd] * 3, axis=1), np.concatenate([bwd] * 3, axis=1)])
    return sums, np.stack([mf, mf[:, ::-1, ::-1]])


def _decay_logs(sums, g, d):
    c = g.shape[0]
    fwd = d == 0
    b = sums[:c]
    out = []
    for s in LEVELS:
        if s >= SUBLANES:
            parts = []
            for b0 in range(0, c, 2 * s):
                ref = b0 + s - 1 if fwd else b0 + s
                br, lo, hi = b[ref:ref + 1], b[b0:b0 + s], b[b0 + s:b0 + 2 * s]
                parts += [br - lo, hi - br] if fwd else [lo - br, br - hi]
            out.append(jnp.concatenate(parts, axis=0))
        elif s in MXU_LEVELS:
            i = 1 + MXU_LEVELS.index(s)
            out.append(sums[i * c:(i + 1) * c])
        else:
            r = lax.broadcasted_iota(jnp.int32, (c, 1), 0)
            out.append(jnp.where((r & 1) == (1 if fwd else 0), g, 0.0))
    far = b[c - 1:c] if fwd else b[0:1]
    return out + [b, far - b]


def _scan_kernel(nc, heads, has_s0, want_state, *refs):
    q_ref, kf_ref, kb_ref, lf_ref, lb_ref, v_ref = refs[:6]
    refs = refs[6:]
    if has_s0:
        s0_ref, refs = refs[0], refs[1:]
    sum_ref, m_ref, o_ref = refs[:3]
    refs = refs[3:]
    if want_state:
        st_ref, refs = refs[0], refs[1:]
    s_scr, ob_scr, qs_scr, ks_scr, v_scr, ea_scr = refs
    c = CHUNK
    dirs = ((kf_ref, lf_ref), (kb_ref, lb_ref))
    pairs = [(d, h) for d in range(2) for h in range(heads)]
    cols = lambda h: slice(h * HG_DK, (h + 1) * HG_DK)

    def rows_of(ci, d):
        return pl.ds((ci if d == 0 else nc - 1 - ci) * c, c)

    def prep_start(ci):
        out = []
        for d, (k_ref, g_ref) in enumerate(dirs):
            rows = rows_of(ci, d)
            g = g_ref[rows, :] * LOG2E
            g1 = g.astype(BF16)
            r1 = g - g1.astype(F32)
            g2 = r1.astype(BF16)
            g3 = (r1 - g2.astype(F32)).astype(BF16)
            sums = jnp.dot(sum_ref[d], jnp.concatenate([g1, g2, g3], axis=0), preferred_element_type=F32)
            out.append((g, q_ref[rows, :], k_ref[rows, :], v_ref[rows, :], sums))
        return out

    def prep_finish(slot, prepped):
        for d, (g, q, k, v, sums) in enumerate(prepped):
            e = [jnp.exp2(x) for x in _decay_logs(sums, g, d)]
            eb = [x.astype(BF16) for x in e]
            qb, kb = q.astype(BF16), k.astype(BF16)
            for s in range(N_LEVELS):
                qs_scr[slot, d, s] = qb * eb[s]
                ks_scr[slot, d, s] = kb * eb[s]
            qs_scr[slot, d, N_LEVELS] = qb
            ks_scr[slot, d, N_LEVELS] = kb
            qs_scr[slot, d, N_LEVELS + 1] = qb * eb[N_LEVELS]
            ks_scr[slot, d, N_LEVELS + 1] = kb * eb[N_LEVELS + 1]
            v_scr[slot, d] = v.astype(BF16)
            ea_scr[slot, d, 0:1, :] = e[N_LEVELS][c - 1:c] if d == 0 else e[N_LEVELS][0:1]

    def use_start(slot):
        scores, inter, new_state, vals = {}, {}, {}, {}
        for d, h in pairs:
            scores[d, h] = [_bdot_nt(qs_scr[slot, d, s, :, cols(h)], ks_scr[slot, d, s, :, cols(h)])
                            for s in range(N_LEVELS + 1)]
            st = s_scr[d, h]
            vals[d, h] = v_scr[slot, d, :, cols(h)]
            inter[d, h] = _bdot_nt(qs_scr[slot, d, N_LEVELS + 1, :, cols(h)], st)
            new_state[d, h] = (ea_scr[slot, d, 0:1, cols(h)] * st
                               + _bdot_tn(vals[d, h], ks_scr[slot, d, N_LEVELS + 1, :, cols(h)]))
        return scores, inter, new_state, vals

    def use_finish(ci, scores, inter, new_state, vals):
        for d in range(2):
            keep = [m_ref[d, s] > 0.5 for s in range(N_LEVELS + 1)]
            for h in range(heads):
                a = jnp.where(keep[N_LEVELS], scores[d, h][N_LEVELS], 0.0)
                for s in range(N_LEVELS):
                    a = jnp.where(keep[s], scores[d, h][s], a)
                o = _bdot(a, vals[d, h]) + inter[d, h]
                (o_ref if d == 0 else ob_scr)[rows_of(ci, d), cols(h)] = o
                s_scr[d, h] = new_state[d, h]

    for d, h in pairs:
        s_scr[d, h] = s0_ref[0, 0, d, h].T if has_s0 else jnp.zeros(s_scr.shape[2:], F32)

    prep_finish(0, prep_start(0))
    for ci in range(nc):
        slot = ci % 2
        prepped = prep_start(ci + 1) if ci + 1 < nc else None
        part = use_start(slot)
        if prepped is not None:
            prep_finish(1 - slot, prepped)
        use_finish(ci, *part)
    o_ref[...] += ob_scr[...]
    if want_state:
        for d in range(2):
            for h in range(heads):
                st_ref[0, d, h] = s_scr[d, h].T


def _scan(l, seq, nseq, arrs, s0, want_state):
    t, hg = arrs[0].shape
    heads = hg // HG_DK
    nc = seq // CHUNK
    sums, masks = _scan_tables()
    sums, masks = jnp.asarray(sums, F32).astype(BF16), jnp.asarray(masks)
    seq_spec = pl.BlockSpec((seq, hg), lambda b: (b, 0))
    in_specs = [seq_spec] * 6
    args = list(arrs)
    if s0 is not None:
        in_specs.append(pl.BlockSpec((1, 1) + s0.shape[2:], lambda b: (b, l, 0, 0, 0, 0)))
        args.append(s0)
    in_specs += [_const_spec(sums.shape, lambda b: (0, 0, 0)), _const_spec(masks.shape, lambda b: (0, 0, 0, 0))]
    args += [sums, masks]
    out_specs = [seq_spec]
    out_shape = [jax.ShapeDtypeStruct((t, hg), F32)]
    if want_state:
        out_specs.append(pl.BlockSpec((1, 2, heads, HG_DK, HG_DK), lambda b: (b, 0, 0, 0, 0)))
        out_shape.append(jax.ShapeDtypeStruct((nseq, 2, heads, HG_DK, HG_DK), F32))
    res = pl.pallas_call(
        functools.partial(_scan_kernel, nc, heads, s0 is not None, want_state),
        grid=(nseq,),
        in_specs=in_specs,
        out_specs=out_specs,
        out_shape=out_shape,
        scratch_shapes=[pltpu.VMEM((2, heads, HG_DK, HG_DK), F32), pltpu.VMEM((seq, hg), F32),
                        pltpu.VMEM((2, 2, N_LEVELS + 2, CHUNK, hg), BF16),
                        pltpu.VMEM((2, 2, N_LEVELS + 2, CHUNK, hg), BF16),
                        pltpu.VMEM((2, 2, CHUNK, hg), BF16), pltpu.VMEM((2, 2, SUBLANES, hg), F32)],
        compiler_params=pltpu.CompilerParams(vmem_limit_bytes=VMEM_LIMIT),
        name="hgrn2_scan",
    )(*args)
    return res if want_state else (res[0], None)


def _hyena_kernel(L, u_ref, x2_ref, kr_ref, ki_ref, d_ref, nw_ref, fe_ref, fo_ref, ie_ref, io_ref, y_ref,
                  u_scr, z_scr):
    h = L // 2
    groups = range(u_scr.shape[0])
    lanes = lambda j: slice(j * LANES, (j + 1) * LANES)
    u = u_ref[...].astype(F32)
    for j in groups:
        u_scr[j] = u[:, lanes(j)]
    parity = lambda par: jnp.concatenate([u_scr[j, pl.ds(par, h, stride=2), :] for j in groups], axis=1)
    ue, uo = parity(0).astype(BF16), parity(1).astype(BF16)
    p = jnp.dot(fe_ref[...], ue, preferred_element_type=F32)
    q = jnp.dot(fo_ref[...], uo, preferred_element_type=F32)
    ur = jnp.concatenate([p[:h] + q[:h], p[:h] - q[:h]], axis=0)
    us = jnp.concatenate([p[h:] + q[h:], q[h:] - p[h:]], axis=0)
    kr, ki = kr_ref[0], ki_ref[0]
    zr = ur * kr + us * ki
    zi = ur * ki - us * kr
    even = jnp.concatenate([zr[:h] + zr[h:], zi[:h] - zi[h:]], axis=0).astype(BF16)
    odd = jnp.concatenate([zr[:h] - zr[h:], zi[:h] + zi[h:]], axis=0).astype(BF16)
    for par, (tab_ref, zz) in enumerate(((ie_ref, even), (io_ref, odd))):
        zp = jnp.dot(tab_ref[...], zz, preferred_element_type=F32)
        for j in groups:
            z_scr[j, pl.ds(par, h, stride=2), :] = zp[:, lanes(j)]
    z = jnp.concatenate([z_scr[j] for j in groups], axis=1)
    y = x2_ref[...] * (z + d_ref[0] * u)
    y_ref[...] = (_rms(y) * nw_ref[0]).astype(y_ref.dtype)


def _hyena(l, seq, nseq, u, x2, kr, ki, p):
    t, hw = u.shape
    h = seq // 2
    cm, sm = _dft_tables(seq)
    tabs = []
    for par in range(2):
        c, s = cm[:h, par::2], sm[:h, par::2]
        tabs.append((np.concatenate([c, s], axis=0), np.concatenate([c.T, -s.T], axis=1) / seq))
    bf = lambda a: jnp.asarray(a, F32).astype(BF16)
    fe, fo, ie, io = bf(tabs[0][0]), bf(tabs[1][0]), bf(tabs[0][1]), bf(tabs[1][1])
    seq_spec = pl.BlockSpec((seq, hw), lambda b: (b, 0))
    lay = lambda shape: _const_spec((1,) + shape, lambda b: (l, 0, 0))
    tab = lambda a: _const_spec(a.shape, lambda b: (0, 0))
    return pl.pallas_call(
        functools.partial(_hyena_kernel, seq),
        grid=(nseq,),
        in_specs=[seq_spec, seq_spec, lay((seq, hw)), lay((seq, hw)), lay((1, hw)), lay((1, hw)),
                  tab(fe), tab(fo), tab(ie), tab(io)],
        out_specs=seq_spec,
        out_shape=jax.ShapeDtypeStruct((t, hw), BF16),
        scratch_shapes=[pltpu.VMEM((hw // LANES, seq, LANES), F32)] * 2,
        compiler_params=pltpu.CompilerParams(vmem_limit_bytes=VMEM_LIMIT),
        name="hyena_conv",
    )(u, x2, kr, ki, p["hy_d3"], p["hy_norm_w3"], fe, fo, ie, io)


def _post_kernel(row_w, heads, ff_chunk, last, x_ref, o_ref, sg_ref, y_ref, mod_ref, hn_ref, wout_ref,
                 n2_ref, wup_ref, cw_ref, wdn_ref, fn_ref, out_ref, x_scr, h2_scr, act_scr):
    d = x_ref.shape[1]
    hg = o_ref.shape[1]
    dff = wdn_ref.shape[1]
    m = mod_ref[0]
    g1, sh2, sc2, g2 = (m[:, i * d:(i + 1) * d] for i in range(2, 6))
    hn = hn_ref[0]
    mix = _bdot(y_ref[...], wout_ref[0, hg:, :])
    for h in range(heads):
        cols = slice(h * HG_DK, (h + 1) * HG_DK)
        mix = mix + _bdot(_rms(o_ref[:, cols]) * hn[:, cols] * sg_ref[:, cols], wout_ref[0, cols, :])
    x_scr[...] = x_ref[...] + g1 * mix
    h2_scr[...] = (_rms(x_scr[...]) * n2_ref[0] * (1.0 + sc2) + sh2).astype(BF16)
    cw = cw_ref[0]

    for j in range(dff // ff_chunk):
        a, b = j * ff_chunk, (j + 1) * ff_chunk
        gate = _dwconv3(jnp.dot(h2_scr[...], wup_ref[0, :, a:b], preferred_element_type=F32),
                        cw[:, a:b], row_w)
        val = _dwconv3(jnp.dot(h2_scr[...], wup_ref[0, :, dff + a:dff + b], preferred_element_type=F32),
                       cw[:, dff + a:dff + b], row_w)
        act_scr[:, a:b] = (_silu(gate) * val).astype(BF16)
    x = x_scr[...] + g2 * jnp.dot(act_scr[...], wdn_ref[0], preferred_element_type=F32)
    out_ref[...] = _rms(x) * fn_ref[...] if last else x


def _post(l, x, o, sg, y, mod, mod_row, row_w, seq, tm, last, p):
    t, d = x.shape
    hg = o.shape[1]
    dff = p["ffn_w_down"].shape[1]
    tok = lambda w: pl.BlockSpec((tm, w), lambda i: (i, 0))
    lay = lambda shape: _const_spec((1,) + shape, lambda i: (l, 0, 0))
    return pl.pallas_call(
        functools.partial(_post_kernel, row_w, hg // HG_DK, 256, last),
        grid=(t // tm,),
        in_specs=[
            tok(d), tok(hg), tok(hg), tok(y.shape[1]),
            pl.BlockSpec((1, 1, mod.shape[2]), lambda i: (mod_row(l, i * tm // seq), 0, 0)),
            lay((1, hg)), lay((d, d)), lay((1, d)), lay((d, 2 * dff)), lay((3, 2 * dff)), lay((dff, d)),
            _const_spec((1, d), lambda i: (0, 0)),
        ],
        out_specs=tok(d),
        out_shape=jax.ShapeDtypeStruct((t, d), F32),
        scratch_shapes=[pltpu.VMEM((tm, d), F32), pltpu.VMEM((tm, d), BF16), pltpu.VMEM((tm, dff), BF16)],
        compiler_params=pltpu.CompilerParams(vmem_limit_bytes=VMEM_LIMIT),
        name="post_mixer_ffn",
    )(x, o, sg, y, mod, p["hg_norm_w3"], p["w_out_bf"], p["norm2_w3"], p["ffn_w_up_bf"],
      p["ffn_conv_w"], p["ffn_w_down_bf"], p["final_norm_w2"])


def _group(x3, mod, mod_row, row_w, s0, want_state, filt, p):
    nseq, seq, d = x3.shape
    depth = p["w_in"].shape[0]
    x = x3.reshape(nseq * seq, d)
    kr, ki = filt
    states = []
    for l in range(depth):
        q, kf, kb, lf, lb, v, sg, u, x2 = _pre(l, x, mod, mod_row, row_w, seq, PRE_TILE, p)
        o, st = _scan(l, seq, nseq, (q, kf, kb, lf, lb, v), s0, want_state)
        y = _hyena(l, seq, nseq, u, x2, kr, ki, p)
        x = _post(l, x, o, sg, y, mod, mod_row, row_w, seq, POST_TILE, l == depth - 1, p)
        states.append(st)
    return x.reshape(nseq, seq, d), states


def kernel(x_prompt, x_sample, state_hgrn, c, c_ctx, w_in, w_out, ada_w, ada_b, norm1_w, norm2_w,
           hg_lb_fwd, hg_lb_bwd, hg_norm_w, hy_conv_w, hy_w1, hy_b1, hy_freq1, hy_w2, hy_b2, hy_freq2,
           hy_w3, hy_d, hy_norm_w, ffn_w_up, ffn_conv_w, ffn_w_down, final_norm_w):
    depth, d, _ = w_in.shape
    vec3 = lambda a: a.reshape(a.shape[0], 1, a.shape[1])
    p = dict(w_in=w_in, w_in_bf=w_in.astype(BF16), w_out_bf=w_out.astype(BF16),
             ffn_w_up_bf=ffn_w_up.astype(BF16), ffn_w_down=ffn_w_down, ffn_w_down_bf=ffn_w_down.astype(BF16),
             norm1_w3=vec3(norm1_w), norm2_w3=vec3(norm2_w), hg_norm_w3=vec3(hg_norm_w),
             hy_d=hy_d, hy_d3=vec3(hy_d), hy_norm_w3=vec3(hy_norm_w), final_norm_w2=final_norm_w[None, :],
             hg_lb_fwd=hg_lb_fwd, hg_lb_bwd=hg_lb_bwd, hy_conv_w=hy_conv_w, ffn_conv_w=ffn_conv_w,
             hy_w1=hy_w1, hy_b1=hy_b1, hy_freq1=hy_freq1, hy_w2=hy_w2, hy_b2=hy_b2, hy_freq2=hy_freq2,
             hy_w3=hy_w3)

    n_dec = c.shape[0]
    cv = jnp.concatenate([c, c_ctx[None, :], jnp.zeros((MOD_ROWS - n_dec - 1, d), F32)], axis=0)
    mod = _modulation(cv, ada_w, ada_b).reshape(depth * MOD_ROWS, 1, 6 * d)

    seq_p, seq_s = x_prompt.shape[1], x_sample.shape[1]
    y_p, states = _group(x_prompt, mod, lambda l, b: l * MOD_ROWS + n_dec, seq_p, None, True,
                         _filters(seq_p, p), p)
    y_s, _ = _group(x_sample, mod, lambda l, b: l * MOD_ROWS + b, GRID_W, state_hgrn, False,
                    _filters(seq_s, p), p)
    return (y_p, y_s, jnp.stack(states, axis=1))
```

```python
import functools
import math

import numpy as np
import jax
import jax.numpy as jnp
from jax import lax
from jax.experimental import pallas as pl
from jax.experimental.pallas import tpu as pltpu

F32 = jnp.float32
BF16 = jnp.bfloat16

HG_DK = 128
CHUNK = 64
GRID_W = 64
HY_BANDS = 16
HY_TARGET = 1e-2
HY_FAST = 0.3
HY_SLOW = 1.5
EPS = 1e-6

LANES = 128
VMEM_LIMIT = 56 * 1024 * 1024

N_LEVELS = int(math.log2(CHUNK))
MOD_ROWS = 16
PRE_TILE = 1024
HYENA_ROWS = 1024
POST_TILE = 512
PROJ_CHUNK = 256


def _bdot(a, b):
    return jnp.dot(a.astype(BF16), b.astype(BF16), preferred_element_type=F32)


def _bdot_nt(a, b):
    return lax.dot_general(a.astype(BF16), b.astype(BF16), (((1,), (1,)), ((), ())),
                           preferred_element_type=F32)


def _bdot_tn(a, b):
    return lax.dot_general(a.astype(BF16), b.astype(BF16), (((0,), (0,)), ((), ())),
                           preferred_element_type=F32)


def _hdot(a, b):
    return jnp.dot(a, b, preferred_element_type=F32, precision=lax.Precision.HIGHEST)


def _silu(x):
    return x * jax.nn.sigmoid(x)


def _rms(x):
    return x * lax.rsqrt(jnp.mean(x * x, axis=-1, keepdims=True) + EPS)


def _const_spec(shape, index_map):
    return pl.BlockSpec(shape, index_map, pipeline_mode=pl.Buffered(1))


def _dwconv3(p, w, row_w):
    n = p.shape[0]
    r = lax.broadcasted_iota(jnp.int32, (n, 1), 0) & (row_w - 1)
    prev = jnp.where(r == 0, 0.0, pltpu.roll(p, 1, 0))
    nxt = jnp.where(r == row_w - 1, 0.0, pltpu.roll(p, n - 1, 0))
    return w[0:1] * prev + w[1:2] * p + w[2:3] * nxt


def _mod_kernel(cv_ref, w_ref, b_ref, o_ref):
    s = _silu(cv_ref[...])
    w = w_ref[0]
    s_hi = s.astype(BF16)
    s_lo = (s - s_hi.astype(F32)).astype(BF16)
    w_hi = w.astype(BF16)
    w_lo = (w - w_hi.astype(F32)).astype(BF16)
    rows = s.shape[0]
    hi = jnp.dot(jnp.concatenate([s_hi, s_lo], axis=0), w_hi, preferred_element_type=F32)
    o_ref[0] = hi[:rows] + hi[rows:] + jnp.dot(s_hi, w_lo, preferred_element_type=F32) + b_ref[0]


def _modulation(cv, ada_w, ada_b):
    depth, d, n = ada_w.shape
    tn = 1536
    return pl.pallas_call(
        _mod_kernel,
        grid=(depth, n // tn),
        in_specs=[
            pl.BlockSpec((MOD_ROWS, d), lambda l, j: (0, 0)),
            pl.BlockSpec((1, d, tn), lambda l, j: (l, 0, j)),
            pl.BlockSpec((1, 1, tn), lambda l, j: (l, 0, j)),
        ],
        out_specs=pl.BlockSpec((1, MOD_ROWS, tn), lambda l, j: (l, 0, j)),
        out_shape=jax.ShapeDtypeStruct((depth, MOD_ROWS, n), F32),
        compiler_params=pltpu.CompilerParams(vmem_limit_bytes=VMEM_LIMIT),
        name="modulation",
    )(cv, ada_w, ada_b.reshape(depth, 1, n))


def _dft_tables(L):
    f = np.arange(L, dtype=np.int64)[:, None]
    t = np.arange(L, dtype=np.int64)[None, :]
    ang = 2.0 * np.pi * (((2 * f + 1) * t) % (4 * L)).astype(np.float64) / (4 * L)
    return np.cos(ang), np.sin(ang)


def _paired_freq_order(L):
    half = np.arange(L // 2)
    return np.concatenate([half, L - 1 - half])


def _filter_feats(L):
    half = LANES // 2

    def feats(pos):
        t = pos / (L - 1)
        bands = np.linspace(1e-4, HY_BANDS - 1, HY_BANDS)[None, :]
        ang = 2.0 * np.pi * pos[:, None] * bands / L
        out = np.zeros((pos.shape[0], half), np.float64)
        out[:, 0] = t
        out[:, 1:1 + HY_BANDS] = np.cos(ang)
        out[:, 1 + HY_BANDS:1 + 2 * HY_BANDS] = np.sin(ang)
        return out
    pos = np.arange(L, dtype=np.float64)
    return np.concatenate([feats(pos), feats(L - pos)], axis=1).astype(np.float32)


def _filter_taps_kernel(L, hw, ft_ref, w1_ref, b1_ref, f1_ref, w2_ref, b2_ref, f2_ref, w3_ref,
                        dl_ref, kk_ref):
    half = LANES // 2
    ft = ft_ref[...]
    h = jnp.sin(f1_ref[0] * (_hdot(ft, w1_ref[0]) + b1_ref[0]))
    h = jnp.sin(f2_ref[0] * (_hdot(h, w2_ref[0]) + b2_ref[0]))
    h = _hdot(h, w3_ref[0])
    dl = dl_ref[...]
    k1 = h[:, :hw] * jnp.exp(-ft[:, 0:1] * dl)
    row = lax.broadcasted_iota(jnp.int32, (L, 1), 0)
    k2 = jnp.where(row == 0, 0.0, -(h[:, hw:] * jnp.exp(-ft[:, half:half + 1] * dl)))
    norm = jnp.sum(jnp.abs(k1), axis=0, keepdims=True) + jnp.sum(jnp.abs(k2), axis=0, keepdims=True) + EPS
    kk_ref[0, :, :hw] = (k1 / norm).astype(BF16)
    kk_ref[0, :, hw:] = (k2 / norm).astype(BF16)


def _filter_dft_kernel(hw, kk_ref, cm_ref, sm_ref, kr_ref, ki_ref):
    kk = kk_ref[0]
    gc = jnp.dot(cm_ref[...], kk, preferred_element_type=F32)
    gs = jnp.dot(sm_ref[...], kk, preferred_element_type=F32)
    tf = gc.shape[0]
    row = lax.broadcasted_iota(jnp.int32, (tf, 1), 0)
    second = pl.program_id(1) * tf + row >= kk.shape[0] // 2
    alt = jnp.where(((row & 1) == 0) != second, 1.0, -1.0)
    kr_ref[0] = gc[:, :hw] - alt * gs[:, hw:]
    ki_ref[0] = -(gs[:, :hw] + alt * gc[:, hw:])


def _filters(L, p):
    depth, _, hw2 = p["hy_w3"].shape
    hw = hw2 // 2
    half = LANES // 2
    cm, sm = _dft_tables(L)
    order = _paired_freq_order(L)
    ft = jnp.asarray(_filter_feats(L))
    min_decay = math.log(HY_TARGET) / HY_SLOW
    max_decay = math.log(HY_TARGET) / HY_FAST
    deltas = jnp.asarray(np.abs(np.linspace(min_decay, max_decay, hw)).astype(np.float32)[None, :])

    def pad_to(a, rows, cols):
        return jnp.pad(a, ((0, 0), (0, rows - a.shape[1]), (0, cols - a.shape[2])))

    def block_diag(a, b):
        za = jnp.zeros(a.shape[:2] + (b.shape[2],), F32)
        zb = jnp.zeros(b.shape[:2] + (a.shape[2],), F32)
        return jnp.concatenate([jnp.concatenate([a, za], axis=2), jnp.concatenate([zb, b], axis=2)], axis=1)

    w1 = pad_to(p["hy_w1"], half, half)
    w2 = pad_to(p["hy_w2"], half, half)
    w3 = pad_to(p["hy_w3"], half, hw2)
    w1, w2, w3 = block_diag(w1, w1), block_diag(w2, w2), block_diag(w3[:, :, :hw], w3[:, :, hw:])
    vec = lambda a: jnp.tile(pad_to(a[:, None, :], 1, half), (1, 1, 2))
    lay = lambda shape: pl.BlockSpec((1,) + shape, lambda l: (l, 0, 0))
    kk = pl.pallas_call(
        functools.partial(_filter_taps_kernel, L, hw),
        grid=(depth,),
        in_specs=[
            _const_spec((L, LANES), lambda l: (0, 0)),
            lay((LANES, LANES)), lay((1, LANES)), lay((1, LANES)),
            lay((LANES, LANES)), lay((1, LANES)), lay((1, LANES)),
            lay((LANES, hw2)),
            _const_spec((1, hw), lambda l: (0, 0)),
        ],
        out_specs=lay((L, hw2)),
        out_shape=jax.ShapeDtypeStruct((depth, L, hw2), BF16),
        compiler_params=pltpu.CompilerParams(vmem_limit_bytes=VMEM_LIMIT),
        name=f"hyena_filter_taps_{L}",
    )(ft, w1, vec(p["hy_b1"]), vec(p["hy_freq1"]), w2, vec(p["hy_b2"]), vec(p["hy_freq2"]), w3, deltas)
    tf = 256
    frq = pl.BlockSpec((tf, L), lambda l, j: (j, 0))
    out = pl.BlockSpec((1, tf, hw), lambda l, j: (l, j, 0))
    return pl.pallas_call(
        functools.partial(_filter_dft_kernel, hw),
        grid=(depth, L // tf),
        in_specs=[pl.BlockSpec((1, L, hw2), lambda l, j: (l, 0, 0)), frq, frq],
        out_specs=[out, out],
        out_shape=[jax.ShapeDtypeStruct((depth, L, hw), F32)] * 2,
        compiler_params=pltpu.CompilerParams(vmem_limit_bytes=VMEM_LIMIT),
        name=f"hyena_filter_dft_{L}",
    )(kk, jnp.asarray(cm[order], F32).astype(BF16), jnp.asarray(sm[order], F32).astype(BF16))


def _lower_bound(lb_ref, l):
    prm = lb_ref[...]
    e = jnp.exp(prm - jnp.max(prm, axis=0, keepdims=True))
    s = e / jnp.sum(e, axis=0, keepdims=True)
    acc = jnp.zeros_like(s[0:1])
    for i in range(1, l + 1):
        acc = acc + s[i:i + 1]
    return acc


def _forget_gate(z, lb):
    t = jnp.exp(-jnp.abs(z))
    one_t = 1.0 + t
    pos = z >= 0.0
    num = jnp.where(pos, 1.0 + lb * t, lb + t)
    log_f = jnp.where(num > 0.0, jnp.log(num), z) - jnp.log(one_t)
    return log_f, (1.0 - lb) * (jnp.where(pos, t, 1.0) / one_t)


def _pre_kernel(l, row_w, hg, hy, x_ref, mod_ref, n1_ref, win_ref, lbf_ref, lbb_ref, cw_ref,
                q_ref, kf_ref, kb_ref, lf_ref, lb_ref, v_ref, sg_ref, u_ref, x2_ref, h_scr):
    d = x_ref.shape[1]
    m = mod_ref[0]
    sh1, sc1 = m[:, 0:d], m[:, d:2 * d]
    h_scr[...] = (_rms(x_ref[...]) * n1_ref[0] * (1.0 + sc1) + sh1).astype(BF16)

    w = PROJ_CHUNK
    cw = cw_ref[0]
    lbs = (_lower_bound(lbf_ref, l), _lower_bound(lbb_ref, l))
    conv = lambda r, j, c: _dwconv3(r, cw[:, j * hy + c:j * hy + c + w], row_w)

    tasks = []
    for c in range(0, hg, w):
        cs = slice(c, c + w)

        def q_task(r, cs=cs):
            q_ref[:, cs] = (_silu(r[0]) * (HG_DK ** -0.5)).astype(q_ref.dtype)

        def gate_task(r, cs=cs, i=0):
            g_ref, k_ref = ((lf_ref, kf_ref), (lb_ref, kb_ref))[i]
            log_f, k = _forget_gate(r[0], lbs[i][:, cs])
            g_ref[:, cs], k_ref[:, cs] = log_f, k.astype(k_ref.dtype)

        def v_task(r, cs=cs):
            v_ref[:, cs] = r[0].astype(v_ref.dtype)

        def sg_task(r, cs=cs):
            sg_ref[:, cs] = _silu(r[0]).astype(sg_ref.dtype)

        def u_task(r, cs=cs, c=c):
            u_ref[:, cs] = (conv(r[1], 1, c) * conv(r[0], 0, c)).astype(u_ref.dtype)

        def x2_task(r, cs=cs, c=c):
            x2_ref[:, cs] = conv(r[0], 2, c).astype(x2_ref.dtype)

        tasks += [((c,), q_task), ((hg + c,), gate_task), ((2 * hg + c,), functools.partial(gate_task, i=1)),
                  ((3 * hg + c,), v_task), ((4 * hg + c,), sg_task),
                  ((5 * hg + c, 5 * hg + hy + c), u_task), ((5 * hg + 2 * hy + c,), x2_task)]

    issue = lambda offs: [jnp.dot(h_scr[...], win_ref[0, :, o:o + w], preferred_element_type=F32) for o in offs]
    pending = issue(tasks[0][0])
    for i, (_, consume) in enumerate(tasks):
        ahead = issue(tasks[i + 1][0]) if i + 1 < len(tasks) else None
        consume(pending)
        pending = ahead


def _pre(l, x, mod, mod_row, row_w, seq, tm, p):
    t, d = x.shape
    hg = p["hg_lb_fwd"].shape[1]
    hy = p["hy_d"].shape[1]
    n_in = p["w_in"].shape[2]
    depth = p["w_in"].shape[0]
    tok = lambda w: pl.BlockSpec((tm, w), lambda i: (i, 0))
    out = lambda dt: jax.ShapeDtypeStruct((t, hg), dt)
    return pl.pallas_call(
        functools.partial(_pre_kernel, l, row_w, hg, hy),
        grid=(t // tm,),
        in_specs=[
            tok(d),
            pl.BlockSpec((1, 1, mod.shape[2]), lambda i: (mod_row(l, i * tm // seq), 0, 0)),
            _const_spec((1, 1, d), lambda i: (l, 0, 0)),
            _const_spec((1, d, n_in), lambda i: (l, 0, 0)),
            _const_spec((depth, hg), lambda i: (0, 0)),
            _const_spec((depth, hg), lambda i: (0, 0)),
            _const_spec((1, 3, 3 * hy), lambda i: (l, 0, 0)),
        ],
        out_specs=[tok(hg)] * 9,
        out_shape=[out(BF16)] * 3 + [out(F32)] * 2 + [out(BF16)] * 4,
        scratch_shapes=[pltpu.VMEM((tm, d), BF16)],
        compiler_params=pltpu.CompilerParams(vmem_limit_bytes=VMEM_LIMIT),
        name="pre_mixer",
    )(x, mod, p["norm1_w3"], p["w_in_bf"], p["hg_lb_fwd"], p["hg_lb_bwd"], p["hy_conv_w"])


LEVELS = tuple(CHUNK >> (i + 1) for i in range(N_LEVELS))
SUBLANES = 8
MXU_LEVELS = tuple(s for s in LEVELS if 1 < s < SUBLANES)
LOG2E = 1.4426950408889634


def _scan_tables():
    c = CHUNK
    j = np.arange(c)[:, None]
    i = np.arange(c)[None, :]
    masks = [((j // (2 * s) == i // (2 * s)) & (j % (2 * s) >= s) & (i % (2 * s) < s)).astype(np.float32)
             for s in LEVELS]
    masks.append(np.eye(c, dtype=np.float32))
    mf = np.stack(masks)
    blocks = [np.tril(np.ones((c, c), np.float32))]
    for s in MXU_LEVELS:
        w = np.zeros((c, c), np.float32)
        for t in range(c):
            ref = (t // (2 * s)) * 2 * s + s - 1
            if t % (2 * s) >= s:
                w[t, ref + 1:t + 1] = 1.0
            else:
                w[t, t + 1:ref + 1] = 1.0
        blocks.append(w)
    fwd = np.concatenate(blocks, axis=0)
    bwd = np.concatenate([w[::-1, ::-1] for w in blocks], axis=0)
    sums = np.stack([np.concatenate([fwd] * 3, axis=1), np.concatenate([bwd] * 3, axis=1)])
    return sums, np.stack([mf, mf[:, ::-1, ::-1]])


def _decay_logs(sums, g, d):
    c = g.shape[0]
    fwd = d == 0
    b = sums[:c]
    out = []
    for s in LEVELS:
        if s >= SUBLANES:
            parts = []
            for b0 in range(0, c, 2 * s):
                ref = b0 + s - 1 if fwd else b0 + s
                br, lo, hi = b[ref:ref + 1], b[b0:b0 + s], b[b0 + s:b0 + 2 * s]
                parts += [br - lo, hi - br] if fwd else [lo - br, br - hi]
            out.append(jnp.concatenate(parts, axis=0))
        elif s in MXU_LEVELS:
            i = 1 + MXU_LEVELS.index(s)
            out.append(sums[i * c:(i + 1) * c])
        else:
            r = lax.broadcasted_iota(jnp.int32, (c, 1), 0)
            out.append(jnp.where((r & 1) == (1 if fwd else 0), g, 0.0))
    far = b[c - 1:c] if fwd else b[0:1]
    return out + [b, far - b]


def _scan_kernel(nc, heads, has_s0, want_state, *refs):
    q_ref, kf_ref, kb_ref, lf_ref, lb_ref, v_ref = refs[:6]
    refs = refs[6:]
    if has_s0:
        s0_ref, refs = refs[0], refs[1:]
    sum_ref, m_ref, o_ref = refs[:3]
    refs = refs[3:]
    if want_state:
        st_ref, refs = refs[0], refs[1:]
    s_scr, ob_scr, qs_scr, ks_scr, v_scr, ea_scr = refs
    c = CHUNK
    dirs = ((kf_ref, lf_ref), (kb_ref, lb_ref))
    pairs = [(d, h) for d in range(2) for h in range(heads)]
    cols = lambda h: slice(h * HG_DK, (h + 1) * HG_DK)

    def rows_of(ci, d):
        return pl.ds((ci if d == 0 else nc - 1 - ci) * c, c)

    def prep_start(ci):
        out = []
        for d, (k_ref, g_ref) in enumerate(dirs):
            rows = rows_of(ci, d)
            g = g_ref[rows, :] * LOG2E
            g1 = g.astype(BF16)
            r1 = g - g1.astype(F32)
            g2 = r1.astype(BF16)
            g3 = (r1 - g2.astype(F32)).astype(BF16)
            sums = jnp.dot(sum_ref[d], jnp.concatenate([g1, g2, g3], axis=0), preferred_element_type=F32)
            out.append((g, q_ref[rows, :], k_ref[rows, :], v_ref[rows, :], sums))
        return out

    def prep_finish(slot, prepped):
        for d, (g, q, k, v, sums) in enumerate(prepped):
            e = [jnp.exp2(x) for x in _decay_logs(sums, g, d)]
            eb = [x.astype(BF16) for x in e]
            qb, kb = q.astype(BF16), k.astype(BF16)
            for s in range(N_LEVELS):
                qs_scr[slot, d, s] = qb * eb[s]
                ks_scr[slot, d, s] = kb * eb[s]
            qs_scr[slot, d, N_LEVELS] = qb
            ks_scr[slot, d, N_LEVELS] = kb
            qs_scr[slot, d, N_LEVELS + 1] = qb * eb[N_LEVELS]
            ks_scr[slot, d, N_LEVELS + 1] = kb * eb[N_LEVELS + 1]
            v_scr[slot, d] = v.astype(BF16)
            ea_scr[slot, d, 0:1, :] = e[N_LEVELS][c - 1:c] if d == 0 else e[N_LEVELS][0:1]

    def use_start(slot):
        scores, inter, new_state, vals = {}, {}, {}, {}
        for d, h in pairs:
            scores[d, h] = [_bdot_nt(qs_scr[slot, d, s, :, cols(h)], ks_scr[slot, d, s, :, cols(h)])
                            for s in range(N_LEVELS + 1)]
            st = s_scr[d, h]
            vals[d, h] = v_scr[slot, d, :, cols(h)]
            inter[d, h] = _bdot_nt(qs_scr[slot, d, N_LEVELS + 1, :, cols(h)], st)
            new_state[d, h] = (ea_scr[slot, d, 0:1, cols(h)] * st
                               + _bdot_tn(vals[d, h], ks_scr[slot, d, N_LEVELS + 1, :, cols(h)]))
        return scores, inter, new_state, vals

    def use_finish(ci, scores, inter, new_state, vals):
        for d in range(2):
            keep = [m_ref[d, s] > 0.5 for s in range(N_LEVELS + 1)]
            for h in range(heads):
                a = jnp.where(keep[N_LEVELS], scores[d, h][N_LEVELS], 0.0)
                for s in range(N_LEVELS):
                    a = jnp.where(keep[s], scores[d, h][s], a)
                o = _bdot(a, vals[d, h]) + inter[d, h]
                (o_ref if d == 0 else ob_scr)[rows_of(ci, d), cols(h)] = o
                s_scr[d, h] = new_state[d, h]

    for d, h in pairs:
        s_scr[d, h] = s0_ref[0, 0, d, h].T if has_s0 else jnp.zeros(s_scr.shape[2:], F32)

    prep_finish(0, prep_start(0))
    for ci in range(nc):
        slot = ci % 2
        prepped = prep_start(ci + 1) if ci + 1 < nc else None
        part = use_start(slot)
        if prepped is not None:
            prep_finish(1 - slot, prepped)
        use_finish(ci, *part)
    o_ref[...] += ob_scr[...]
    if want_state:
        for d in range(2):
            for h in range(heads):
                st_ref[0, d, h] = s_scr[d, h].T


def _scan(l, seq, nseq, arrs, s0, want_state):
    t, hg = arrs[0].shape
    heads = hg // HG_DK
    nc = seq // CHUNK
    sums, masks = _scan_tables()
    sums, masks = jnp.asarray(sums, F32).astype(BF16), jnp.asarray(masks)
    seq_spec = pl.BlockSpec((seq, hg), lambda b: (b, 0))
    in_specs = [seq_spec] * 6
    args = list(arrs)
    if s0 is not None:
        in_specs.append(pl.BlockSpec((1, 1) + s0.shape[2:], lambda b: (b, l, 0, 0, 0, 0)))
        args.append(s0)
    in_specs += [_const_spec(sums.shape, lambda b: (0, 0, 0)), _const_spec(masks.shape, lambda b: (0, 0, 0, 0))]
    args += [sums, masks]
    out_specs = [seq_spec]
    out_shape = [jax.ShapeDtypeStruct((t, hg), F32)]
    if want_state:
        out_specs.append(pl.BlockSpec((1, 2, heads, HG_DK, HG_DK), lambda b: (b, 0, 0, 0, 0)))
        out_shape.append(jax.ShapeDtypeStruct((nseq, 2, heads, HG_DK, HG_DK), F32))
    res = pl.pallas_call(
        functools.partial(_scan_kernel, nc, heads, s0 is not None, want_state),
        grid=(nseq,),
        in_specs=in_specs,
        out_specs=out_specs,
        out_shape=out_shape,
        scratch_shapes=[pltpu.VMEM((2, heads, HG_DK, HG_DK), F32), pltpu.VMEM((seq, hg), F32),
                        pltpu.VMEM((2, 2, N_LEVELS + 2, CHUNK, hg), BF16),
                        pltpu.VMEM((2, 2, N_LEVELS + 2, CHUNK, hg), BF16),
                        pltpu.VMEM((2, 2, CHUNK, hg), BF16), pltpu.VMEM((2, 2, SUBLANES, hg), F32)],
        compiler_params=pltpu.CompilerParams(vmem_limit_bytes=VMEM_LIMIT),
        name="hgrn2_scan",
    )(*args)
    return res if want_state else (res[0], None)


def _hyena_kernel(L, nsub, u_ref, x2_ref, kr_ref, ki_ref, d_ref, nw_ref, fe_ref, fo_ref, ie_ref, io_ref, y_ref,
                  u_scr, z_scr):
    h = L // 2
    groups = range(u_scr.shape[1])
    lanes = lambda j: slice(j * LANES, (j + 1) * LANES)
    kr, ki = kr_ref[0], ki_ref[0]
    for b in range(nsub):
        rows = slice(b * L, (b + 1) * L)
        u = u_ref[rows, :].astype(F32)
        for j in groups:
            u_scr[b, j] = u[:, lanes(j)]
        parity = lambda par: jnp.concatenate([u_scr[b, j, pl.ds(par, h, stride=2), :] for j in groups], axis=1)
        ue, uo = parity(0).astype(BF16), parity(1).astype(BF16)
        p = jnp.dot(fe_ref[...], ue, preferred_element_type=F32)
        q = jnp.dot(fo_ref[...], uo, preferred_element_type=F32)
        ur = jnp.concatenate([p[:h] + q[:h], p[:h] - q[:h]], axis=0)
        us = jnp.concatenate([p[h:] + q[h:], q[h:] - p[h:]], axis=0)
        zr = ur * kr + us * ki
        zi = ur * ki - us * kr
        even = jnp.concatenate([zr[:h] + zr[h:], zi[:h] - zi[h:]], axis=0).astype(BF16)
        odd = jnp.concatenate([zr[:h] - zr[h:], zi[:h] + zi[h:]], axis=0).astype(BF16)
        for par, (tab_ref, zz) in enumerate(((ie_ref, even), (io_ref, odd))):
            zp = jnp.dot(tab_ref[...], zz, preferred_element_type=F32)
            for j in groups:
                z_scr[b, j, pl.ds(par, h, stride=2), :] = zp[:, lanes(j)]
        z = jnp.concatenate([z_scr[b, j] for j in groups], axis=1)
        y = x2_ref[rows, :] * (z + d_ref[0] * u)
        y_ref[rows, :] = (_rms(y) * nw_ref[0]).astype(y_ref.dtype)


def _hyena(l, seq, nseq, u, x2, kr, ki, p):
    t, hw = u.shape
    h = seq // 2
    cm, sm = _dft_tables(seq)
    tabs = []
    for par in range(2):
        c, s = cm[:h, par::2], sm[:h, par::2]
        tabs.append((np.concatenate([c, s], axis=0), np.concatenate([c.T, -s.T], axis=1) / seq))
    bf = lambda a: jnp.asarray(a, F32).astype(BF16)
    fe, fo, ie, io = bf(tabs[0][0]), bf(tabs[1][0]), bf(tabs[0][1]), bf(tabs[1][1])
    nsub = max(1, HYENA_ROWS // seq)
    seq_spec = pl.BlockSpec((nsub * seq, hw), lambda b: (b, 0))
    lay = lambda shape: _const_spec((1,) + shape, lambda b: (l, 0, 0))
    tab = lambda a: _const_spec(a.shape, lambda b: (0, 0))
    return pl.pallas_call(
        functools.partial(_hyena_kernel, seq, nsub),
        grid=(nseq // nsub,),
        in_specs=[seq_spec, seq_spec, lay((seq, hw)), lay((seq, hw)), lay((1, hw)), lay((1, hw)),
                  tab(fe), tab(fo), tab(ie), tab(io)],
        out_specs=seq_spec,
        out_shape=jax.ShapeDtypeStruct((t, hw), BF16),
        scratch_shapes=[pltpu.VMEM((nsub, hw // LANES, seq, LANES), F32)] * 2,
        compiler_params=pltpu.CompilerParams(vmem_limit_bytes=VMEM_LIMIT),
        name="hyena_conv",
    )(u, x2, kr, ki, p["hy_d3"], p["hy_norm_w3"], fe, fo, ie, io)


def _post_kernel(row_w, heads, ff_chunk, last, x_ref, o_ref, sg_ref, y_ref, mod_ref, hn_ref, wout_ref,
                 n2_ref, wup_ref, cw_ref, wdn_ref, fn_ref, out_ref, x_scr, h2_scr, act_scr):
    d = x_ref.shape[1]
    hg = o_ref.shape[1]
    dff = wdn_ref.shape[1]
    m = mod_ref[0]
    g1, sh2, sc2, g2 = (m[:, i * d:(i + 1) * d] for i in range(2, 6))
    hn = hn_ref[0]
    mix = _bdot(y_ref[...], wout_ref[0, hg:, :])
    for h in range(heads):
        cols = slice(h * HG_DK, (h + 1) * HG_DK)
        mix = mix + _bdot(_rms(o_ref[:, cols]) * hn[:, cols] * sg_ref[:, cols], wout_ref[0, cols, :])
    x_scr[...] = x_ref[...] + g1 * mix
    h2_scr[...] = (_rms(x_scr[...]) * n2_ref[0] * (1.0 + sc2) + sh2).astype(BF16)
    cw = cw_ref[0]

    for j in range(dff // ff_chunk):
        a, b = j * ff_chunk, (j + 1) * ff_chunk
        gate = _dwconv3(jnp.dot(h2_scr[...], wup_ref[0, :, a:b], preferred_element_type=F32),
                        cw[:, a:b], row_w)
        val = _dwconv3(jnp.dot(h2_scr[...], wup_ref[0, :, dff + a:dff + b], preferred_element_type=F32),
                       cw[:, dff + a:dff + b], row_w)
        act_scr[:, a:b] = (_silu(gate) * val).astype(BF16)
    x = x_scr[...] + g2 * jnp.dot(act_scr[...], wdn_ref[0], preferred_element_type=F32)
    out_ref[...] = _rms(x) * fn_ref[...] if last else x


def _post(l, x, o, sg, y, mod, mod_row, row_w, seq, tm, last, p):
    t, d = x.shape
    hg = o.shape[1]
    dff = p["ffn_w_down"].shape[1]
    tok = lambda w: pl.BlockSpec((tm, w), lambda i: (i, 0))
    lay = lambda shape: _const_spec((1,) + shape, lambda i: (l, 0, 0))
    return pl.pallas_call(
        functools.partial(_post_kernel, row_w, hg // HG_DK, 256, last),
        grid=(t // tm,),
        in_specs=[
            tok(d), tok(hg), tok(hg), tok(y.shape[1]),
            pl.BlockSpec((1, 1, mod.shape[2]), lambda i: (mod_row(l, i * tm // seq), 0, 0)),
            lay((1, hg)), lay((d, d)), lay((1, d)), lay((d, 2 * dff)), lay((3, 2 * dff)), lay((dff, d)),
            _const_spec((1, d), lambda i: (0, 0)),
        ],
        out_specs=tok(d),
        out_shape=jax.ShapeDtypeStruct((t, d), F32),
        scratch_shapes=[pltpu.VMEM((tm, d), F32), pltpu.VMEM((tm, d), BF16), pltpu.VMEM((tm, dff), BF16)],
        compiler_params=pltpu.CompilerParams(vmem_limit_bytes=VMEM_LIMIT),
        name="post_mixer_ffn",
    )(x, o, sg, y, mod, p["hg_norm_w3"], p["w_out_bf"], p["norm2_w3"], p["ffn_w_up_bf"],
      p["ffn_conv_w"], p["ffn_w_down_bf"], p["final_norm_w2"])


def _group(x3, mod, mod_row, row_w, s0, want_state, filt, p):
    nseq, seq, d = x3.shape
    depth = p["w_in"].shape[0]
    x = x3.reshape(nseq * seq, d)
    kr, ki = filt
    states = []
    for l in range(depth):
        q, kf, kb, lf, lb, v, sg, u, x2 = _pre(l, x, mod, mod_row, row_w, seq, PRE_TILE, p)
        o, st = _scan(l, seq, nseq, (q, kf, kb, lf, lb, v), s0, want_state)
        y = _hyena(l, seq, nseq, u, x2, kr, ki, p)
        x = _post(l, x, o, sg, y, mod, mod_row, row_w, seq, POST_TILE, l == depth - 1, p)
        states.append(st)
    return x.reshape(nseq, seq, d), states


def kernel(x_prompt, x_sample, state_hgrn, c, c_ctx, w_in, w_out, ada_w, ada_b, norm1_w, norm2_w,
           hg_lb_fwd, hg_lb_bwd, hg_norm_w, hy_conv_w, hy_w1, hy_b1, hy_freq1, hy_w2, hy_b2, hy_freq2,
           hy_w3, hy_d, hy_norm_w, ffn_w_up, ffn_conv_w, ffn_w_down, final_norm_w):
    depth, d, _ = w_in.shape
    vec3 = lambda a: a.reshape(a.shape[0], 1, a.shape[1])
    p = dict(w_in=w_in, w_in_bf=w_in.astype(BF16), w_out_bf=w_out.astype(BF16),
             ffn_w_up_bf=ffn_w_up.astype(BF16), ffn_w_down=ffn_w_down, ffn_w_down_bf=ffn_w_down.astype(BF16),
             norm1_w3=vec3(norm1_w), norm2_w3=vec3(norm2_w), hg_norm_w3=vec3(hg_norm_w),
             hy_d=hy_d, hy_d3=vec3(hy_d), hy_norm_w3=vec3(hy_norm_w), final_norm_w2=final_norm_w[None, :],
             hg_lb_fwd=hg_lb_fwd, hg_lb_bwd=hg_lb_bwd, hy_conv_w=hy_conv_w, ffn_conv_w=ffn_conv_w,
             hy_w1=hy_w1, hy_b1=hy_b1, hy_freq1=hy_freq1, hy_w2=hy_w2, hy_b2=hy_b2, hy_freq2=hy_freq2,
             hy_w3=hy_w3)

    n_dec = c.shape[0]
    cv = jnp.concatenate([c, c_ctx[None, :], jnp.zeros((MOD_ROWS - n_dec - 1, d), F32)], axis=0)
    mod = _modulation(cv, ada_w, ada_b).reshape(depth * MOD_ROWS, 1, 6 * d)

    seq_p, seq_s = x_prompt.shape[1], x_sample.shape[1]
    y_p, states = _group(x_prompt, mod, lambda l, b: l * MOD_ROWS + n_dec, seq_p, None, True,
                         _filters(seq_p, p), p)
    y_s, _ = _group(x_sample, mod, lambda l, b: l * MOD_ROWS + b, GRID_W, state_hgrn, False,
                    _filters(seq_s, p), p)
    return (y_p, y_s, jnp.stack(states, axis=1))
```

```python
import functools
import math

import numpy as np
import jax
import jax.numpy as jnp
from jax import lax
from jax.experimental import pallas as pl
from jax.experimental.pallas import tpu as pltpu

F32 = jnp.float32
BF16 = jnp.bfloat16

HG_DK = 128
CHUNK = 64
GRID_W = 64
HY_BANDS = 16
HY_TARGET = 1e-2
HY_FAST = 0.3
HY_SLOW = 1.5
EPS = 1e-6

LANES = 128
VMEM_LIMIT = 56 * 1024 * 1024

N_LEVELS = int(math.log2(CHUNK))
MOD_ROWS = 16
PRE_TILE = 1024
HYENA_ROWS = 1024
POST_TILE = 512
PROJ_CHUNK = 256


def _bdot(a, b):
    return jnp.dot(a.astype(BF16), b.astype(BF16), preferred_element_type=F32)


def _bdot_nt(a, b):
    return lax.dot_general(a.astype(BF16), b.astype(BF16), (((1,), (1,)), ((), ())),
                           preferred_element_type=F32)


def _bdot_tn(a, b):
    return lax.dot_general(a.astype(BF16), b.astype(BF16), (((0,), (0,)), ((), ())),
                           preferred_element_type=F32)


def _hdot(a, b):
    return jnp.dot(a, b, preferred_element_type=F32, precision=lax.Precision.HIGHEST)


def _silu(x):
    return x * jax.nn.sigmoid(x)


def _rms(x):
    return x * lax.rsqrt(jnp.mean(x * x, axis=-1, keepdims=True) + EPS)


def _const_spec(shape, index_map):
    return pl.BlockSpec(shape, index_map, pipeline_mode=pl.Buffered(1))


def _dwconv3(p, w, row_w):
    n = p.shape[0]
    r = lax.broadcasted_iota(jnp.int32, (n, 1), 0) & (row_w - 1)
    prev = jnp.where(r == 0, 0.0, pltpu.roll(p, 1, 0))
    nxt = jnp.where(r == row_w - 1, 0.0, pltpu.roll(p, n - 1, 0))
    return w[0:1] * prev + w[1:2] * p + w[2:3] * nxt


def _mod_kernel(cv_ref, w_ref, b_ref, o_ref):
    s = _silu(cv_ref[...])
    w = w_ref[0]
    s_hi = s.astype(BF16)
    s_lo = (s - s_hi.astype(F32)).astype(BF16)
    w_hi = w.astype(BF16)
    w_lo = (w - w_hi.astype(F32)).astype(BF16)
    rows = s.shape[0]
    hi = jnp.dot(jnp.concatenate([s_hi, s_lo], axis=0), w_hi, preferred_element_type=F32)
    o_ref[0] = hi[:rows] + hi[rows:] + jnp.dot(s_hi, w_lo, preferred_element_type=F32) + b_ref[0]


def _modulation(cv, ada_w, ada_b):
    depth, d, n = ada_w.shape
    tn = 1536
    return pl.pallas_call(
        _mod_kernel,
        grid=(depth, n // tn),
        in_specs=[
            pl.BlockSpec((MOD_ROWS, d), lambda l, j: (0, 0)),
            pl.BlockSpec((1, d, tn), lambda l, j: (l, 0, j)),
            pl.BlockSpec((1, 1, tn), lambda l, j: (l, 0, j)),
        ],
        out_specs=pl.BlockSpec((1, MOD_ROWS, tn), lambda l, j: (l, 0, j)),
        out_shape=jax.ShapeDtypeStruct((depth, MOD_ROWS, n), F32),
        compiler_params=pltpu.CompilerParams(vmem_limit_bytes=VMEM_LIMIT),
        name="modulation",
    )(cv, ada_w, ada_b.reshape(depth, 1, n))


def _dft_tables(L):
    f = np.arange(L, dtype=np.int64)[:, None]
    t = np.arange(L, dtype=np.int64)[None, :]
    ang = 2.0 * np.pi * (((2 * f + 1) * t) % (4 * L)).astype(np.float64) / (4 * L)
    return np.cos(ang), np.sin(ang)


def _paired_freq_order(L):
    half = np.arange(L // 2)
    return np.concatenate([half, L - 1 - half])


def _filter_feats(L):
    half = LANES // 2

    def feats(pos):
        t = pos / (L - 1)
        bands = np.linspace(1e-4, HY_BANDS - 1, HY_BANDS)[None, :]
        ang = 2.0 * np.pi * pos[:, None] * bands / L
        out = np.zeros((pos.shape[0], half), np.float64)
        out[:, 0] = t
        out[:, 1:1 + HY_BANDS] = np.cos(ang)
        out[:, 1 + HY_BANDS:1 + 2 * HY_BANDS] = np.sin(ang)
        return out
    pos = np.arange(L, dtype=np.float64)
    return np.concatenate([feats(pos), feats(L - pos)], axis=1).astype(np.float32)


def _filter_taps_kernel(L, hw, ft_ref, w1_ref, b1_ref, f1_ref, w2_ref, b2_ref, f2_ref, w3_ref,
                        dl_ref, kk_ref):
    half = LANES // 2
    ft = ft_ref[...]
    h = jnp.sin(f1_ref[0] * (_hdot(ft, w1_ref[0]) + b1_ref[0]))
    h = jnp.sin(f2_ref[0] * (_hdot(h, w2_ref[0]) + b2_ref[0]))
    h = _hdot(h, w3_ref[0])
    dl = dl_ref[...]
    k1 = h[:, :hw] * jnp.exp(-ft[:, 0:1] * dl)
    row = lax.broadcasted_iota(jnp.int32, (L, 1), 0)
    k2 = jnp.where(row == 0, 0.0, -(h[:, hw:] * jnp.exp(-ft[:, half:half + 1] * dl)))
    norm = jnp.sum(jnp.abs(k1), axis=0, keepdims=True) + jnp.sum(jnp.abs(k2), axis=0, keepdims=True) + EPS
    kk_ref[0, :, :hw] = (k1 / norm).astype(BF16)
    kk_ref[0, :, hw:] = (k2 / norm).astype(BF16)


def _filter_dft_kernel(hw, kk_ref, cm_ref, sm_ref, kr_ref, ki_ref):
    kk = kk_ref[0]
    gc = jnp.dot(cm_ref[...], kk, preferred_element_type=F32)
    gs = jnp.dot(sm_ref[...], kk, preferred_element_type=F32)
    tf = gc.shape[0]
    row = lax.broadcasted_iota(jnp.int32, (tf, 1), 0)
    second = pl.program_id(1) * tf + row >= kk.shape[0] // 2
    alt = jnp.where(((row & 1) == 0) != second, 1.0, -1.0)
    kr_ref[0] = gc[:, :hw] - alt * gs[:, hw:]
    ki_ref[0] = -(gs[:, :hw] + alt * gc[:, hw:])


def _filters(L, p):
    depth, _, hw2 = p["hy_w3"].shape
    hw = hw2 // 2
    half = LANES // 2
    cm, sm = _dft_tables(L)
    order = _paired_freq_order(L)
    ft = jnp.asarray(_filter_feats(L))
    min_decay = math.log(HY_TARGET) / HY_SLOW
    max_decay = math.log(HY_TARGET) / HY_FAST
    deltas = jnp.asarray(np.abs(np.linspace(min_decay, max_decay, hw)).astype(np.float32)[None, :])

    def pad_to(a, rows, cols):
        return jnp.pad(a, ((0, 0), (0, rows - a.shape[1]), (0, cols - a.shape[2])))

    def block_diag(a, b):
        za = jnp.zeros(a.shape[:2] + (b.shape[2],), F32)
        zb = jnp.zeros(b.shape[:2] + (a.shape[2],), F32)
        return jnp.concatenate([jnp.concatenate([a, za], axis=2), jnp.concatenate([zb, b], axis=2)], axis=1)

    w1 = pad_to(p["hy_w1"], half, half)
    w2 = pad_to(p["hy_w2"], half, half)
    w3 = pad_to(p["hy_w3"], half, hw2)
    w1, w2, w3 = block_diag(w1, w1), block_diag(w2, w2), block_diag(w3[:, :, :hw], w3[:, :, hw:])
    vec = lambda a: jnp.tile(pad_to(a[:, None, :], 1, half), (1, 1, 2))
    lay = lambda shape: pl.BlockSpec((1,) + shape, lambda l: (l, 0, 0))
    kk = pl.pallas_call(
        functools.partial(_filter_taps_kernel, L, hw),
        grid=(depth,),
        in_specs=[
            _const_spec((L, LANES), lambda l: (0, 0)),
            lay((LANES, LANES)), lay((1, LANES)), lay((1, LANES)),
            lay((LANES, LANES)), lay((1, LANES)), lay((1, LANES)),
            lay((LANES, hw2)),
            _const_spec((1, hw), lambda l: (0, 0)),
        ],
        out_specs=lay((L, hw2)),
        out_shape=jax.ShapeDtypeStruct((depth, L, hw2), BF16),
        compiler_params=pltpu.CompilerParams(vmem_limit_bytes=VMEM_LIMIT),
        name=f"hyena_filter_taps_{L}",
    )(ft, w1, vec(p["hy_b1"]), vec(p["hy_freq1"]), w2, vec(p["hy_b2"]), vec(p["hy_freq2"]), w3, deltas)
    tf = 256
    frq = pl.BlockSpec((tf, L), lambda l, j: (j, 0))
    out = pl.BlockSpec((1, tf, hw), lambda l, j: (l, j, 0))
    return pl.pallas_call(
        functools.partial(_filter_dft_kernel, hw),
        grid=(depth, L // tf),
        in_specs=[pl.BlockSpec((1, L, hw2), lambda l, j: (l, 0, 0)), frq, frq],
        out_specs=[out, out],
        out_shape=[jax.ShapeDtypeStruct((depth, L, hw), F32)] * 2,
        compiler_params=pltpu.CompilerParams(vmem_limit_bytes=VMEM_LIMIT),
        name=f"hyena_filter_dft_{L}",
    )(kk, jnp.asarray(cm[order], F32).astype(BF16), jnp.asarray(sm[order], F32).astype(BF16))


def _lower_bound(lb_ref, l):
    prm = lb_ref[...]
    e = jnp.exp(prm - jnp.max(prm, axis=0, keepdims=True))
    s = e / jnp.sum(e, axis=0, keepdims=True)
    acc = jnp.zeros_like(s[0:1])
    for i in range(1, l + 1):
        acc = acc + s[i:i + 1]
    return acc


def _forget_gate(z, lb):
    t = jnp.exp(-jnp.abs(z))
    one_t = 1.0 + t
    pos = z >= 0.0
    num = jnp.where(pos, 1.0 + lb * t, lb + t)
    log_f = jnp.where(num > 0.0, jnp.log(num), z) - jnp.log(one_t)
    return log_f, (1.0 - lb) * (jnp.where(pos, t, 1.0) / one_t)


def _pre_kernel(l, row_w, hg, hy, x_ref, mod_ref, n1_ref, win_ref, lbf_ref, lbb_ref, cw_ref,
                q_ref, kf_ref, kb_ref, lf_ref, lb_ref, v_ref, og_ref, u_ref, x2_ref, h_scr):
    d = x_ref.shape[1]
    m = mod_ref[0]
    sh1, sc1 = m[:, 0:d], m[:, d:2 * d]
    h_scr[...] = (_rms(x_ref[...]) * n1_ref[0] * (1.0 + sc1) + sh1).astype(BF16)

    w = PROJ_CHUNK
    cw = cw_ref[0]
    lbs = (_lower_bound(lbf_ref, l), _lower_bound(lbb_ref, l))
    conv = lambda r, j, c: _dwconv3(r, cw[:, j * hy + c:j * hy + c + w], row_w)

    tasks = []
    for c in range(0, hg, w):
        cs = slice(c, c + w)

        def q_task(r, cs=cs):
            q_ref[:, cs] = (_silu(r[0]) * (HG_DK ** -0.5)).astype(q_ref.dtype)

        def gate_task(r, cs=cs, i=0):
            g_ref, k_ref = ((lf_ref, kf_ref), (lb_ref, kb_ref))[i]
            log_f, k = _forget_gate(r[0], lbs[i][:, cs])
            g_ref[:, cs], k_ref[:, cs] = log_f, k.astype(k_ref.dtype)

        def v_task(r, cs=cs):
            v_ref[:, cs] = r[0].astype(v_ref.dtype)

        def og_task(r, cs=cs):
            og_ref[:, cs] = r[0].astype(og_ref.dtype)

        def u_task(r, cs=cs, c=c):
            u_ref[:, cs] = (conv(r[1], 1, c) * conv(r[0], 0, c)).astype(u_ref.dtype)

        def x2_task(r, cs=cs, c=c):
            x2_ref[:, cs] = conv(r[0], 2, c).astype(x2_ref.dtype)

        tasks += [((c,), q_task), ((hg + c,), gate_task), ((2 * hg + c,), functools.partial(gate_task, i=1)),
                  ((3 * hg + c,), v_task), ((4 * hg + c,), og_task),
                  ((5 * hg + c, 5 * hg + hy + c), u_task), ((5 * hg + 2 * hy + c,), x2_task)]

    issue = lambda offs: [jnp.dot(h_scr[...], win_ref[0, :, o:o + w], preferred_element_type=F32) for o in offs]
    pending = issue(tasks[0][0])
    for i, (_, consume) in enumerate(tasks):
        ahead = issue(tasks[i + 1][0]) if i + 1 < len(tasks) else None
        consume(pending)
        pending = ahead


def _pre(l, x, mod, mod_row, row_w, seq, tm, p):
    t, d = x.shape
    hg = p["hg_lb_fwd"].shape[1]
    hy = p["hy_d"].shape[1]
    n_in = p["w_in"].shape[2]
    depth = p["w_in"].shape[0]
    tok = lambda w: pl.BlockSpec((tm, w), lambda i: (i, 0))
    out = lambda dt: jax.ShapeDtypeStruct((t, hg), dt)
    return pl.pallas_call(
        functools.partial(_pre_kernel, l, row_w, hg, hy),
        grid=(t // tm,),
        in_specs=[
            tok(d),
            pl.BlockSpec((1, 1, mod.shape[2]), lambda i: (mod_row(l, i * tm // seq), 0, 0)),
            _const_spec((1, 1, d), lambda i: (l, 0, 0)),
            _const_spec((1, d, n_in), lambda i: (l, 0, 0)),
            _const_spec((depth, hg), lambda i: (0, 0)),
            _const_spec((depth, hg), lambda i: (0, 0)),
            _const_spec((1, 3, 3 * hy), lambda i: (l, 0, 0)),
        ],
        out_specs=[tok(hg)] * 9,
        out_shape=[out(BF16)] * 3 + [out(F32)] * 2 + [out(BF16)] * 4,
        scratch_shapes=[pltpu.VMEM((tm, d), BF16)],
        compiler_params=pltpu.CompilerParams(vmem_limit_bytes=VMEM_LIMIT),
        name="pre_mixer",
    )(x, mod, p["norm1_w3"], p["w_in_bf"], p["hg_lb_fwd"], p["hg_lb_bwd"], p["hy_conv_w"])


LEVELS = tuple(CHUNK >> (i + 1) for i in range(N_LEVELS))
SUBLANES = 8
MXU_LEVELS = tuple(s for s in LEVELS if 1 < s < SUBLANES)
LOG2E = 1.4426950408889634


def _scan_tables():
    c = CHUNK
    j = np.arange(c)[:, None]
    i = np.arange(c)[None, :]
    masks = [((j // (2 * s) == i // (2 * s)) & (j % (2 * s) >= s) & (i % (2 * s) < s)).astype(np.float32)
             for s in LEVELS]
    masks.append(np.eye(c, dtype=np.float32))
    mf = np.stack(masks)
    blocks = [np.tril(np.ones((c, c), np.float32))]
    for s in MXU_LEVELS:
        w = np.zeros((c, c), np.float32)
        for t in range(c):
            ref = (t // (2 * s)) * 2 * s + s - 1
            if t % (2 * s) >= s:
                w[t, ref + 1:t + 1] = 1.0
            else:
                w[t, t + 1:ref + 1] = 1.0
        blocks.append(w)
    fwd = np.concatenate(blocks, axis=0)
    bwd = np.concatenate([w[::-1, ::-1] for w in blocks], axis=0)
    sums = np.stack([np.concatenate([fwd] * 3, axis=1), np.concatenate([bwd] * 3, axis=1)])
    return sums, np.stack([mf, mf[:, ::-1, ::-1]])


def _decay_logs(sums, g, d):
    c = g.shape[0]
    fwd = d == 0
    b = sums[:c]
    out = []
    for s in LEVELS:
        if s >= SUBLANES:
            parts = []
            for b0 in range(0, c, 2 * s):
                ref = b0 + s - 1 if fwd else b0 + s
                br, lo, hi = b[ref:ref + 1], b[b0:b0 + s], b[b0 + s:b0 + 2 * s]
                parts += [br - lo, hi - br] if fwd else [lo - br, br - hi]
            out.append(jnp.concatenate(parts, axis=0))
        elif s in MXU_LEVELS:
            i = 1 + MXU_LEVELS.index(s)
            out.append(sums[i * c:(i + 1) * c])
        else:
            r = lax.broadcasted_iota(jnp.int32, (c, 1), 0)
            out.append(jnp.where((r & 1) == (1 if fwd else 0), g, 0.0))
    far = b[c - 1:c] if fwd else b[0:1]
    return out + [b, far - b]


def _scan_kernel(nc, heads, has_s0, want_state, *refs):
    q_ref, kf_ref, kb_ref, lf_ref, lb_ref, v_ref = refs[:6]
    refs = refs[6:]
    if has_s0:
        s0_ref, refs = refs[0], refs[1:]
    sum_ref, m_ref, o_ref = refs[:3]
    refs = refs[3:]
    if want_state:
        st_ref, refs = refs[0], refs[1:]
    s_scr, ob_scr, qs_scr, ks_scr, v_scr, ea_scr = refs
    c = CHUNK
    dirs = ((kf_ref, lf_ref), (kb_ref, lb_ref))
    pairs = [(d, h) for d in range(2) for h in range(heads)]
    cols = lambda h: slice(h * HG_DK, (h + 1) * HG_DK)

    def rows_of(ci, d):
        return pl.ds((ci if d == 0 else nc - 1 - ci) * c, c)

    def prep_start(ci):
        out = []
        for d, (k_ref, g_ref) in enumerate(dirs):
            rows = rows_of(ci, d)
            g = g_ref[rows, :] * LOG2E
            g1 = g.astype(BF16)
            r1 = g - g1.astype(F32)
            g2 = r1.astype(BF16)
            g3 = (r1 - g2.astype(F32)).astype(BF16)
            sums = jnp.dot(sum_ref[d], jnp.concatenate([g1, g2, g3], axis=0), preferred_element_type=F32)
            out.append((g, q_ref[rows, :], k_ref[rows, :], v_ref[rows, :], sums))
        return out

    def prep_finish(slot, prepped):
        for d, (g, q, k, v, sums) in enumerate(prepped):
            e = [jnp.exp2(x) for x in _decay_logs(sums, g, d)]
            eb = [x.astype(BF16) for x in e]
            qb, kb = q.astype(BF16), k.astype(BF16)
            for s in range(N_LEVELS):
                qs_scr[slot, d, s] = qb * eb[s]
                ks_scr[slot, d, s] = kb * eb[s]
            qs_scr[slot, d, N_LEVELS] = qb
            ks_scr[slot, d, N_LEVELS] = kb
            qs_scr[slot, d, N_LEVELS + 1] = qb * eb[N_LEVELS]
            ks_scr[slot, d, N_LEVELS + 1] = kb * eb[N_LEVELS + 1]
            v_scr[slot, d] = v.astype(BF16)
            ea_scr[slot, d, 0:1, :] = e[N_LEVELS][c - 1:c] if d == 0 else e[N_LEVELS][0:1]

    def use_start(slot):
        scores, inter, new_state, vals = {}, {}, {}, {}
        for d, h in pairs:
            scores[d, h] = [_bdot_nt(qs_scr[slot, d, s, :, cols(h)], ks_scr[slot, d, s, :, cols(h)])
                            for s in range(N_LEVELS + 1)]
            st = s_scr[d, h]
            vals[d, h] = v_scr[slot, d, :, cols(h)]
            inter[d, h] = _bdot_nt(qs_scr[slot, d, N_LEVELS + 1, :, cols(h)], st)
            new_state[d, h] = (ea_scr[slot, d, 0:1, cols(h)] * st
                               + _bdot_tn(vals[d, h], ks_scr[slot, d, N_LEVELS + 1, :, cols(h)]))
        return scores, inter, new_state, vals

    def use_finish(ci, scores, inter, new_state, vals):
        for d in range(2):
            keep = [m_ref[d, s] > 0.5 for s in range(N_LEVELS + 1)]
            for h in range(heads):
                a = jnp.where(keep[N_LEVELS], scores[d, h][N_LEVELS], 0.0)
                for s in range(N_LEVELS):
                    a = jnp.where(keep[s], scores[d, h][s], a)
                o = _bdot(a, vals[d, h]) + inter[d, h]
                (o_ref if d == 0 else ob_scr)[rows_of(ci, d), cols(h)] = o
                s_scr[d, h] = new_state[d, h]

    for d, h in pairs:
        s_scr[d, h] = s0_ref[0, 0, d, h].T if has_s0 else jnp.zeros(s_scr.shape[2:], F32)

    prep_finish(0, prep_start(0))
    for ci in range(nc):
        slot = ci % 2
        prepped = prep_start(ci + 1) if ci + 1 < nc else None
        part = use_start(slot)
        if prepped is not None:
            prep_finish(1 - slot, prepped)
        use_finish(ci, *part)
    o_ref[...] += ob_scr[...]
    if want_state:
        for d in range(2):
            for h in range(heads):
                st_ref[0, d, h] = s_scr[d, h].T


def _scan(l, seq, nseq, arrs, s0, want_state):
    t, hg = arrs[0].shape
    heads = hg // HG_DK
    nc = seq // CHUNK
    sums, masks = _scan_tables()
    sums, masks = jnp.asarray(sums, F32).astype(BF16), jnp.asarray(masks)
    seq_spec = pl.BlockSpec((seq, hg), lambda b: (b, 0))
    in_specs = [seq_spec] * 6
    args = list(arrs)
    if s0 is not None:
        in_specs.append(pl.BlockSpec((1, 1) + s0.shape[2:], lambda b: (b, l, 0, 0, 0, 0)))
        args.append(s0)
    in_specs += [_const_spec(sums.shape, lambda b: (0, 0, 0)), _const_spec(masks.shape, lambda b: (0, 0, 0, 0))]
    args += [sums, masks]
    out_specs = [seq_spec]
    out_shape = [jax.ShapeDtypeStruct((t, hg), F32)]
    if want_state:
        out_specs.append(pl.BlockSpec((1, 2, heads, HG_DK, HG_DK), lambda b: (b, 0, 0, 0, 0)))
        out_shape.append(jax.ShapeDtypeStruct((nseq, 2, heads, HG_DK, HG_DK), F32))
    res = pl.pallas_call(
        functools.partial(_scan_kernel, nc, heads, s0 is not None, want_state),
        grid=(nseq,),
        in_specs=in_specs,
        out_specs=out_specs,
        out_shape=out_shape,
        scratch_shapes=[pltpu.VMEM((2, heads, HG_DK, HG_DK), F32), pltpu.VMEM((seq, hg), F32),
                        pltpu.VMEM((2, 2, N_LEVELS + 2, CHUNK, hg), BF16),
                        pltpu.VMEM((2, 2, N_LEVELS + 2, CHUNK, hg), BF16),
                        pltpu.VMEM((2, 2, CHUNK, hg), BF16), pltpu.VMEM((2, 2, SUBLANES, hg), F32)],
        compiler_params=pltpu.CompilerParams(vmem_limit_bytes=VMEM_LIMIT),
        name="hgrn2_scan",
    )(*args)
    return res if want_state else (res[0], None)


def _hyena_kernel(L, nsub, u_ref, x2_ref, kr_ref, ki_ref, d_ref, nw_ref, fe_ref, fo_ref, ie_ref, io_ref, y_ref,
                  u_scr, z_scr):
    h = L // 2
    groups = range(u_scr.shape[1])
    lanes = lambda j: slice(j * LANES, (j + 1) * LANES)
    kr, ki = kr_ref[0], ki_ref[0]
    for b in range(nsub):
        rows = slice(b * L, (b + 1) * L)
        u = u_ref[rows, :].astype(F32)
        for j in groups:
            u_scr[b, j] = u[:, lanes(j)]
        parity = lambda par: jnp.concatenate([u_scr[b, j, pl.ds(par, h, stride=2), :] for j in groups], axis=1)
        ue, uo = parity(0).astype(BF16), parity(1).astype(BF16)
        p = jnp.dot(fe_ref[...], ue, preferred_element_type=F32)
        q = jnp.dot(fo_ref[...], uo, preferred_element_type=F32)
        ur = jnp.concatenate([p[:h] + q[:h], p[:h] - q[:h]], axis=0)
        us = jnp.concatenate([p[h:] + q[h:], q[h:] - p[h:]], axis=0)
        zr = ur * kr + us * ki
        zi = ur * ki - us * kr
        even = jnp.concatenate([zr[:h] + zr[h:], zi[:h] - zi[h:]], axis=0).astype(BF16)
        odd = jnp.concatenate([zr[:h] - zr[h:], zi[:h] + zi[h:]], axis=0).astype(BF16)
        for par, (tab_ref, zz) in enumerate(((ie_ref, even), (io_ref, odd))):
            zp = jnp.dot(tab_ref[...], zz, preferred_element_type=F32)
            for j in groups:
                z_scr[b, j, pl.ds(par, h, stride=2), :] = zp[:, lanes(j)]
        z = jnp.concatenate([z_scr[b, j] for j in groups], axis=1)
        y = x2_ref[rows, :] * (z + d_ref[0] * u)
        y_ref[rows, :] = (_rms(y) * nw_ref[0]).astype(y_ref.dtype)


def _hyena(l, seq, nseq, u, x2, kr, ki, p):
    t, hw = u.shape
    h = seq // 2
    cm, sm = _dft_tables(seq)
    tabs = []
    for par in range(2):
        c, s = cm[:h, par::2], sm[:h, par::2]
        tabs.append((np.concatenate([c, s], axis=0), np.concatenate([c.T, -s.T], axis=1) / seq))
    bf = lambda a: jnp.asarray(a, F32).astype(BF16)
    fe, fo, ie, io = bf(tabs[0][0]), bf(tabs[1][0]), bf(tabs[0][1]), bf(tabs[1][1])
    nsub = max(1, HYENA_ROWS // seq)
    seq_spec = pl.BlockSpec((nsub * seq, hw), lambda b: (b, 0))
    lay = lambda shape: _const_spec((1,) + shape, lambda b: (l, 0, 0))
    tab = lambda a: _const_spec(a.shape, lambda b: (0, 0))
    return pl.pallas_call(
        functools.partial(_hyena_kernel, seq, nsub),
        grid=(nseq // nsub,),
        in_specs=[seq_spec, seq_spec, lay((seq, hw)), lay((seq, hw)), lay((1, hw)), lay((1, hw)),
                  tab(fe), tab(fo), tab(ie), tab(io)],
        out_specs=seq_spec,
        out_shape=jax.ShapeDtypeStruct((t, hw), BF16),
        scratch_shapes=[pltpu.VMEM((nsub, hw // LANES, seq, LANES), F32)] * 2,
        compiler_params=pltpu.CompilerParams(vmem_limit_bytes=VMEM_LIMIT),
        name="hyena_conv",
    )(u, x2, kr, ki, p["hy_d3"], p["hy_norm_w3"], fe, fo, ie, io)


def _post_kernel(row_w, heads, ff_chunk, last, x_ref, o_ref, og_ref, y_ref, mod_ref, hn_ref, wout_ref,
                 n2_ref, wup_ref, cw_ref, wdn_ref, fn_ref, out_ref, x_scr, h2_scr, act_scr):
    d = x_ref.shape[1]
    hg = o_ref.shape[1]
    dff = wdn_ref.shape[1]
    m = mod_ref[0]
    g1, sh2, sc2, g2 = (m[:, i * d:(i + 1) * d] for i in range(2, 6))
    hn = hn_ref[0]
    mix = _bdot(y_ref[...], wout_ref[0, hg:, :])
    for h in range(heads):
        cols = slice(h * HG_DK, (h + 1) * HG_DK)
        out_gate = _silu(og_ref[:, cols].astype(F32))
        mix = mix + _bdot(_rms(o_ref[:, cols]) * hn[:, cols] * out_gate, wout_ref[0, cols, :])
    x_scr[...] = x_ref[...] + g1 * mix
    h2_scr[...] = (_rms(x_scr[...]) * n2_ref[0] * (1.0 + sc2) + sh2).astype(BF16)
    cw = cw_ref[0]

    for j in range(dff // ff_chunk):
        a, b = j * ff_chunk, (j + 1) * ff_chunk
        gate = _dwconv3(jnp.dot(h2_scr[...], wup_ref[0, :, a:b], preferred_element_type=F32),
                        cw[:, a:b], row_w)
        val = _dwconv3(jnp.dot(h2_scr[...], wup_ref[0, :, dff + a:dff + b], preferred_element_type=F32),
                       cw[:, dff + a:dff + b], row_w)
        act_scr[:, a:b] = (_silu(gate) * val).astype(BF16)
    x = x_scr[...] + g2 * jnp.dot(act_scr[...], wdn_ref[0], preferred_element_type=F32)
    out_ref[...] = _rms(x) * fn_ref[...] if last else x


def _post(l, x, o, og, y, mod, mod_row, row_w, seq, tm, last, p):
    t, d = x.shape
    hg = o.shape[1]
    dff = p["ffn_w_down"].shape[1]
    tok = lambda w: pl.BlockSpec((tm, w), lambda i: (i, 0))
    lay = lambda shape: _const_spec((1,) + shape, lambda i: (l, 0, 0))
    return pl.pallas_call(
        functools.partial(_post_kernel, row_w, hg // HG_DK, 256, last),
        grid=(t // tm,),
        in_specs=[
            tok(d), tok(hg), tok(hg), tok(y.shape[1]),
            pl.BlockSpec((1, 1, mod.shape[2]), lambda i: (mod_row(l, i * tm // seq), 0, 0)),
            lay((1, hg)), lay((d, d)), lay((1, d)), lay((d, 2 * dff)), lay((3, 2 * dff)), lay((dff, d)),
            _const_spec((1, d), lambda i: (0, 0)),
        ],
        out_specs=tok(d),
        out_shape=jax.ShapeDtypeStruct((t, d), F32),
        scratch_shapes=[pltpu.VMEM((tm, d), F32), pltpu.VMEM((tm, d), BF16), pltpu.VMEM((tm, dff), BF16)],
        compiler_params=pltpu.CompilerParams(vmem_limit_bytes=VMEM_LIMIT),
        name="post_mixer_ffn",
    )(x, o, og, y, mod, p["hg_norm_w3"], p["w_out_bf"], p["norm2_w3"], p["ffn_w_up_bf"],
      p["ffn_conv_w"], p["ffn_w_down_bf"], p["final_norm_w2"])


def _group(x3, mod, mod_row, row_w, s0, want_state, filt, p):
    nseq, seq, d = x3.shape
    depth = p["w_in"].shape[0]
    x = x3.reshape(nseq * seq, d)
    kr, ki = filt
    states = []
    for l in range(depth):
        q, kf, kb, lf, lb, v, og, u, x2 = _pre(l, x, mod, mod_row, row_w, seq, PRE_TILE, p)
        o, st = _scan(l, seq, nseq, (q, kf, kb, lf, lb, v), s0, want_state)
        y = _hyena(l, seq, nseq, u, x2, kr, ki, p)
        x = _post(l, x, o, og, y, mod, mod_row, row_w, seq, POST_TILE, l == depth - 1, p)
        states.append(st)
    return x.reshape(nseq, seq, d), states


def kernel(x_prompt, x_sample, state_hgrn, c, c_ctx, w_in, w_out, ada_w, ada_b, norm1_w, norm2_w,
           hg_lb_fwd, hg_lb_bwd, hg_norm_w, hy_conv_w, hy_w1, hy_b1, hy_freq1, hy_w2, hy_b2, hy_freq2,
           hy_w3, hy_d, hy_norm_w, ffn_w_up, ffn_conv_w, ffn_w_down, final_norm_w):
    depth, d, _ = w_in.shape
    vec3 = lambda a: a.reshape(a.shape[0], 1, a.shape[1])
    p = dict(w_in=w_in, w_in_bf=w_in.astype(BF16), w_out_bf=w_out.astype(BF16),
             ffn_w_up_bf=ffn_w_up.astype(BF16), ffn_w_down=ffn_w_down, ffn_w_down_bf=ffn_w_down.astype(BF16),
             norm1_w3=vec3(norm1_w), norm2_w3=vec3(norm2_w), hg_norm_w3=vec3(hg_norm_w),
             hy_d=hy_d, hy_d3=vec3(hy_d), hy_norm_w3=vec3(hy_norm_w), final_norm_w2=final_norm_w[None, :],
             hg_lb_fwd=hg_lb_fwd, hg_lb_bwd=hg_lb_bwd, hy_conv_w=hy_conv_w, ffn_conv_w=ffn_conv_w,
             hy_w1=hy_w1, hy_b1=hy_b1, hy_freq1=hy_freq1, hy_w2=hy_w2, hy_b2=hy_b2, hy_freq2=hy_freq2,
             hy_w3=hy_w3)

    n_dec = c.shape[0]
    cv = jnp.concatenate([c, c_ctx[None, :], jnp.zeros((MOD_ROWS - n_dec - 1, d), F32)], axis=0)
    mod = _modulation(cv, ada_w, ada_b).reshape(depth * MOD_ROWS, 1, 6 * d)

    seq_p, seq_s = x_prompt.shape[1], x_sample.shape[1]
    y_p, states = _group(x_prompt, mod, lambda l, b: l * MOD_ROWS + n_dec, seq_p, None, True,
                         _filters(seq_p, p), p)
    y_s, _ = _group(x_sample, mod, lambda l, b: l * MOD_ROWS + b, GRID_W, state_hgrn, False,
                    _filters(seq_s, p), p)
    return (y_p, y_s, jnp.stack(states, axis=1))
```

```python
import functools
import math

import numpy as np
import jax
import jax.numpy as jnp
from jax import lax
from jax.experimental import pallas as pl
from jax.experimental.pallas import tpu as pltpu

F32 = jnp.float32
BF16 = jnp.bfloat16

HG_DK = 128
CHUNK = 64
GRID_W = 64
HY_BANDS = 16
HY_TARGET = 1e-2
HY_FAST = 0.3
HY_SLOW = 1.5
EPS = 1e-6

LANES = 128
VMEM_LIMIT = 56 * 1024 * 1024

N_LEVELS = int(math.log2(CHUNK))
MOD_ROWS = 16
PRE_TILE = 1024
SCAN_ROWS = 1024
HYENA_ROWS = 1024
POST_TILE = 512
PROJ_CHUNK = 256


def _bdot(a, b):
    return jnp.dot(a.astype(BF16), b.astype(BF16), preferred_element_type=F32)


def _bdot_nt(a, b):
    return lax.dot_general(a.astype(BF16), b.astype(BF16), (((1,), (1,)), ((), ())),
                           preferred_element_type=F32)


def _bdot_tn(a, b):
    return lax.dot_general(a.astype(BF16), b.astype(BF16), (((0,), (0,)), ((), ())),
                           preferred_element_type=F32)


def _hdot(a, b):
    return jnp.dot(a, b, preferred_element_type=F32, precision=lax.Precision.HIGHEST)


def _silu(x):
    return x * jax.nn.sigmoid(x)


def _rms(x):
    return x * lax.rsqrt(jnp.mean(x * x, axis=-1, keepdims=True) + EPS)


def _const_spec(shape, index_map):
    return pl.BlockSpec(shape, index_map, pipeline_mode=pl.Buffered(1))


def _dwconv3(p, w, row_w):
    n = p.shape[0]
    r = lax.broadcasted_iota(jnp.int32, (n, 1), 0) & (row_w - 1)
    prev = jnp.where(r == 0, 0.0, pltpu.roll(p, 1, 0))
    nxt = jnp.where(r == row_w - 1, 0.0, pltpu.roll(p, n - 1, 0))
    return w[0:1] * prev + w[1:2] * p + w[2:3] * nxt


def _mod_kernel(cv_ref, w_ref, b_ref, o_ref):
    s = _silu(cv_ref[...])
    w = w_ref[0]
    s_hi = s.astype(BF16)
    s_lo = (s - s_hi.astype(F32)).astype(BF16)
    w_hi = w.astype(BF16)
    w_lo = (w - w_hi.astype(F32)).astype(BF16)
    rows = s.shape[0]
    hi = jnp.dot(jnp.concatenate([s_hi, s_lo], axis=0), w_hi, preferred_element_type=F32)
    o_ref[0] = hi[:rows] + hi[rows:] + jnp.dot(s_hi, w_lo, preferred_element_type=F32) + b_ref[0]


def _modulation(cv, ada_w, ada_b):
    depth, d, n = ada_w.shape
    tn = 1536
    return pl.pallas_call(
        _mod_kernel,
        grid=(depth, n // tn),
        in_specs=[
            pl.BlockSpec((MOD_ROWS, d), lambda l, j: (0, 0)),
            pl.BlockSpec((1, d, tn), lambda l, j: (l, 0, j)),
            pl.BlockSpec((1, 1, tn), lambda l, j: (l, 0, j)),
        ],
        out_specs=pl.BlockSpec((1, MOD_ROWS, tn), lambda l, j: (l, 0, j)),
        out_shape=jax.ShapeDtypeStruct((depth, MOD_ROWS, n), F32),
        compiler_params=pltpu.CompilerParams(vmem_limit_bytes=VMEM_LIMIT),
        name="modulation",
    )(cv, ada_w, ada_b.reshape(depth, 1, n))


def _dft_tables(L):
    f = np.arange(L, dtype=np.int64)[:, None]
    t = np.arange(L, dtype=np.int64)[None, :]
    ang = 2.0 * np.pi * (((2 * f + 1) * t) % (4 * L)).astype(np.float64) / (4 * L)
    return np.cos(ang), np.sin(ang)


def _paired_freq_order(L):
    half = np.arange(L // 2)
    return np.concatenate([half, L - 1 - half])


def _filter_feats(L):
    half = LANES // 2

    def feats(pos):
        t = pos / (L - 1)
        bands = np.linspace(1e-4, HY_BANDS - 1, HY_BANDS)[None, :]
        ang = 2.0 * np.pi * pos[:, None] * bands / L
        out = np.zeros((pos.shape[0], half), np.float64)
        out[:, 0] = t
        out[:, 1:1 + HY_BANDS] = np.cos(ang)
        out[:, 1 + HY_BANDS:1 + 2 * HY_BANDS] = np.sin(ang)
        return out
    pos = np.arange(L, dtype=np.float64)
    return np.concatenate([feats(pos), feats(L - pos)], axis=1).astype(np.float32)


def _filter_taps_kernel(L, hw, ft_ref, w1_ref, b1_ref, f1_ref, w2_ref, b2_ref, f2_ref, w3_ref,
                        dl_ref, kk_ref):
    half = LANES // 2
    ft = ft_ref[...]
    h = jnp.sin(f1_ref[0] * (_hdot(ft, w1_ref[0]) + b1_ref[0]))
    h = jnp.sin(f2_ref[0] * (_hdot(h, w2_ref[0]) + b2_ref[0]))
    h = _hdot(h, w3_ref[0])
    dl = dl_ref[...]
    k1 = h[:, :hw] * jnp.exp(-ft[:, 0:1] * dl)
    row = lax.broadcasted_iota(jnp.int32, (L, 1), 0)
    k2 = jnp.where(row == 0, 0.0, -(h[:, hw:] * jnp.exp(-ft[:, half:half + 1] * dl)))
    norm = jnp.sum(jnp.abs(k1), axis=0, keepdims=True) + jnp.sum(jnp.abs(k2), axis=0, keepdims=True) + EPS
    kk_ref[0, :, :hw] = (k1 / norm).astype(BF16)
    kk_ref[0, :, hw:] = (k2 / norm).astype(BF16)


def _filter_dft_kernel(hw, kk_ref, cm_ref, sm_ref, kr_ref, ki_ref):
    kk = kk_ref[0]
    gc = jnp.dot(cm_ref[...], kk, preferred_element_type=F32)
    gs = jnp.dot(sm_ref[...], kk, preferred_element_type=F32)
    tf = gc.shape[0]
    row = lax.broadcasted_iota(jnp.int32, (tf, 1), 0)
    second = pl.program_id(1) * tf + row >= kk.shape[0] // 2
    alt = jnp.where(((row & 1) == 0) != second, 1.0, -1.0)
    kr_ref[0] = gc[:, :hw] - alt * gs[:, hw:]
    ki_ref[0] = -(gs[:, :hw] + alt * gc[:, hw:])


def _filters(L, p):
    depth, _, hw2 = p["hy_w3"].shape
    hw = hw2 // 2
    half = LANES // 2
    cm, sm = _dft_tables(L)
    order = _paired_freq_order(L)
    ft = jnp.asarray(_filter_feats(L))
    min_decay = math.log(HY_TARGET) / HY_SLOW
    max_decay = math.log(HY_TARGET) / HY_FAST
    deltas = jnp.asarray(np.abs(np.linspace(min_decay, max_decay, hw)).astype(np.float32)[None, :])

    def pad_to(a, rows, cols):
        return jnp.pad(a, ((0, 0), (0, rows - a.shape[1]), (0, cols - a.shape[2])))

    def block_diag(a, b):
        za = jnp.zeros(a.shape[:2] + (b.shape[2],), F32)
        zb = jnp.zeros(b.shape[:2] + (a.shape[2],), F32)
        return jnp.concatenate([jnp.concatenate([a, za], axis=2), jnp.concatenate([zb, b], axis=2)], axis=1)

    w1 = pad_to(p["hy_w1"], half, half)
    w2 = pad_to(p["hy_w2"], half, half)
    w3 = pad_to(p["hy_w3"], half, hw2)
    w1, w2, w3 = block_diag(w1, w1), block_diag(w2, w2), block_diag(w3[:, :, :hw], w3[:, :, hw:])
    vec = lambda a: jnp.tile(pad_to(a[:, None, :], 1, half), (1, 1, 2))
    lay = lambda shape: pl.BlockSpec((1,) + shape, lambda l: (l, 0, 0))
    kk = pl.pallas_call(
        functools.partial(_filter_taps_kernel, L, hw),
        grid=(depth,),
        in_specs=[
            _const_spec((L, LANES), lambda l: (0, 0)),
            lay((LANES, LANES)), lay((1, LANES)), lay((1, LANES)),
            lay((LANES, LANES)), lay((1, LANES)), lay((1, LANES)),
            lay((LANES, hw2)),
            _const_spec((1, hw), lambda l: (0, 0)),
        ],
        out_specs=lay((L, hw2)),
        out_shape=jax.ShapeDtypeStruct((depth, L, hw2), BF16),
        compiler_params=pltpu.CompilerParams(vmem_limit_bytes=VMEM_LIMIT),
        name=f"hyena_filter_taps_{L}",
    )(ft, w1, vec(p["hy_b1"]), vec(p["hy_freq1"]), w2, vec(p["hy_b2"]), vec(p["hy_freq2"]), w3, deltas)
    tf = 256
    frq = pl.BlockSpec((tf, L), lambda l, j: (j, 0))
    out = pl.BlockSpec((1, tf, hw), lambda l, j: (l, j, 0))
    return pl.pallas_call(
        functools.partial(_filter_dft_kernel, hw),
        grid=(depth, L // tf),
        in_specs=[pl.BlockSpec((1, L, hw2), lambda l, j: (l, 0, 0)), frq, frq],
        out_specs=[out, out],
        out_shape=[jax.ShapeDtypeStruct((depth, L, hw), F32)] * 2,
        compiler_params=pltpu.CompilerParams(vmem_limit_bytes=VMEM_LIMIT),
        name=f"hyena_filter_dft_{L}",
    )(kk, jnp.asarray(cm[order], F32).astype(BF16), jnp.asarray(sm[order], F32).astype(BF16))


def _lower_bound(lb_ref, l):
    prm = lb_ref[...]
    e = jnp.exp(prm - jnp.max(prm, axis=0, keepdims=True))
    s = e / jnp.sum(e, axis=0, keepdims=True)
    acc = jnp.zeros_like(s[0:1])
    for i in range(1, l + 1):
        acc = acc + s[i:i + 1]
    return acc


def _forget_gate(z, lb):
    t = jnp.exp(-jnp.abs(z))
    one_t = 1.0 + t
    pos = z >= 0.0
    num = jnp.where(pos, 1.0 + lb * t, lb + t)
    log_f = jnp.where(num > 0.0, jnp.log(num), z) - jnp.log(one_t)
    return log_f, (1.0 - lb) * (jnp.where(pos, t, 1.0) / one_t)


def _pre_kernel(l, row_w, hg, hy, x_ref, mod_ref, n1_ref, win_ref, lbf_ref, lbb_ref, cw_ref,
                q_ref, kf_ref, kb_ref, lf_ref, lb_ref, v_ref, sg_ref, u_ref, x2_ref, h_scr):
    d = x_ref.shape[1]
    m = mod_ref[0]
    sh1, sc1 = m[:, 0:d], m[:, d:2 * d]
    h_scr[...] = (_rms(x_ref[...]) * n1_ref[0] * (1.0 + sc1) + sh1).astype(BF16)

    w = PROJ_CHUNK
    cw = cw_ref[0]
    lbs = (_lower_bound(lbf_ref, l), _lower_bound(lbb_ref, l))
    conv = lambda r, j, c: _dwconv3(r, cw[:, j * hy + c:j * hy + c + w], row_w)

    tasks = []
    for c in range(0, hg, w):
        cs = slice(c, c + w)

        def q_task(r, cs=cs):
            q_ref[:, cs] = (_silu(r[0]) * (HG_DK ** -0.5)).astype(q_ref.dtype)

        def gate_task(r, cs=cs, i=0):
            g_ref, k_ref = ((lf_ref, kf_ref), (lb_ref, kb_ref))[i]
            log_f, k = _forget_gate(r[0], lbs[i][:, cs])
            g_ref[:, cs], k_ref[:, cs] = log_f, k.astype(k_ref.dtype)

        def v_task(r, cs=cs):
            v_ref[:, cs] = r[0].astype(v_ref.dtype)

        def sg_task(r, cs=cs):
            sg_ref[:, cs] = _silu(r[0]).astype(sg_ref.dtype)

        def u_task(r, cs=cs, c=c):
            u_ref[:, cs] = (conv(r[1], 1, c) * conv(r[0], 0, c)).astype(u_ref.dtype)

        def x2_task(r, cs=cs, c=c):
            x2_ref[:, cs] = conv(r[0], 2, c).astype(x2_ref.dtype)

        tasks += [((c,), q_task), ((hg + c,), gate_task), ((2 * hg + c,), functools.partial(gate_task, i=1)),
                  ((3 * hg + c,), v_task), ((4 * hg + c,), sg_task),
                  ((5 * hg + c, 5 * hg + hy + c), u_task), ((5 * hg + 2 * hy + c,), x2_task)]

    issue = lambda offs: [jnp.dot(h_scr[...], win_ref[0, :, o:o + w], preferred_element_type=F32) for o in offs]
    pending = issue(tasks[0][0])
    for i, (_, consume) in enumerate(tasks):
        ahead = issue(tasks[i + 1][0]) if i + 1 < len(tasks) else None
        consume(pending)
        pending = ahead


def _pre(l, x, mod, mod_row, row_w, seq, tm, p):
    t, d = x.shape
    hg = p["hg_lb_fwd"].shape[1]
    hy = p["hy_d"].shape[1]
    n_in = p["w_in"].shape[2]
    depth = p["w_in"].shape[0]
    tok = lambda w: pl.BlockSpec((tm, w), lambda i: (i, 0))
    out = lambda dt: jax.ShapeDtypeStruct((t, hg), dt)
    return pl.pallas_call(
        functools.partial(_pre_kernel, l, row_w, hg, hy),
        grid=(t // tm,),
        in_specs=[
            tok(d),
            pl.BlockSpec((1, 1, mod.shape[2]), lambda i: (mod_row(l, i * tm // seq), 0, 0)),
            _const_spec((1, 1, d), lambda i: (l, 0, 0)),
            _const_spec((1, d, n_in), lambda i: (l, 0, 0)),
            _const_spec((depth, hg), lambda i: (0, 0)),
            _const_spec((depth, hg), lambda i: (0, 0)),
            _const_spec((1, 3, 3 * hy), lambda i: (l, 0, 0)),
        ],
        out_specs=[tok(hg)] * 9,
        out_shape=[out(BF16)] * 3 + [out(F32)] * 2 + [out(BF16)] * 4,
        scratch_shapes=[pltpu.VMEM((tm, d), BF16)],
        compiler_params=pltpu.CompilerParams(vmem_limit_bytes=VMEM_LIMIT),
        name="pre_mixer",
    )(x, mod, p["norm1_w3"], p["w_in_bf"], p["hg_lb_fwd"], p["hg_lb_bwd"], p["hy_conv_w"])


LEVELS = tuple(CHUNK >> (i + 1) for i in range(N_LEVELS))
SUBLANES = 8
MXU_LEVELS = tuple(s for s in LEVELS if 1 < s < SUBLANES)
LOG2E = 1.4426950408889634


def _scan_tables():
    c = CHUNK
    j = np.arange(c)[:, None]
    i = np.arange(c)[None, :]
    masks = [((j // (2 * s) == i // (2 * s)) & (j % (2 * s) >= s) & (i % (2 * s) < s)).astype(np.float32)
             for s in LEVELS]
    masks.append(np.eye(c, dtype=np.float32))
    mf = np.stack(masks)
    blocks = [np.tril(np.ones((c, c), np.float32))]
    for s in MXU_LEVELS:
        w = np.zeros((c, c), np.float32)
        for t in range(c):
            ref = (t // (2 * s)) * 2 * s + s - 1
            if t % (2 * s) >= s:
                w[t, ref + 1:t + 1] = 1.0
            else:
                w[t, t + 1:ref + 1] = 1.0
        blocks.append(w)
    fwd = np.concatenate(blocks, axis=0)
    bwd = np.concatenate([w[::-1, ::-1] for w in blocks], axis=0)
    sums = np.stack([np.concatenate([fwd] * 3, axis=1), np.concatenate([bwd] * 3, axis=1)])
    return sums, np.stack([mf, mf[:, ::-1, ::-1]])


def _decay_logs(sums, g, d):
    c = g.shape[0]
    fwd = d == 0
    b = sums[:c]
    out = []
    for s in LEVELS:
        if s >= SUBLANES:
            parts = []
            for b0 in range(0, c, 2 * s):
                ref = b0 + s - 1 if fwd else b0 + s
                br, lo, hi = b[ref:ref + 1], b[b0:b0 + s], b[b0 + s:b0 + 2 * s]
                parts += [br - lo, hi - br] if fwd else [lo - br, br - hi]
            out.append(jnp.concatenate(parts, axis=0))
        elif s in MXU_LEVELS:
            i = 1 + MXU_LEVELS.index(s)
            out.append(sums[i * c:(i + 1) * c])
        else:
            r = lax.broadcasted_iota(jnp.int32, (c, 1), 0)
            out.append(jnp.where((r & 1) == (1 if fwd else 0), g, 0.0))
    far = b[c - 1:c] if fwd else b[0:1]
    return out + [b, far - b]


def _scan_kernel(nc, heads, nb, has_s0, want_state, *refs):
    q_ref, kf_ref, kb_ref, lf_ref, lb_ref, v_ref = refs[:6]
    refs = refs[6:]
    if has_s0:
        s0_ref, refs = refs[0], refs[1:]
    sum_ref, m_ref, o_ref = refs[:3]
    refs = refs[3:]
    if want_state:
        st_ref, refs = refs[0], refs[1:]
    s_scr, ob_scr, qs_scr, ks_scr, v_scr, ea_scr = refs
    c = CHUNK
    dirs = ((kf_ref, lf_ref), (kb_ref, lb_ref))
    streams = [(b, d) for b in range(nb) for d in range(2)]
    pairs = [(t, h) for t in range(len(streams)) for h in range(heads)]
    cols = lambda h: slice(h * HG_DK, (h + 1) * HG_DK)

    def rows_of(ci, t):
        b, d = streams[t]
        return pl.ds(b * nc * c + (ci if d == 0 else nc - 1 - ci) * c, c)

    def prep_start(ci):
        out = []
        for t, (b, d) in enumerate(streams):
            k_ref, g_ref = dirs[d]
            rows = rows_of(ci, t)
            g = g_ref[rows, :] * LOG2E
            g1 = g.astype(BF16)
            r1 = g - g1.astype(F32)
            g2 = r1.astype(BF16)
            g3 = (r1 - g2.astype(F32)).astype(BF16)
            sums = jnp.dot(sum_ref[d], jnp.concatenate([g1, g2, g3], axis=0), preferred_element_type=F32)
            out.append((g, q_ref[rows, :], k_ref[rows, :], v_ref[rows, :], sums))
        return out

    def prep_finish(slot, prepped):
        for t, (g, q, k, v, sums) in enumerate(prepped):
            d = streams[t][1]
            e = [jnp.exp2(x) for x in _decay_logs(sums, g, d)]
            eb = [x.astype(BF16) for x in e]
            qb, kb = q.astype(BF16), k.astype(BF16)
            for s in range(N_LEVELS):
                qs_scr[slot, t, s] = qb * eb[s]
                ks_scr[slot, t, s] = kb * eb[s]
            qs_scr[slot, t, N_LEVELS] = qb
            ks_scr[slot, t, N_LEVELS] = kb
            qs_scr[slot, t, N_LEVELS + 1] = qb * eb[N_LEVELS]
            ks_scr[slot, t, N_LEVELS + 1] = kb * eb[N_LEVELS + 1]
            v_scr[slot, t] = v.astype(BF16)
            ea_scr[slot, t, 0:1, :] = e[N_LEVELS][c - 1:c] if d == 0 else e[N_LEVELS][0:1]

    def use_start(slot):
        scores, inter, new_state, vals = {}, {}, {}, {}
        for d, h in pairs:
            scores[d, h] = [_bdot_nt(qs_scr[slot, d, s, :, cols(h)], ks_scr[slot, d, s, :, cols(h)])
                            for s in range(N_LEVELS + 1)]
            st = s_scr[d, h]
            vals[d, h] = v_scr[slot, d, :, cols(h)]
            inter[d, h] = _bdot_nt(qs_scr[slot, d, N_LEVELS + 1, :, cols(h)], st)
            new_state[d, h] = (ea_scr[slot, d, 0:1, cols(h)] * st
                               + _bdot_tn(vals[d, h], ks_scr[slot, d, N_LEVELS + 1, :, cols(h)]))
        return scores, inter, new_state, vals

    def use_finish(ci, scores, inter, new_state, vals):
        for d, (_, direction) in enumerate(streams):
            keep = [m_ref[direction, s] > 0.5 for s in range(N_LEVELS + 1)]
            for h in range(heads):
                a = jnp.where(keep[N_LEVELS], scores[d, h][N_LEVELS], 0.0)
                for s in range(N_LEVELS):
                    a = jnp.where(keep[s], scores[d, h][s], a)
                o = _bdot(a, vals[d, h]) + inter[d, h]
                (o_ref if direction == 0 else ob_scr)[rows_of(ci, d), cols(h)] = o
                s_scr[d, h] = new_state[d, h]

    for t, h in pairs:
        b, d = streams[t]
        s_scr[t, h] = s0_ref[b, 0, d, h].T if has_s0 else jnp.zeros(s_scr.shape[2:], F32)

    prep_finish(0, prep_start(0))
    for ci in range(nc):
        slot = ci % 2
        prepped = prep_start(ci + 1) if ci + 1 < nc else None
        part = use_start(slot)
        if prepped is not None:
            prep_finish(1 - slot, prepped)
        use_finish(ci, *part)
    o_ref[...] += ob_scr[...]
    if want_state:
        for t, h in pairs:
            b, d = streams[t]
            st_ref[b, d, h] = s_scr[t, h].T


def _scan(l, seq, nseq, arrs, s0, want_state):
    t, hg = arrs[0].shape
    heads = hg // HG_DK
    nc = seq // CHUNK
    sums, masks = _scan_tables()
    sums, masks = jnp.asarray(sums, F32).astype(BF16), jnp.asarray(masks)
    nb = max(1, SCAN_ROWS // seq)
    seq_spec = pl.BlockSpec((nb * seq, hg), lambda b: (b, 0))
    in_specs = [seq_spec] * 6
    args = list(arrs)
    if s0 is not None:
        in_specs.append(pl.BlockSpec((nb, 1) + s0.shape[2:], lambda b: (b, l, 0, 0, 0, 0)))
        args.append(s0)
    in_specs += [_const_spec(sums.shape, lambda b: (0, 0, 0)), _const_spec(masks.shape, lambda b: (0, 0, 0, 0))]
    args += [sums, masks]
    out_specs = [seq_spec]
    out_shape = [jax.ShapeDtypeStruct((t, hg), F32)]
    if want_state:
        out_specs.append(pl.BlockSpec((nb, 2, heads, HG_DK, HG_DK), lambda b: (b, 0, 0, 0, 0)))
        out_shape.append(jax.ShapeDtypeStruct((nseq, 2, heads, HG_DK, HG_DK), F32))
    res = pl.pallas_call(
        functools.partial(_scan_kernel, nc, heads, nb, s0 is not None, want_state),
        grid=(nseq // nb,),
        in_specs=in_specs,
        out_specs=out_specs,
        out_shape=out_shape,
        scratch_shapes=[pltpu.VMEM((2 * nb, heads, HG_DK, HG_DK), F32), pltpu.VMEM((nb * seq, hg), F32),
                        pltpu.VMEM((2, 2 * nb, N_LEVELS + 2, CHUNK, hg), BF16),
                        pltpu.VMEM((2, 2 * nb, N_LEVELS + 2, CHUNK, hg), BF16),
                        pltpu.VMEM((2, 2 * nb, CHUNK, hg), BF16), pltpu.VMEM((2, 2 * nb, SUBLANES, hg), F32)],
        compiler_params=pltpu.CompilerParams(vmem_limit_bytes=VMEM_LIMIT),
        name="hgrn2_scan",
    )(*args)
    return res if want_state else (res[0], None)


def _hyena_kernel(L, nsub, u_ref, x2_ref, kr_ref, ki_ref, d_ref, nw_ref, fe_ref, fo_ref, ie_ref, io_ref, y_ref,
                  u_scr, z_scr):
    h = L // 2
    groups = range(u_scr.shape[1])
    lanes = lambda j: slice(j * LANES, (j + 1) * LANES)
    kr, ki = kr_ref[0], ki_ref[0]
    for b in range(nsub):
        rows = slice(b * L, (b + 1) * L)
        u = u_ref[rows, :].astype(F32)
        for j in groups:
            u_scr[b, j] = u[:, lanes(j)]
        parity = lambda par: jnp.concatenate([u_scr[b, j, pl.ds(par, h, stride=2), :] for j in groups], axis=1)
        ue, uo = parity(0).astype(BF16), parity(1).astype(BF16)
        p = jnp.dot(fe_ref[...], ue, preferred_element_type=F32)
        q = jnp.dot(fo_ref[...], uo, preferred_element_type=F32)
        ur = jnp.concatenate([p[:h] + q[:h], p[:h] - q[:h]], axis=0)
        us = jnp.concatenate([p[h:] + q[h:], q[h:] - p[h:]], axis=0)
        zr = ur * kr + us * ki
        zi = ur * ki - us * kr
        even = jnp.concatenate([zr[:h] + zr[h:], zi[:h] - zi[h:]], axis=0).astype(BF16)
        odd = jnp.concatenate([zr[:h] - zr[h:], zi[:h] + zi[h:]], axis=0).astype(BF16)
        for par, (tab_ref, zz) in enumerate(((ie_ref, even), (io_ref, odd))):
            zp = jnp.dot(tab_ref[...], zz, preferred_element_type=F32)
            for j in groups:
                z_scr[b, j, pl.ds(par, h, stride=2), :] = zp[:, lanes(j)]
        z = jnp.concatenate([z_scr[b, j] for j in groups], axis=1)
        y = x2_ref[rows, :] * (z + d_ref[0] * u)
        y_ref[rows, :] = (_rms(y) * nw_ref[0]).astype(y_ref.dtype)


def _hyena(l, seq, nseq, u, x2, kr, ki, p):
    t, hw = u.shape
    h = seq // 2
    cm, sm = _dft_tables(seq)
    tabs = []
    for par in range(2):
        c, s = cm[:h, par::2], sm[:h, par::2]
        tabs.append((np.concatenate([c, s], axis=0), np.concatenate([c.T, -s.T], axis=1) / seq))
    bf = lambda a: jnp.asarray(a, F32).astype(BF16)
    fe, fo, ie, io = bf(tabs[0][0]), bf(tabs[1][0]), bf(tabs[0][1]), bf(tabs[1][1])
    nsub = max(1, HYENA_ROWS // seq)
    seq_spec = pl.BlockSpec((nsub * seq, hw), lambda b: (b, 0))
    lay = lambda shape: _const_spec((1,) + shape, lambda b: (l, 0, 0))
    tab = lambda a: _const_spec(a.shape, lambda b: (0, 0))
    return pl.pallas_call(
        functools.partial(_hyena_kernel, seq, nsub),
        grid=(nseq // nsub,),
        in_specs=[seq_spec, seq_spec, lay((seq, hw)), lay((seq, hw)), lay((1, hw)), lay((1, hw)),
                  tab(fe), tab(fo), tab(ie), tab(io)],
        out_specs=seq_spec,
        out_shape=jax.ShapeDtypeStruct((t, hw), BF16),
        scratch_shapes=[pltpu.VMEM((nsub, hw // LANES, seq, LANES), F32)] * 2,
        compiler_params=pltpu.CompilerParams(vmem_limit_bytes=VMEM_LIMIT),
        name="hyena_conv",
    )(u, x2, kr, ki, p["hy_d3"], p["hy_norm_w3"], fe, fo, ie, io)


def _post_kernel(row_w, heads, ff_chunk, last, x_ref, o_ref, sg_ref, y_ref, mod_ref, hn_ref, wout_ref,
                 n2_ref, wup_ref, cw_ref, wdn_ref, fn_ref, out_ref, x_scr, h2_scr, act_scr):
    d = x_ref.shape[1]
    hg = o_ref.shape[1]
    dff = wdn_ref.shape[1]
    m = mod_ref[0]
    g1, sh2, sc2, g2 = (m[:, i * d:(i + 1) * d] for i in range(2, 6))
    hn = hn_ref[0]
    mix = _bdot(y_ref[...], wout_ref[0, hg:, :])
    for h in range(heads):
        cols = slice(h * HG_DK, (h + 1) * HG_DK)
        mix = mix + _bdot(_rms(o_ref[:, cols]) * hn[:, cols] * sg_ref[:, cols], wout_ref[0, cols, :])
    x_scr[...] = x_ref[...] + g1 * mix
    h2_scr[...] = (_rms(x_scr[...]) * n2_ref[0] * (1.0 + sc2) + sh2).astype(BF16)
    cw = cw_ref[0]

    for j in range(dff // ff_chunk):
        a, b = j * ff_chunk, (j + 1) * ff_chunk
        gate = _dwconv3(jnp.dot(h2_scr[...], wup_ref[0, :, a:b], preferred_element_type=F32),
                        cw[:, a:b], row_w)
        val = _dwconv3(jnp.dot(h2_scr[...], wup_ref[0, :, dff + a:dff + b], preferred_element_type=F32),
                       cw[:, dff + a:dff + b], row_w)
        act_scr[:, a:b] = (_silu(gate) * val).astype(BF16)
    x = x_scr[...] + g2 * jnp.dot(act_scr[...], wdn_ref[0], preferred_element_type=F32)
    out_ref[...] = _rms(x) * fn_ref[...] if last else x


def _post(l, x, o, sg, y, mod, mod_row, row_w, seq, tm, last, p):
    t, d = x.shape
    hg = o.shape[1]
    dff = p["ffn_w_down"].shape[1]
    tok = lambda w: pl.BlockSpec((tm, w), lambda i: (i, 0))
    lay = lambda shape: _const_spec((1,) + shape, lambda i: (l, 0, 0))
    return pl.pallas_call(
        functools.partial(_post_kernel, row_w, hg // HG_DK, 256, last),
        grid=(t // tm,),
        in_specs=[
            tok(d), tok(hg), tok(hg), tok(y.shape[1]),
            pl.BlockSpec((1, 1, mod.shape[2]), lambda i: (mod_row(l, i * tm // seq), 0, 0)),
            lay((1, hg)), lay((d, d)), lay((1, d)), lay((d, 2 * dff)), lay((3, 2 * dff)), lay((dff, d)),
            _const_spec((1, d), lambda i: (0, 0)),
        ],
        out_specs=tok(d),
        out_shape=jax.ShapeDtypeStruct((t, d), F32),
        scratch_shapes=[pltpu.VMEM((tm, d), F32), pltpu.VMEM((tm, d), BF16), pltpu.VMEM((tm, dff), BF16)],
        compiler_params=pltpu.CompilerParams(vmem_limit_bytes=VMEM_LIMIT),
        name="post_mixer_ffn",
    )(x, o, sg, y, mod, p["hg_norm_w3"], p["w_out_bf"], p["norm2_w3"], p["ffn_w_up_bf"],
      p["ffn_conv_w"], p["ffn_w_down_bf"], p["final_norm_w2"])


def _group(x3, mod, mod_row, row_w, s0, want_state, filt, p):
    nseq, seq, d = x3.shape
    depth = p["w_in"].shape[0]
    x = x3.reshape(nseq * seq, d)
    kr, ki = filt
    states = []
    for l in range(depth):
        q, kf, kb, lf, lb, v, sg, u, x2 = _pre(l, x, mod, mod_row, row_w, seq, PRE_TILE, p)
        o, st = _scan(l, seq, nseq, (q, kf, kb, lf, lb, v), s0, want_state)
        y = _hyena(l, seq, nseq, u, x2, kr, ki, p)
        x = _post(l, x, o, sg, y, mod, mod_row, row_w, seq, POST_TILE, l == depth - 1, p)
        states.append(st)
    return x.reshape(nseq, seq, d), states


def kernel(x_prompt, x_sample, state_hgrn, c, c_ctx, w_in, w_out, ada_w, ada_b, norm1_w, norm2_w,
           hg_lb_fwd, hg_lb_bwd, hg_norm_w, hy_conv_w, hy_w1, hy_b1, hy_freq1, hy_w2, hy_b2, hy_freq2,
           hy_w3, hy_d, hy_norm_w, ffn_w_up, ffn_conv_w, ffn_w_down, final_norm_w):
    depth, d, _ = w_in.shape
    vec3 = lambda a: a.reshape(a.shape[0], 1, a.shape[1])
    p = dict(w_in=w_in, w_in_bf=w_in.astype(BF16), w_out_bf=w_out.astype(BF16),
             ffn_w_up_bf=ffn_w_up.astype(BF16), ffn_w_down=ffn_w_down, ffn_w_down_bf=ffn_w_down.astype(BF16),
             norm1_w3=vec3(norm1_w), norm2_w3=vec3(norm2_w), hg_norm_w3=vec3(hg_norm_w),
             hy_d=hy_d, hy_d3=vec3(hy_d), hy_norm_w3=vec3(hy_norm_w), final_norm_w2=final_norm_w[None, :],
             hg_lb_fwd=hg_lb_fwd, hg_lb_bwd=hg_lb_bwd, hy_conv_w=hy_conv_w, ffn_conv_w=ffn_conv_w,
             hy_w1=hy_w1, hy_b1=hy_b1, hy_freq1=hy_freq1, hy_w2=hy_w2, hy_b2=hy_b2, hy_freq2=hy_freq2,
             hy_w3=hy_w3)

    n_dec = c.shape[0]
    cv = jnp.concatenate([c, c_ctx[None, :], jnp.zeros((MOD_ROWS - n_dec - 1, d), F32)], axis=0)
    mod = _modulation(cv, ada_w, ada_b).reshape(depth * MOD_ROWS, 1, 6 * d)

    seq_p, seq_s = x_prompt.shape[1], x_sample.shape[1]
    y_p, states = _group(x_prompt, mod, lambda l, b: l * MOD_ROWS + n_dec, seq_p, None, True,
                         _filters(seq_p, p), p)
    y_s, _ = _group(x_sample, mod, lambda l, b: l * MOD_ROWS + b, GRID_W, state_hgrn, False,
                    _filters(seq_s, p), p)
    return (y_p, y_s, jnp.stack(states, axis=1))
```

```python
import functools
import math

import numpy as np
import jax
import jax.numpy as jnp
from jax import lax
from jax.experimental import pallas as pl
from jax.experimental.pallas import tpu as pltpu

F32 = jnp.float32
BF16 = jnp.bfloat16

HG_DK = 128
CHUNK = 64
GRID_W = 64
HY_BANDS = 16
HY_TARGET = 1e-2
HY_FAST = 0.3
HY_SLOW = 1.5
EPS = 1e-6

LANES = 128
VMEM_LIMIT = 56 * 1024 * 1024

N_LEVELS = int(math.log2(CHUNK))
MOD_ROWS = 16
PRE_TILE = 1024
HYENA_ROWS = 1024
POST_TILE = 512
PROJ_CHUNK = 256


def _bdot(a, b):
    return jnp.dot(a.astype(BF16), b.astype(BF16), preferred_element_type=F32)


def _bdot_nt(a, b):
    return lax.dot_general(a.astype(BF16), b.astype(BF16), (((1,), (1,)), ((), ())),
                           preferred_element_type=F32)


def _bdot_tn(a, b):
    return lax.dot_general(a.astype(BF16), b.astype(BF16), (((0,), (0,)), ((), ())),
                           preferred_element_type=F32)


def _hdot(a, b):
    return jnp.dot(a, b, preferred_element_type=F32, precision=lax.Precision.HIGHEST)


def _silu(x):
    return x * jax.nn.sigmoid(x)


def _rms(x):
    return x * lax.rsqrt(jnp.mean(x * x, axis=-1, keepdims=True) + EPS)


def _const_spec(shape, index_map):
    return pl.BlockSpec(shape, index_map, pipeline_mode=pl.Buffered(1))


def _dwconv3(p, w, row_w):
    n = p.shape[0]
    r = lax.broadcasted_iota(jnp.int32, (n, 1), 0) & (row_w - 1)
    prev = jnp.where(r == 0, 0.0, pltpu.roll(p, 1, 0))
    nxt = jnp.where(r == row_w - 1, 0.0, pltpu.roll(p, n - 1, 0))
    return w[0:1] * prev + w[1:2] * p + w[2:3] * nxt


def _mod_kernel(cv_ref, w_ref, b_ref, o_ref):
    s = _silu(cv_ref[...])
    w = w_ref[0]
    s_hi = s.astype(BF16)
    s_lo = (s - s_hi.astype(F32)).astype(BF16)
    w_hi = w.astype(BF16)
    w_lo = (w - w_hi.astype(F32)).astype(BF16)
    rows = s.shape[0]
    hi = jnp.dot(jnp.concatenate([s_hi, s_lo], axis=0), w_hi, preferred_element_type=F32)
    o_ref[0] = hi[:rows] + hi[rows:] + jnp.dot(s_hi, w_lo, preferred_element_type=F32) + b_ref[0]


def _modulation(cv, ada_w, ada_b):
    depth, d, n = ada_w.shape
    tn = 1536
    return pl.pallas_call(
        _mod_kernel,
        grid=(depth, n // tn),
        in_specs=[
            pl.BlockSpec((MOD_ROWS, d), lambda l, j: (0, 0)),
            pl.BlockSpec((1, d, tn), lambda l, j: (l, 0, j)),
            pl.BlockSpec((1, 1, tn), lambda l, j: (l, 0, j)),
        ],
        out_specs=pl.BlockSpec((1, MOD_ROWS, tn), lambda l, j: (l, 0, j)),
        out_shape=jax.ShapeDtypeStruct((depth, MOD_ROWS, n), F32),
        compiler_params=pltpu.CompilerParams(vmem_limit_bytes=VMEM_LIMIT),
        name="modulation",
    )(cv, ada_w, ada_b.reshape(depth, 1, n))


def _dft_tables(L):
    f = np.arange(L, dtype=np.int64)[:, None]
    t = np.arange(L, dtype=np.int64)[None, :]
    ang = 2.0 * np.pi * (((2 * f + 1) * t) % (4 * L)).astype(np.float64) / (4 * L)
    return np.cos(ang), np.sin(ang)


def _paired_freq_order(L):
    half = np.arange(L // 2)
    return np.concatenate([half, L - 1 - half])


def _filter_feats(L):
    half = LANES // 2

    def feats(pos):
        t = pos / (L - 1)
        bands = np.linspace(1e-4, HY_BANDS - 1, HY_BANDS)[None, :]
        ang = 2.0 * np.pi * pos[:, None] * bands / L
        out = np.zeros((pos.shape[0], half), np.float64)
        out[:, 0] = t
        out[:, 1:1 + HY_BANDS] = np.cos(ang)
        out[:, 1 + HY_BANDS:1 + 2 * HY_BANDS] = np.sin(ang)
        return out
    pos = np.arange(L, dtype=np.float64)
    return np.concatenate([feats(pos), feats(L - pos)], axis=1).astype(np.float32)


def _filter_taps_kernel(L, hw, ft_ref, w1_ref, b1_ref, f1_ref, w2_ref, b2_ref, f2_ref, w3_ref,
                        dl_ref, kk_ref):
    half = LANES // 2
    ft = ft_ref[...]
    h = jnp.sin(f1_ref[0] * (_hdot(ft, w1_ref[0]) + b1_ref[0]))
    h = jnp.sin(f2_ref[0] * (_hdot(h, w2_ref[0]) + b2_ref[0]))
    h = _hdot(h, w3_ref[0])
    dl = dl_ref[...]
    k1 = h[:, :hw] * jnp.exp(-ft[:, 0:1] * dl)
    row = lax.broadcasted_iota(jnp.int32, (L, 1), 0)
    k2 = jnp.where(row == 0, 0.0, -(h[:, hw:] * jnp.exp(-ft[:, half:half + 1] * dl)))
    norm = jnp.sum(jnp.abs(k1), axis=0, keepdims=True) + jnp.sum(jnp.abs(k2), axis=0, keepdims=True) + EPS
    kk_ref[0, :, :hw] = (k1 / norm).astype(BF16)
    kk_ref[0, :, hw:] = (k2 / norm).astype(BF16)


def _filter_dft_kernel(hw, kk_ref, cm_ref, sm_ref, kr_ref, ki_ref):
    kk = kk_ref[0]
    gc = jnp.dot(cm_ref[...], kk, preferred_element_type=F32)
    gs = jnp.dot(sm_ref[...], kk, preferred_element_type=F32)
    tf = gc.shape[0]
    row = lax.broadcasted_iota(jnp.int32, (tf, 1), 0)
    second = pl.program_id(1) * tf + row >= kk.shape[0] // 2
    alt = jnp.where(((row & 1) == 0) != second, 1.0, -1.0)
    kr_ref[0] = gc[:, :hw] - alt * gs[:, hw:]
    ki_ref[0] = -(gs[:, :hw] + alt * gc[:, hw:])


def _filters(L, p):
    depth, _, hw2 = p["hy_w3"].shape
    hw = hw2 // 2
    half = LANES // 2
    cm, sm = _dft_tables(L)
    order = _paired_freq_order(L)
    ft = jnp.asarray(_filter_feats(L))
    min_decay = math.log(HY_TARGET) / HY_SLOW
    max_decay = math.log(HY_TARGET) / HY_FAST
    deltas = jnp.asarray(np.abs(np.linspace(min_decay, max_decay, hw)).astype(np.float32)[None, :])

    def pad_to(a, rows, cols):
        return jnp.pad(a, ((0, 0), (0, rows - a.shape[1]), (0, cols - a.shape[2])))

    def block_diag(a, b):
        za = jnp.zeros(a.shape[:2] + (b.shape[2],), F32)
        zb = jnp.zeros(b.shape[:2] + (a.shape[2],), F32)
        return jnp.concatenate([jnp.concatenate([a, za], axis=2), jnp.concatenate([zb, b], axis=2)], axis=1)

    w1 = pad_to(p["hy_w1"], half, half)
    w2 = pad_to(p["hy_w2"], half, half)
    w3 = pad_to(p["hy_w3"], half, hw2)
    w1, w2, w3 = block_diag(w1, w1), block_diag(w2, w2), block_diag(w3[:, :, :hw], w3[:, :, hw:])
    vec = lambda a: jnp.tile(pad_to(a[:, None, :], 1, half), (1, 1, 2))
    lay = lambda shape: pl.BlockSpec((1,) + shape, lambda l: (l, 0, 0))
    kk = pl.pallas_call(
        functools.partial(_filter_taps_kernel, L, hw),
        grid=(depth,),
        in_specs=[
            _const_spec((L, LANES), lambda l: (0, 0)),
            lay((LANES, LANES)), lay((1, LANES)), lay((1, LANES)),
            lay((LANES, LANES)), lay((1, LANES)), lay((1, LANES)),
            lay((LANES, hw2)),
            _const_spec((1, hw), lambda l: (0, 0)),
        ],
        out_specs=lay((L, hw2)),
        out_shape=jax.ShapeDtypeStruct((depth, L, hw2), BF16),
        compiler_params=pltpu.CompilerParams(vmem_limit_bytes=VMEM_LIMIT),
        name=f"hyena_filter_taps_{L}",
    )(ft, w1, vec(p["hy_b1"]), vec(p["hy_freq1"]), w2, vec(p["hy_b2"]), vec(p["hy_freq2"]), w3, deltas)
    tf = 256
    frq = pl.BlockSpec((tf, L), lambda l, j: (j, 0))
    out = pl.BlockSpec((1, tf, hw), lambda l, j: (l, j, 0))
    return pl.pallas_call(
        functools.partial(_filter_dft_kernel, hw),
        grid=(depth, L // tf),
        in_specs=[pl.BlockSpec((1, L, hw2), lambda l, j: (l, 0, 0)), frq, frq],
        out_specs=[out, out],
        out_shape=[jax.ShapeDtypeStruct((depth, L, hw), F32)] * 2,
        compiler_params=pltpu.CompilerParams(vmem_limit_bytes=VMEM_LIMIT),
        name=f"hyena_filter_dft_{L}",
    )(kk, jnp.asarray(cm[order], F32).astype(BF16), jnp.asarray(sm[order], F32).astype(BF16))


def _lower_bound(lb_ref, l):
    prm = lb_ref[...]
    e = jnp.exp(prm - jnp.max(prm, axis=0, keepdims=True))
    s = e / jnp.sum(e, axis=0, keepdims=True)
    acc = jnp.zeros_like(s[0:1])
    for i in range(1, l + 1):
        acc = acc + s[i:i + 1]
    return acc


def _forget_gate(z, lb):
    t = jnp.exp(-jnp.abs(z))
    one_t = 1.0 + t
    pos = z >= 0.0
    num = jnp.where(pos, 1.0 + lb * t, lb + t)
    log_f = jnp.where(num > 0.0, jnp.log(num), z) - jnp.log(one_t)
    return log_f, (1.0 - lb) * (jnp.where(pos, t, 1.0) / one_t)


def _pre_kernel(l, row_w, hg, hy, x_ref, mod_ref, n1_ref, win_ref, lbf_ref, lbb_ref, cw_ref,
                q_ref, kf_ref, kb_ref, lf_ref, lb_ref, v_ref, sg_ref, u_ref, x2_ref, h_scr):
    d = x_ref.shape[1]
    m = mod_ref[0]
    sh1, sc1 = m[:, 0:d], m[:, d:2 * d]
    h_scr[...] = (_rms(x_ref[...]) * n1_ref[0] * (1.0 + sc1) + sh1).astype(BF16)

    w = PROJ_CHUNK
    cw = cw_ref[0]
    lbs = (_lower_bound(lbf_ref, l), _lower_bound(lbb_ref, l))
    conv = lambda r, j, c: _dwconv3(r, cw[:, j * hy + c:j * hy + c + w], row_w)

    tasks = []
    for c in range(0, hg, w):
        cs = slice(c, c + w)

        def q_task(r, cs=cs):
            q_ref[:, cs] = (_silu(r[0]) * (HG_DK ** -0.5)).astype(q_ref.dtype)

        def gate_task(r, cs=cs, i=0):
            g_ref, k_ref = ((lf_ref, kf_ref), (lb_ref, kb_ref))[i]
            log_f, k = _forget_gate(r[0], lbs[i][:, cs])
            g_ref[:, cs], k_ref[:, cs] = log_f, k.astype(k_ref.dtype)

        def v_task(r, cs=cs):
            v_ref[:, cs] = r[0].astype(v_ref.dtype)

        def sg_task(r, cs=cs):
            sg_ref[:, cs] = _silu(r[0]).astype(sg_ref.dtype)

        def u_task(r, cs=cs, c=c):
            u_ref[:, cs] = (conv(r[1], 1, c) * conv(r[0], 0, c)).astype(u_ref.dtype)

        def x2_task(r, cs=cs, c=c):
            x2_ref[:, cs] = conv(r[0], 2, c).astype(x2_ref.dtype)

        tasks += [((c,), q_task), ((hg + c,), gate_task), ((2 * hg + c,), functools.partial(gate_task, i=1)),
                  ((3 * hg + c,), v_task), ((4 * hg + c,), sg_task),
                  ((5 * hg + c, 5 * hg + hy + c), u_task), ((5 * hg + 2 * hy + c,), x2_task)]

    issue = lambda offs: [jnp.dot(h_scr[...], win_ref[0, :, o:o + w], preferred_element_type=F32) for o in offs]
    pending = issue(tasks[0][0])
    for i, (_, consume) in enumerate(tasks):
        ahead = issue(tasks[i + 1][0]) if i + 1 < len(tasks) else None
        consume(pending)
        pending = ahead


def _pre(l, x, mod, mod_row, row_w, seq, tm, p):
    t, d = x.shape
    hg = p["hg_lb_fwd"].shape[1]
    hy = p["hy_d"].shape[1]
    n_in = p["w_in"].shape[2]
    depth = p["w_in"].shape[0]
    tok = lambda w: pl.BlockSpec((tm, w), lambda i: (i, 0))
    out = lambda dt: jax.ShapeDtypeStruct((t, hg), dt)
    return pl.pallas_call(
        functools.partial(_pre_kernel, l, row_w, hg, hy),
        grid=(t // tm,),
        in_specs=[
            tok(d),
            pl.BlockSpec((1, 1, mod.shape[2]), lambda i: (mod_row(l, i * tm // seq), 0, 0)),
            _const_spec((1, 1, d), lambda i: (l, 0, 0)),
            _const_spec((1, d, n_in), lambda i: (l, 0, 0)),
            _const_spec((depth, hg), lambda i: (0, 0)),
            _const_spec((depth, hg), lambda i: (0, 0)),
            _const_spec((1, 3, 3 * hy), lambda i: (l, 0, 0)),
        ],
        out_specs=[tok(hg)] * 9,
        out_shape=[out(BF16)] * 3 + [out(F32)] * 2 + [out(BF16)] * 4,
        scratch_shapes=[pltpu.VMEM((tm, d), BF16)],
        compiler_params=pltpu.CompilerParams(vmem_limit_bytes=VMEM_LIMIT),
        name="pre_mixer",
    )(x, mod, p["norm1_w3"], p["w_in_bf"], p["hg_lb_fwd"], p["hg_lb_bwd"], p["hy_conv_w"])


LEVELS = tuple(CHUNK >> (i + 1) for i in range(N_LEVELS))
SUBLANES = 8
MXU_LEVELS = tuple(s for s in LEVELS if 1 < s < SUBLANES)
LOG2E = 1.4426950408889634


def _scan_tables():
    c = CHUNK
    j = np.arange(c)[:, None]
    i = np.arange(c)[None, :]
    masks = [((j // (2 * s) == i // (2 * s)) & (j % (2 * s) >= s) & (i % (2 * s) < s)).astype(np.float32)
             for s in LEVELS]
    masks.append(np.eye(c, dtype=np.float32))
    mf = np.stack(masks)
    blocks = [np.tril(np.ones((c, c), np.float32))]
    for s in MXU_LEVELS:
        w = np.zeros((c, c), np.float32)
        for t in range(c):
            ref = (t // (2 * s)) * 2 * s + s - 1
            if t % (2 * s) >= s:
                w[t, ref + 1:t + 1] = 1.0
            else:
                w[t, t + 1:ref + 1] = 1.0
        blocks.append(w)
    fwd = np.concatenate(blocks, axis=0)
    bwd = np.concatenate([w[::-1, ::-1] for w in blocks], axis=0)
    sums = np.stack([np.concatenate([fwd] * 3, axis=1), np.concatenate([bwd] * 3, axis=1)])
    return sums, np.stack([mf, mf[:, ::-1, ::-1]])


def _decay_logs(sums, g, d):
    c = g.shape[0]
    fwd = d == 0
    b = sums[:c]
    out = []
    for s in LEVELS:
        if s >= SUBLANES:
            parts = []
            for b0 in range(0, c, 2 * s):
                ref = b0 + s - 1 if fwd else b0 + s
                br, lo, hi = b[ref:ref + 1], b[b0:b0 + s], b[b0 + s:b0 + 2 * s]
                parts += [br - lo, hi - br] if fwd else [lo - br, br - hi]
            out.append(jnp.concatenate(parts, axis=0))
        elif s in MXU_LEVELS:
            i = 1 + MXU_LEVELS.index(s)
            out.append(sums[i * c:(i + 1) * c])
        else:
            r = lax.broadcasted_iota(jnp.int32, (c, 1), 0)
            out.append(jnp.where((r & 1) == (1 if fwd else 0), g, 0.0))
    far = b[c - 1:c] if fwd else b[0:1]
    return out + [b, far - b]


def _scan_kernel(nc, heads, has_s0, want_state, *refs):
    q_ref, kf_ref, kb_ref, lf_ref, lb_ref, v_ref = refs[:6]
    refs = refs[6:]
    if has_s0:
        s0_ref, refs = refs[0], refs[1:]
    sum_ref, m_ref, o_ref = refs[:3]
    refs = refs[3:]
    if want_state:
        st_ref, refs = refs[0], refs[1:]
    s_scr, ob_scr, qs_scr, ks_scr, v_scr, ea_scr = refs
    c = CHUNK
    dirs = ((kf_ref, lf_ref), (kb_ref, lb_ref))
    pairs = [(d, h) for d in range(2) for h in range(heads)]
    cols = lambda h: slice(h * HG_DK, (h + 1) * HG_DK)

    def rows_of(ci, d):
        return pl.ds((ci if d == 0 else nc - 1 - ci) * c, c)

    def prep_start(ci):
        out = []
        for d, (k_ref, g_ref) in enumerate(dirs):
            rows = rows_of(ci, d)
            g = g_ref[rows, :] * LOG2E
            g1 = g.astype(BF16)
            r1 = g - g1.astype(F32)
            g2 = r1.astype(BF16)
            g3 = (r1 - g2.astype(F32)).astype(BF16)
            sums = jnp.dot(sum_ref[d], jnp.concatenate([g1, g2, g3], axis=0), preferred_element_type=F32)
            out.append((g, q_ref[rows, :], k_ref[rows, :], v_ref[rows, :], sums))
        return out

    def prep_finish(slot, prepped):
        for d, (g, q, k, v, sums) in enumerate(prepped):
            e = [jnp.exp2(x) for x in _decay_logs(sums, g, d)]
            eb = [x.astype(BF16) for x in e]
            qb, kb = q.astype(BF16), k.astype(BF16)
            for s in range(N_LEVELS):
                qs_scr[slot, d, s] = qb * eb[s]
                ks_scr[slot, d, s] = kb * eb[s]
            qs_scr[slot, d, N_LEVELS] = qb * eb[N_LEVELS]
            ks_scr[slot, d, N_LEVELS] = kb * eb[N_LEVELS + 1]
            v_scr[slot, d] = v.astype(BF16)
            ea_scr[slot, d, 0:1, :] = e[N_LEVELS][c - 1:c] if d == 0 else e[N_LEVELS][0:1]

    def use_start(ci, slot):
        scores, inter, new_state, vals = {}, {}, {}, {}
        for d, h in pairs:
            scores[d, h] = [_bdot_nt(qs_scr[slot, d, s, :, cols(h)], ks_scr[slot, d, s, :, cols(h)])
                            for s in range(N_LEVELS)]
            scores[d, h].append(_bdot_nt(q_ref[rows_of(ci, d), cols(h)], dirs[d][0][rows_of(ci, d), cols(h)]))
            st = s_scr[d, h]
            vals[d, h] = v_scr[slot, d, :, cols(h)]
            inter[d, h] = _bdot_nt(qs_scr[slot, d, N_LEVELS, :, cols(h)], st)
            new_state[d, h] = (ea_scr[slot, d, 0:1, cols(h)] * st
                               + _bdot_tn(vals[d, h], ks_scr[slot, d, N_LEVELS, :, cols(h)]))
        return scores, inter, new_state, vals

    def use_finish(ci, scores, inter, new_state, vals):
        for d in range(2):
            keep = [m_ref[d, s] > 0.5 for s in range(N_LEVELS + 1)]
            for h in range(heads):
                a = jnp.where(keep[N_LEVELS], scores[d, h][N_LEVELS], 0.0)
                for s in range(N_LEVELS):
                    a = jnp.where(keep[s], scores[d, h][s], a)
                o = _bdot(a, vals[d, h]) + inter[d, h]
                (o_ref if d == 0 else ob_scr)[rows_of(ci, d), cols(h)] = o
                s_scr[d, h] = new_state[d, h]

    for d, h in pairs:
        s_scr[d, h] = s0_ref[0, 0, d, h].T if has_s0 else jnp.zeros(s_scr.shape[2:], F32)

    prep_finish(0, prep_start(0))
    for ci in range(nc):
        slot = ci % 2
        prepped = prep_start(ci + 1) if ci + 1 < nc else None
        part = use_start(ci, slot)
        if prepped is not None:
            prep_finish(1 - slot, prepped)
        use_finish(ci, *part)
    o_ref[...] += ob_scr[...]
    if want_state:
        for d in range(2):
            for h in range(heads):
                st_ref[0, d, h] = s_scr[d, h].T


def _scan(l, seq, nseq, arrs, s0, want_state):
    t, hg = arrs[0].shape
    heads = hg // HG_DK
    nc = seq // CHUNK
    sums, masks = _scan_tables()
    sums, masks = jnp.asarray(sums, F32).astype(BF16), jnp.asarray(masks)
    seq_spec = pl.BlockSpec((seq, hg), lambda b: (b, 0))
    in_specs = [seq_spec] * 6
    args = list(arrs)
    if s0 is not None:
        in_specs.append(pl.BlockSpec((1, 1) + s0.shape[2:], lambda b: (b, l, 0, 0, 0, 0)))
        args.append(s0)
    in_specs += [_const_spec(sums.shape, lambda b: (0, 0, 0)), _const_spec(masks.shape, lambda b: (0, 0, 0, 0))]
    args += [sums, masks]
    out_specs = [seq_spec]
    out_shape = [jax.ShapeDtypeStruct((t, hg), F32)]
    if want_state:
        out_specs.append(pl.BlockSpec((1, 2, heads, HG_DK, HG_DK), lambda b: (b, 0, 0, 0, 0)))
        out_shape.append(jax.ShapeDtypeStruct((nseq, 2, heads, HG_DK, HG_DK), F32))
    res = pl.pallas_call(
        functools.partial(_scan_kernel, nc, heads, s0 is not None, want_state),
        grid=(nseq,),
        in_specs=in_specs,
        out_specs=out_specs,
        out_shape=out_shape,
        scratch_shapes=[pltpu.VMEM((2, heads, HG_DK, HG_DK), F32), pltpu.VMEM((seq, hg), F32),
                        pltpu.VMEM((2, 2, N_LEVELS + 1, CHUNK, hg), BF16),
                        pltpu.VMEM((2, 2, N_LEVELS + 1, CHUNK, hg), BF16),
                        pltpu.VMEM((2, 2, CHUNK, hg), BF16), pltpu.VMEM((2, 2, SUBLANES, hg), F32)],
        compiler_params=pltpu.CompilerParams(vmem_limit_bytes=VMEM_LIMIT),
        name="hgrn2_scan",
    )(*args)
    return res if want_state else (res[0], None)


def _hyena_kernel(L, nsub, u_ref, x2_ref, kr_ref, ki_ref, d_ref, nw_ref, fe_ref, fo_ref, ie_ref, io_ref, y_ref,
                  u_scr, z_scr):
    h = L // 2
    groups = range(u_scr.shape[1])
    lanes = lambda j: slice(j * LANES, (j + 1) * LANES)
    kr, ki = kr_ref[0], ki_ref[0]
    for b in range(nsub):
        rows = slice(b * L, (b + 1) * L)
        u = u_ref[rows, :].astype(F32)
        for j in groups:
            u_scr[b, j] = u[:, lanes(j)]
        parity = lambda par: jnp.concatenate([u_scr[b, j, pl.ds(par, h, stride=2), :] for j in groups], axis=1)
        ue, uo = parity(0).astype(BF16), parity(1).astype(BF16)
        p = jnp.dot(fe_ref[...], ue, preferred_element_type=F32)
        q = jnp.dot(fo_ref[...], uo, preferred_element_type=F32)
        ur = jnp.concatenate([p[:h] + q[:h], p[:h] - q[:h]], axis=0)
        us = jnp.concatenate([p[h:] + q[h:], q[h:] - p[h:]], axis=0)
        zr = ur * kr + us * ki
        zi = ur * ki - us * kr
        even = jnp.concatenate([zr[:h] + zr[h:], zi[:h] - zi[h:]], axis=0).astype(BF16)
        odd = jnp.concatenate([zr[:h] - zr[h:], zi[:h] + zi[h:]], axis=0).astype(BF16)
        for par, (tab_ref, zz) in enumerate(((ie_ref, even), (io_ref, odd))):
            zp = jnp.dot(tab_ref[...], zz, preferred_element_type=F32)
            for j in groups:
                z_scr[b, j, pl.ds(par, h, stride=2), :] = zp[:, lanes(j)]
        z = jnp.concatenate([z_scr[b, j] for j in groups], axis=1)
        y = x2_ref[rows, :] * (z + d_ref[0] * u)
        y_ref[rows, :] = (_rms(y) * nw_ref[0]).astype(y_ref.dtype)


def _hyena(l, seq, nseq, u, x2, kr, ki, p):
    t, hw = u.shape
    h = seq // 2
    cm, sm = _dft_tables(seq)
    tabs = []
    for par in range(2):
        c, s = cm[:h, par::2], sm[:h, par::2]
        tabs.append((np.concatenate([c, s], axis=0), np.concatenate([c.T, -s.T], axis=1) / seq))
    bf = lambda a: jnp.asarray(a, F32).astype(BF16)
    fe, fo, ie, io = bf(tabs[0][0]), bf(tabs[1][0]), bf(tabs[0][1]), bf(tabs[1][1])
    nsub = max(1, HYENA_ROWS // seq)
    seq_spec = pl.BlockSpec((nsub * seq, hw), lambda b: (b, 0))
    lay = lambda shape: _const_spec((1,) + shape, lambda b: (l, 0, 0))
    tab = lambda a: _const_spec(a.shape, lambda b: (0, 0))
    return pl.pallas_call(
        functools.partial(_hyena_kernel, seq, nsub),
        grid=(nseq // nsub,),
        in_specs=[seq_spec, seq_spec, lay((seq, hw)), lay((seq, hw)), lay((1, hw)), lay((1, hw)),
                  tab(fe), tab(fo), tab(ie), tab(io)],
        out_specs=seq_spec,
        out_shape=jax.ShapeDtypeStruct((t, hw), BF16),
        scratch_shapes=[pltpu.VMEM((nsub, hw // LANES, seq, LANES), F32)] * 2,
        compiler_params=pltpu.CompilerParams(vmem_limit_bytes=VMEM_LIMIT),
        name="hyena_conv",
    )(u, x2, kr, ki, p["hy_d3"], p["hy_norm_w3"], fe, fo, ie, io)


def _post_kernel(row_w, heads, ff_chunk, last, x_ref, o_ref, sg_ref, y_ref, mod_ref, hn_ref, wout_ref,
                 n2_ref, wup_ref, cw_ref, wdn_ref, fn_ref, out_ref, x_scr, h2_scr, act_scr):
    d = x_ref.shape[1]
    hg = o_ref.shape[1]
    dff = wdn_ref.shape[1]
    m = mod_ref[0]
    g1, sh2, sc2, g2 = (m[:, i * d:(i + 1) * d] for i in range(2, 6))
    hn = hn_ref[0]
    mix = _bdot(y_ref[...], wout_ref[0, hg:, :])
    for h in range(heads):
        cols = slice(h * HG_DK, (h + 1) * HG_DK)
        mix = mix + _bdot(_rms(o_ref[:, cols]) * hn[:, cols] * sg_ref[:, cols], wout_ref[0, cols, :])
    x_scr[...] = x_ref[...] + g1 * mix
    h2_scr[...] = (_rms(x_scr[...]) * n2_ref[0] * (1.0 + sc2) + sh2).astype(BF16)
    cw = cw_ref[0]

    for j in range(dff // ff_chunk):
        a, b = j * ff_chunk, (j + 1) * ff_chunk
        gate = _dwconv3(jnp.dot(h2_scr[...], wup_ref[0, :, a:b], preferred_element_type=F32),
                        cw[:, a:b], row_w)
        val = _dwconv3(jnp.dot(h2_scr[...], wup_ref[0, :, dff + a:dff + b], preferred_element_type=F32),
                       cw[:, dff + a:dff + b], row_w)
        act_scr[:, a:b] = (_silu(gate) * val).astype(BF16)
    x = x_scr[...] + g2 * jnp.dot(act_scr[...], wdn_ref[0], preferred_element_type=F32)
    out_ref[...] = _rms(x) * fn_ref[...] if last else x


def _post(l, x, o, sg, y, mod, mod_row, row_w, seq, tm, last, p):
    t, d = x.shape
    hg = o.shape[1]
    dff = p["ffn_w_down"].shape[1]
    tok = lambda w: pl.BlockSpec((tm, w), lambda i: (i, 0))
    lay = lambda shape: _const_spec((1,) + shape, lambda i: (l, 0, 0))
    return pl.pallas_call(
        functools.partial(_post_kernel, row_w, hg // HG_DK, 256, last),
        grid=(t // tm,),
        in_specs=[
            tok(d), tok(hg), tok(hg), tok(y.shape[1]),
            pl.BlockSpec((1, 1, mod.shape[2]), lambda i: (mod_row(l, i * tm // seq), 0, 0)),
            lay((1, hg)), lay((d, d)), lay((1, d)), lay((d, 2 * dff)), lay((3, 2 * dff)), lay((dff, d)),
            _const_spec((1, d), lambda i: (0, 0)),
        ],
        out_specs=tok(d),
        out_shape=jax.ShapeDtypeStruct((t, d), F32),
        scratch_shapes=[pltpu.VMEM((tm, d), F32), pltpu.VMEM((tm, d), BF16), pltpu.VMEM((tm, dff), BF16)],
        compiler_params=pltpu.CompilerParams(vmem_limit_bytes=VMEM_LIMIT),
        name="post_mixer_ffn",
    )(x, o, sg, y, mod, p["hg_norm_w3"], p["w_out_bf"], p["norm2_w3"], p["ffn_w_up_bf"],
      p["ffn_conv_w"], p["ffn_w_down_bf"], p["final_norm_w2"])


def _group(x3, mod, mod_row, row_w, s0, want_state, filt, p):
    nseq, seq, d = x3.shape
    depth = p["w_in"].shape[0]
    x = x3.reshape(nseq * seq, d)
    kr, ki = filt
    states = []
    for l in range(depth):
        q, kf, kb, lf, lb, v, sg, u, x2 = _pre(l, x, mod, mod_row, row_w, seq, PRE_TILE, p)
        o, st = _scan(l, seq, nseq, (q, kf, kb, lf, lb, v), s0, want_state)
        y = _hyena(l, seq, nseq, u, x2, kr, ki, p)
        x = _post(l, x, o, sg, y, mod, mod_row, row_w, seq, POST_TILE, l == depth - 1, p)
        states.append(st)
    return x.reshape(nseq, seq, d), states


def kernel(x_prompt, x_sample, state_hgrn, c, c_ctx, w_in, w_out, ada_w, ada_b, norm1_w, norm2_w,
           hg_lb_fwd, hg_lb_bwd, hg_norm_w, hy_conv_w, hy_w1, hy_b1, hy_freq1, hy_w2, hy_b2, hy_freq2,
           hy_w3, hy_d, hy_norm_w, ffn_w_up, ffn_conv_w, ffn_w_down, final_norm_w):
    depth, d, _ = w_in.shape
    vec3 = lambda a: a.reshape(a.shape[0], 1, a.shape[1])
    p = dict(w_in=w_in, w_in_bf=w_in.astype(BF16), w_out_bf=w_out.astype(BF16),
             ffn_w_up_bf=ffn_w_up.astype(BF16), ffn_w_down=ffn_w_down, ffn_w_down_bf=ffn_w_down.astype(BF16),
             norm1_w3=vec3(norm1_w), norm2_w3=vec3(norm2_w), hg_norm_w3=vec3(hg_norm_w),
             hy_d=hy_d, hy_d3=vec3(hy_d), hy_norm_w3=vec3(hy_norm_w), final_norm_w2=final_norm_w[None, :],
             hg_lb_fwd=hg_lb_fwd, hg_lb_bwd=hg_lb_bwd, hy_conv_w=hy_conv_w, ffn_conv_w=ffn_conv_w,
             hy_w1=hy_w1, hy_b1=hy_b1, hy_freq1=hy_freq1, hy_w2=hy_w2, hy_b2=hy_b2, hy_freq2=hy_freq2,
             hy_w3=hy_w3)

    n_dec = c.shape[0]
    cv = jnp.concatenate([c, c_ctx[None, :], jnp.zeros((MOD_ROWS - n_dec - 1, d), F32)], axis=0)
    mod = _modulation(cv, ada_w, ada_b).reshape(depth * MOD_ROWS, 1, 6 * d)

    seq_p, seq_s = x_prompt.shape[1], x_sample.shape[1]
    y_p, states = _group(x_prompt, mod, lambda l, b: l * MOD_ROWS + n_dec, seq_p, None, True,
                         _filters(seq_p, p), p)
    y_s, _ = _group(x_sample, mod, lambda l, b: l * MOD_ROWS + b, GRID_W, state_hgrn, False,
                    _filters(seq_s, p), p)
    return (y_p, y_s, jnp.stack(states, axis=1))
```
